```python
import jax, jax.numpy as jnp
from jax import lax
import numpy as np

D_MODEL = 1024
BATCH = 4
SEQ = 8192
DEPTH = 4

N_CONV_LAYERS = DEPTH // 2
N_ATTN_LAYERS = DEPTH - N_CONV_LAYERS
CONV_WIDTH = 3
N_HEADS = 16
HEAD_DIM = D_MODEL // N_HEADS
Q_BLOCK = 128
N_EXPERTS = 32
TOP_K = 4
D_EXPERT = D_MODEL
SWIGLU_LIMIT = 7.0
SWIGLU_ALPHA = 1.702
EXPERT_BLOCK = 128
DEEPNORM_ALPHA = (2.0 * DEPTH) ** 0.25
DEEPNORM_BETA = (8.0 * DEPTH) ** -0.25
LN_EPS = 1e-5
ADA_SCALE = 0.1
FORGET_BIAS_INIT = 3.0

kernel_name = "yoco_shortconv_fox_moe_deepnorm"


def layer_norm(x, g, b):
    xf = x.astype(jnp.float32)
    mu = jnp.mean(xf, axis=-1, keepdims=True)
    var = jnp.mean(jnp.square(xf - mu), axis=-1, keepdims=True)
    y = (xf - mu) * lax.rsqrt(var + LN_EPS) * g.astype(jnp.float32) + b.astype(jnp.float32)
    return y.astype(x.dtype)


def ada_params(cond, w, b):
    return jnp.split(cond @ w + b, 3, axis=-1)


def modulate(x, shift, scale):
    return x * (1 + scale[:, None, :]) + shift[:, None, :]


def deepnorm_residual(x, sub, gate, g, b):
    return layer_norm(DEEPNORM_ALPHA * x + (1 + gate[:, None, :]) * sub, g, b)


def short_conv_mixer(h, w_in, w_conv, w_out):
    gate_c, gate_b, u = jnp.split(h @ w_in, 3, axis=-1)
    z = gate_c * u
    z = lax.conv_general_dilated(
        z, w_conv[:, None, :].astype(z.dtype), window_strides=(1,),
        padding=[(CONV_WIDTH - 1, 0)], dimension_numbers=("NWC", "WIO", "NWC"),
        feature_group_count=D_MODEL)
    return (gate_b * z) @ w_out


def shared_kv(x, cond, kv_ada_w, kv_ada_b, w_kvf, b_f):
    bsz, seq, _ = x.shape
    shift, scale = jnp.split(cond @ kv_ada_w + kv_ada_b, 2, axis=-1)
    kvf = modulate(x, shift, scale) @ w_kvf
    k = kvf[..., :D_MODEL].reshape(bsz, seq, N_HEADS, HEAD_DIM).transpose(0, 2, 1, 3)
    v = kvf[..., D_MODEL:2 * D_MODEL].reshape(bsz, seq, N_HEADS, HEAD_DIM).transpose(0, 2, 1, 3)
    log_f = jax.nn.log_sigmoid(kvf[..., 2 * D_MODEL:].astype(jnp.float32) + b_f.astype(jnp.float32))
    log_f_cum = jnp.cumsum(log_f, axis=1).transpose(0, 2, 1)
    return k, v, log_f_cum


def forgetting_attention(h, w_q, w_o, k, v, log_f_cum):
    bsz, seq, _ = h.shape
    nb = seq // Q_BLOCK
    q = (h @ w_q).reshape(bsz, nb, Q_BLOCK, N_HEADS, HEAD_DIM).transpose(1, 0, 3, 2, 4)
    dq = log_f_cum.reshape(bsz, N_HEADS, nb, Q_BLOCK).transpose(2, 0, 1, 3)
    k_pos = jnp.arange(seq)
    scale = HEAD_DIM ** -0.5

    def block(args):
        qb, dqb, i = args
        q_pos = i * Q_BLOCK + jnp.arange(Q_BLOCK)
        s = jnp.einsum('bhqd,bhkd->bhqk', qb, k, preferred_element_type=jnp.float32) * scale
        s = s + (dqb[..., :, None] - log_f_cum[..., None, :])
        s = jnp.where(q_pos[:, None] >= k_pos[None, :], s, -jnp.inf)
        p = jax.nn.softmax(s, axis=-1)
        return jnp.einsum('bhqk,bhkd->bhqd', p.astype(v.dtype), v)

    o = lax.map(block, (q, dq, jnp.arange(nb)))
    o = o.transpose(1, 0, 3, 2, 4).reshape(bsz, seq, D_MODEL)
    return o @ w_o


def routed_moe(h, w_r, b_r, w_gu, b_gu, w_down, b_down):
    n_tok = h.shape[0]
    n_assign = n_tok * TOP_K
    logits = jnp.dot(h, w_r, preferred_element_type=jnp.float32) + b_r.astype(jnp.float32)
    top_val, top_idx = lax.top_k(logits, TOP_K)
    top_w = jax.nn.softmax(top_val, axis=-1)
    flat_e = top_idx.reshape(-1)
    flat_tok = jnp.arange(n_assign, dtype=jnp.int32) // TOP_K
    order = jnp.argsort(flat_e, stable=True)
    sorted_e = flat_e[order]
    counts = jnp.bincount(flat_e, length=N_EXPERTS)
    padded = (counts + EXPERT_BLOCK - 1) // EXPERT_BLOCK * EXPERT_BLOCK
    padded_end = jnp.cumsum(padded)
    padded_start = padded_end - padded
    group_start = jnp.cumsum(counts) - counts
    dest = padded_start[sorted_e] + jnp.arange(n_assign, dtype=jnp.int32) - group_start[sorted_e]
    n_blocks = -(-n_assign // EXPERT_BLOCK) + N_EXPERTS
    n_slots = n_blocks * EXPERT_BLOCK
    slot_tok = jnp.full((n_slots,), n_tok, jnp.int32).at[dest].set(flat_tok[order])
    slot_w = jnp.zeros((n_slots,), jnp.float32).at[dest].set(top_w.reshape(-1)[order])
    block_expert = jnp.minimum(
        jnp.searchsorted(padded_end, jnp.arange(n_blocks) * EXPERT_BLOCK, side='right'), N_EXPERTS - 1)
    h_pad = jnp.concatenate([h, jnp.zeros((1, h.shape[1]), h.dtype)], axis=0)

    def expert_block(args):
        tok, e, wt = args
        gu = h_pad[tok] @ w_gu[e] + b_gu[e]
        g = jnp.minimum(gu[:, :D_EXPERT], SWIGLU_LIMIT)
        u = jnp.clip(gu[:, D_EXPERT:], -SWIGLU_LIMIT, SWIGLU_LIMIT)
        a = g * jax.nn.sigmoid(SWIGLU_ALPHA * g) * (u + 1)
        y = a @ w_down[e] + b_down[e]
        return y * wt[:, None].astype(y.dtype)

    y = lax.map(expert_block, (slot_tok.reshape(n_blocks, EXPERT_BLOCK), block_expert,
                               slot_w.reshape(n_blocks, EXPERT_BLOCK)))
    return jax.ops.segment_sum(y.reshape(n_slots, -1), slot_tok, num_segments=n_tok + 1)[:n_tok]


def setup_inputs(seed: int = 0) -> dict:
    key = jax.random.key(seed)
    ks = jax.random.split(key, 24)
    D, H, E, F = D_MODEL, N_HEADS, N_EXPERTS, D_EXPERT
    nrm = lambda k, shape, s: jax.random.normal(k, shape, jnp.float32) * s
    w_kvf = jnp.concatenate([
        nrm(ks[6], (D, D), D ** -0.5),
        nrm(ks[7], (D, D), D ** -0.5 * DEEPNORM_BETA),
        nrm(ks[8], (D, H), D ** -0.5)], axis=1)
    return {
        "x": nrm(ks[0], (BATCH, SEQ, D), 1.0),
        "c": nrm(ks[1], (BATCH, D), 1.0),
        "conv_w_in": nrm(ks[2], (N_CONV_LAYERS, D, 3 * D), D ** -0.5),
        "conv_w": nrm(ks[3], (N_CONV_LAYERS, CONV_WIDTH, D), CONV_WIDTH ** -0.5),
        "conv_w_out": nrm(ks[4], (N_CONV_LAYERS, D, D), D ** -0.5 * DEEPNORM_BETA),
        "kv_ada_w": nrm(ks[5], (D, 2 * D), ADA_SCALE * D ** -0.5),
        "kv_ada_b": nrm(ks[9], (2 * D,), 0.01),
        "w_kvf": w_kvf,
        "b_f": FORGET_BIAS_INIT + nrm(ks[10], (H,), 0.1),
        "attn_w_q": nrm(ks[11], (N_ATTN_LAYERS, D, D), D ** -0.5),
        "attn_w_o": nrm(ks[12], (N_ATTN_LAYERS, D, D), D ** -0.5 * DEEPNORM_BETA),
        "ada_w": nrm(ks[13], (DEPTH, 2, D, 3 * D), ADA_SCALE * D ** -0.5),
        "ada_b": nrm(ks[14], (DEPTH, 2, 3 * D), 0.01),
        "ln_g": 1.0 + nrm(ks[15], (DEPTH, 2, D), 0.02),
        "ln_b": nrm(ks[16], (DEPTH, 2, D), 0.02),
        "router_w": nrm(ks[17], (DEPTH, D, E), D ** -0.5),
        "router_b": nrm(ks[18], (DEPTH, E), 0.01),
        "exp_w_gu": nrm(ks[19], (DEPTH, E, D, 2 * F), D ** -0.5),
        "exp_b_gu": nrm(ks[20], (DEPTH, E, 2 * F), 0.01),
        "exp_w_down": nrm(ks[21], (DEPTH, E, F, D), F ** -0.5 * DEEPNORM_BETA),
        "exp_b_down": nrm(ks[22], (DEPTH, E, D), 0.01),
    }


def reference(x, c, conv_w_in, conv_w, conv_w_out, kv_ada_w, kv_ada_b, w_kvf, b_f,
              attn_w_q, attn_w_o, ada_w, ada_b, ln_g, ln_b, router_w, router_b,
              exp_w_gu, exp_b_gu, exp_w_down, exp_b_down):
    cond = jax.nn.silu(c)
    k = v = log_f_cum = None
    for l in range(DEPTH):
        shift, scale, gate = ada_params(cond, ada_w[l, 0], ada_b[l, 0])
        h = modulate(x, shift, scale)
        if l < N_CONV_LAYERS:
            sub = short_conv_mixer(h, conv_w_in[l], conv_w[l], conv_w_out[l])
        else:
            j = l - N_CONV_LAYERS
            sub = forgetting_attention(h, attn_w_q[j], attn_w_o[j], k, v, log_f_cum)
        x = deepnorm_residual(x, sub, gate, ln_g[l, 0], ln_b[l, 0])
        shift, scale, gate = ada_params(cond, ada_w[l, 1], ada_b[l, 1])
        h = modulate(x, shift, scale)
        sub = routed_moe(h.reshape(-1, D_MODEL), router_w[l], router_b[l], exp_w_gu[l],
                         exp_b_gu[l], exp_w_down[l], exp_b_down[l]).reshape(x.shape)
        x = deepnorm_residual(x, sub, gate, ln_g[l, 1], ln_b[l, 1])
        if l == N_CONV_LAYERS - 1:
            k, v, log_f_cum = shared_kv(x, cond, kv_ada_w, kv_ada_b, w_kvf, b_f)
    return x
```

```python
import functools

import jax
import jax.numpy as jnp
from jax import lax
from jax.experimental import pallas as pl
from jax.experimental.pallas import tpu as pltpu

N_HEADS = 16
TOP_K = 4
SWIGLU_LIMIT = 7.0
SWIGLU_ALPHA = 1.702
LN_EPS = 1e-5
LANES = 128
HEAD_PAIR = 2
SEQ_TILE = 512
MOE_TILE = 512
ATTN_TILE = 512
VMEM_LIMIT = 56 * 1024 * 1024

F32 = jnp.float32
BF16 = jnp.bfloat16
NEG_INF = float("-inf")


def _params(*sem):
    return pltpu.CompilerParams(dimension_semantics=sem, vmem_limit_bytes=VMEM_LIMIT)


def _layer_norm(r, g, b):
    mu = jnp.mean(r, axis=-1, keepdims=True)
    d = r - mu
    var = jnp.mean(d * d, axis=-1, keepdims=True)
    return d * lax.rsqrt(var + LN_EPS) * g + b


def _modulate(x, mod_ref):
    return x * (1.0 + mod_ref[0, 1:2, :]) + mod_ref[0, 0:1, :]


def _ada_kernel(c_ref, w_ref, b_ref, o_ref):
    c = c_ref[...]
    cond = c * jax.nn.sigmoid(c)
    o_ref[0] = jnp.dot(cond, w_ref[0], precision=lax.Precision.HIGHEST,
                       preferred_element_type=F32) + b_ref[0]


def _ada_params(c_pad, w, b):
    g, d, n = w.shape
    tn = 1024 if n % 1024 == 0 else n
    return pl.pallas_call(
        _ada_kernel,
        grid=(g, n // tn),
        in_specs=[pl.BlockSpec((8, d), lambda i, j: (0, 0)),
                  pl.BlockSpec((1, d, tn), lambda i, j: (i, 0, j)),
                  pl.BlockSpec((1, 1, tn), lambda i, j: (i, 0, j))],
        out_specs=pl.BlockSpec((1, 8, tn), lambda i, j: (i, 0, j)),
        out_shape=jax.ShapeDtypeStruct((g, 8, n), F32),
        compiler_params=_params("arbitrary", "arbitrary"),
        name="ada_params",
    )(c_pad, w, b)


def _conv_layer_kernel(x_ref, mod_ref, win_ref, wc_ref, wout_ref, g_ref, b_ref, o_ref,
                       carry_ref, a_ref, *, alpha, col_chunk):
    ts, d = x_ref.shape[1], x_ref.shape[2]

    @pl.when(pl.program_id(1) == 0)
    def _():
        carry_ref[...] = jnp.zeros_like(carry_ref)

    x = x_ref[0]
    h = _modulate(x, mod_ref).astype(BF16)
    row = lax.broadcasted_iota(jnp.int32, (ts, col_chunk), 0)
    for c in range(0, d, col_chunk):
        gate_c = jnp.dot(h, win_ref[:, c:c + col_chunk], preferred_element_type=F32)
        u = jnp.dot(h, win_ref[:, 2 * d + c:2 * d + c + col_chunk], preferred_element_type=F32)
        z = gate_c * u
        z_m1 = carry_ref[1:2, c:c + col_chunk]
        z_m2 = carry_ref[0:1, c:c + col_chunk]
        z1 = jnp.where(row == 0, z_m1, pltpu.roll(z, 1, 0))
        z2 = jnp.where(row == 0, z_m2, jnp.where(row == 1, z_m1, pltpu.roll(z, 2, 0)))
        carry_ref[0:2, c:c + col_chunk] = z[ts - 2:ts, :]
        conv = (wc_ref[0:1, c:c + col_chunk] * z2 + wc_ref[1:2, c:c + col_chunk] * z1
                + wc_ref[2:3, c:c + col_chunk] * z)
        gate_b = jnp.dot(h, win_ref[:, d + c:d + c + col_chunk], preferred_element_type=F32)
        a_ref[:, c:c + col_chunk] = (gate_b * conv).astype(BF16)
    y = jnp.dot(a_ref[...], wout_ref[...], preferred_element_type=F32)
    r = alpha * x + (1.0 + mod_ref[0, 2:3, :]) * y
    o_ref[0] = _layer_norm(r, g_ref[...], b_ref[...])


def _conv_layer(x, mod, w_in, w_conv, w_out, ln_g, ln_b, alpha):
    bsz, seq, d = x.shape
    ts = SEQ_TILE
    kern = functools.partial(_conv_layer_kernel, alpha=alpha, col_chunk=256)
    return pl.pallas_call(
        kern,
        grid=(bsz, seq // ts),
        in_specs=[pl.BlockSpec((1, ts, d), lambda b, j: (b, j, 0)),
                  pl.BlockSpec((1, 3, d), lambda b, j: (b, 0, 0)),
                  pl.BlockSpec((d, 3 * d), lambda b, j: (0, 0)),
                  pl.BlockSpec((3, d), lambda b, j: (0, 0)),
                  pl.BlockSpec((d, d), lambda b, j: (0, 0)),
                  pl.BlockSpec((1, d), lambda b, j: (0, 0)),
                  pl.BlockSpec((1, d), lambda b, j: (0, 0))],
        out_specs=pl.BlockSpec((1, ts, d), lambda b, j: (b, j, 0)),
        out_shape=jax.ShapeDtypeStruct(x.shape, F32),
        scratch_shapes=[pltpu.VMEM((8, d), F32), pltpu.VMEM((ts, d), BF16)],
        compiler_params=_params("arbitrary", "arbitrary"),
        name="conv_layer",
    )(x, mod, w_in, w_conv, w_out, ln_g, ln_b)


def _pack_bf16_pairs(h):
    half = h.shape[1] // 2
    lo = lax.bitcast_convert_type(h[:, :half].astype(BF16).astype(F32), jnp.int32)
    hi = lax.bitcast_convert_type(h[:, half:].astype(BF16).astype(F32), jnp.int32)
    return lax.shift_right_logical(lo, 16) | hi


def _unpack_bf16_pairs(w):
    lo = lax.bitcast_convert_type(w << 16, F32).astype(BF16)
    hi = lax.bitcast_convert_type(w & jnp.int32(-65536), F32).astype(BF16)
    return lo, hi


def _router_kernel(x_ref, mod_ref, wr_ref, br_ref, hpk_ref, route_ref, cnt_ref, carry_ref):
    tr = x_ref.shape[0]

    @pl.when(pl.program_id(0) == 0)
    def _():
        carry_ref[...] = jnp.zeros_like(carry_ref)

    h = _modulate(x_ref[...], mod_ref)
    hpk_ref[...] = _pack_bf16_pairs(h)
    logits = jnp.dot(h, wr_ref[...], precision=lax.Precision.HIGHEST,
                     preferred_element_type=F32) + br_ref[...]
    lane = lax.broadcasted_iota(jnp.int32, (tr, LANES), 1)
    lane_f = lane.astype(F32)
    work = logits
    vals, idxs, sels = [], [], []
    for _ in range(TOP_K):
        m = jnp.max(work, axis=-1, keepdims=True)
        idx = jnp.min(jnp.where(work == m, lane_f, float(LANES)), axis=-1, keepdims=True)
        sel = lane_f == idx
        vals.append(m)
        idxs.append(idx.astype(jnp.int32))
        sels.append(sel)
        work = jnp.where(sel, NEG_INF, work)
    exps = [jnp.exp(v - vals[0]) for v in vals]
    denom = exps[0] + exps[1] + exps[2] + exps[3]
    chosen = sels[0] | sels[1] | sels[2] | sels[3]
    onehot = jnp.where(chosen, 1.0, 0.0).astype(BF16)
    r_i = lax.broadcasted_iota(jnp.int32, (tr, tr), 0)
    c_i = lax.broadcasted_iota(jnp.int32, (tr, tr), 1)
    tri = jnp.where(c_i < r_i, 1.0, 0.0).astype(BF16)
    before = jnp.dot(tri, onehot, preferred_element_type=F32) + carry_ref[0:1, :]
    out = jnp.zeros((tr, LANES), jnp.int32)
    for k in range(TOP_K):
        rank = jnp.sum(jnp.where(sels[k], before, 0.0), axis=-1, keepdims=True).astype(jnp.int32)
        wgt = lax.bitcast_convert_type(exps[k] / denom, jnp.int32)
        out = jnp.where(lane == k, idxs[k], out)
        out = jnp.where(lane == TOP_K + k, rank, out)
        out = jnp.where(lane == 2 * TOP_K + k, wgt, out)
    route_ref[...] = out
    carry_ref[0:1, :] = carry_ref[0:1, :] + jnp.sum(onehot.astype(F32), axis=0, keepdims=True)
    cnt_ref[...] = carry_ref[...].astype(jnp.int32)


def _router(x2, mod, w_r, b_r, seq):
    t, d = x2.shape
    tr = SEQ_TILE
    per_b = seq // tr
    return pl.pallas_call(
        _router_kernel,
        grid=(t // tr,),
        in_specs=[pl.BlockSpec((tr, d), lambda i: (i, 0)),
                  pl.BlockSpec((1, 3, d), lambda i: (i // per_b, 0, 0)),
                  pl.BlockSpec((d, LANES), lambda i: (0, 0)),
                  pl.BlockSpec((1, LANES), lambda i: (0, 0))],
        out_specs=[pl.BlockSpec((tr, d // 2), lambda i: (i, 0)),
                   pl.BlockSpec((tr, LANES), lambda i: (i, 0)),
                   pl.BlockSpec((8, LANES), lambda i: (0, 0))],
        out_shape=[jax.ShapeDtypeStruct((t, d // 2), jnp.int32),
                   jax.ShapeDtypeStruct((t, LANES), jnp.int32),
                   jax.ShapeDtypeStruct((8, LANES), jnp.int32)],
        scratch_shapes=[pltpu.VMEM((8, LANES), F32)],
        compiler_params=_params("arbitrary"),
        name="moe_router",
    )(x2, mod, w_r, b_r)


def _row_copy(src_hbm, dst_vmem, sem, src_row, dst_row):
    return pltpu.make_async_copy(src_hbm.at[pl.ds(src_row, 1)], dst_vmem.at[pl.ds(dst_row, 1)], sem)


def _moe_gemm_kernel(te_ref, nused_ref, tok_ref, hpk_hbm, wgu_ref, bgu_ref, wd_ref, bd_ref,
                     y_ref, xbuf, sem):
    tm = xbuf.shape[0]
    half = xbuf.shape[1]
    f = wd_ref.shape[1]

    @pl.when(pl.program_id(0) < nused_ref[0])
    def _():
        def issue(r, carry):
            _row_copy(hpk_hbm, xbuf, sem, tok_ref[0, 0, r], r).start()
            return carry
        lax.fori_loop(0, tm, issue, 0, unroll=8)

        def wait(r, carry):
            _row_copy(hpk_hbm, xbuf, sem, 0, r).wait()
            return carry
        lax.fori_loop(0, tm, wait, 0, unroll=8)

        lo, hi = _unpack_bf16_pairs(xbuf[...])
        gu = (jnp.dot(lo, wgu_ref[0, :half, :], preferred_element_type=F32)
              + jnp.dot(hi, wgu_ref[0, half:, :], preferred_element_type=F32) + bgu_ref[0])
        g = jnp.minimum(gu[:, :f], SWIGLU_LIMIT)
        u = jnp.clip(gu[:, f:], -SWIGLU_LIMIT, SWIGLU_LIMIT)
        a = g * jax.nn.sigmoid(SWIGLU_ALPHA * g) * (u + 1.0)
        y_ref[...] = jnp.dot(a.astype(BF16), wd_ref[0], preferred_element_type=F32) + bd_ref[0]

    @pl.when(pl.program_id(0) >= nused_ref[0])
    def _():
        y_ref[...] = jnp.zeros_like(y_ref)


def _moe_gemm(tile_expert, n_used, slot_tok, hpk, w_gu, b_gu, w_down, b_down):
    n_tiles = tile_expert.shape[0]
    tm = MOE_TILE
    e, d, f2 = w_gu.shape
    f = f2 // 2
    last = lambda i, te, nu: jnp.minimum(i, nu[0] - 1)
    grid_spec = pltpu.PrefetchScalarGridSpec(
        num_scalar_prefetch=2,
        grid=(n_tiles,),
        in_specs=[pl.BlockSpec((1, 1, tm), lambda i, te, nu: (last(i, te, nu), 0, 0),
                               memory_space=pltpu.SMEM),
                  pl.BlockSpec(memory_space=pl.ANY),
                  pl.BlockSpec((1, d, f2), lambda i, te, nu: (te[i], 0, 0)),
                  pl.BlockSpec((1, 1, f2), lambda i, te, nu: (te[i], 0, 0)),
                  pl.BlockSpec((1, f, d), lambda i, te, nu: (te[i], 0, 0)),
                  pl.BlockSpec((1, 1, d), lambda i, te, nu: (te[i], 0, 0))],
        out_specs=pl.BlockSpec((tm, d), lambda i, te, nu: (i, 0)),
        scratch_shapes=[pltpu.VMEM((tm, d // 2), jnp.int32), pltpu.SemaphoreType.DMA(())],
    )
    return pl.pallas_call(
        _moe_gemm_kernel,
        grid_spec=grid_spec,
        out_shape=jax.ShapeDtypeStruct((n_tiles * tm, d), F32),
        compiler_params=_params("arbitrary"),
        name="moe_experts",
    )(tile_expert, n_used, slot_tok, hpk, w_gu, b_gu, w_down, b_down)


def _combine_kernel(dest_ref, y_hbm, x_ref, mod_ref, route_ref, g_ref, b_ref, o_ref, ybuf, sem, *, alpha):
    tr = x_ref.shape[0]

    def issue(r, carry):
        for k in range(TOP_K):
            _row_copy(y_hbm, ybuf.at[k], sem, dest_ref[0, 0, r * TOP_K + k], r).start()
        return carry
    lax.fori_loop(0, tr, issue, 0, unroll=4)

    def wait(r, carry):
        for k in range(TOP_K):
            _row_copy(y_hbm, ybuf.at[k], sem, 0, r).wait()
        return carry
    lax.fori_loop(0, tr, wait, 0, unroll=4)

    wts = lax.bitcast_convert_type(route_ref[...], F32)
    sub = wts[:, 2 * TOP_K:2 * TOP_K + 1] * ybuf[0]
    for k in range(1, TOP_K):
        sub = sub + wts[:, 2 * TOP_K + k:2 * TOP_K + k + 1] * ybuf[k]
    x = x_ref[...]
    r = alpha * x + (1.0 + mod_ref[0, 2:3, :]) * sub
    o_ref[...] = _layer_norm(r, g_ref[...], b_ref[...])


def _combine(dest, y, x2, mod, route, ln_g, ln_b, alpha, seq):
    t, d = x2.shape
    tr = SEQ_TILE
    per_b = seq // tr
    kern = functools.partial(_combine_kernel, alpha=alpha)
    return pl.pallas_call(
        kern,
        grid=(t // tr,),
        in_specs=[pl.BlockSpec((1, 1, tr * TOP_K), lambda i: (i, 0, 0), memory_space=pltpu.SMEM),
                  pl.BlockSpec(memory_space=pl.ANY),
                  pl.BlockSpec((tr, d), lambda i: (i, 0)),
                  pl.BlockSpec((1, 3, d), lambda i: (i // per_b, 0, 0)),
                  pl.BlockSpec((tr, LANES), lambda i: (i, 0)),
                  pl.BlockSpec((1, d), lambda i: (0, 0)),
                  pl.BlockSpec((1, d), lambda i: (0, 0))],
        out_specs=pl.BlockSpec((tr, d), lambda i: (i, 0)),
        out_shape=jax.ShapeDtypeStruct((t, d), F32),
        scratch_shapes=[pltpu.VMEM((TOP_K, tr, d), F32), pltpu.SemaphoreType.DMA(())],
        compiler_params=_params("arbitrary"),
        name="moe_combine",
    )(dest, y, x2, mod, route, ln_g, ln_b)


def _moe_layer(x2, mod, w_r, b_r, w_gu, b_gu, w_down, b_down, ln_g, ln_b, alpha, seq):
    t, d = x2.shape
    n_exp = w_gu.shape[0]
    tm = MOE_TILE
    hpk, route, cnt = _router(x2, mod, w_r, b_r, seq)
    top_idx = route[:, 0:TOP_K]
    rank = route[:, TOP_K:2 * TOP_K]
    counts = cnt[0, :n_exp]
    padded = (counts + tm - 1) // tm * tm
    pend = jnp.cumsum(padded)
    pstart = pend - padded
    dest = pstart[top_idx] + rank
    n_tiles = t * TOP_K // tm + n_exp
    n_used = (pend[-1] // tm).astype(jnp.int32).reshape(1)
    tile_expert = jnp.minimum(
        jnp.searchsorted(pend, jnp.arange(n_tiles, dtype=jnp.int32) * tm, side="right"),
        n_exp - 1).astype(jnp.int32)
    tile_expert = jnp.where(jnp.arange(n_tiles) < n_used[0], tile_expert,
                            tile_expert[jnp.maximum(n_used[0] - 1, 0)])
    tok = jnp.arange(t * TOP_K, dtype=jnp.int32) // TOP_K
    slot_tok = jnp.zeros((n_tiles * tm,), jnp.int32).at[dest.reshape(-1)].set(tok)
    y = _moe_gemm(tile_expert, n_used, slot_tok.reshape(n_tiles, 1, tm), hpk,
                  w_gu, b_gu, w_down, b_down)
    dest3 = dest.reshape(t // SEQ_TILE, 1, SEQ_TILE * TOP_K)
    return _combine(dest3, y, x2, mod, route, ln_g, ln_b, alpha, seq)


def _kv_kernel(x_ref, mod_ref, wk_ref, wv_ref, wf_ref, bf_ref, k_ref, v_ref, cum_ref, carry_ref):
    ts = x_ref.shape[1]
    n_heads = cum_ref.shape[2]

    @pl.when(pl.program_id(1) == 0)
    def _():
        carry_ref[...] = jnp.zeros_like(carry_ref)

    h = _modulate(x_ref[0], mod_ref).astype(BF16)
    k_ref[0] = jnp.dot(h, wk_ref[...], preferred_element_type=F32).astype(BF16)
    v_ref[0] = jnp.dot(h, wv_ref[...], preferred_element_type=F32).astype(BF16)
    fz = jnp.dot(h, wf_ref[...], preferred_element_type=F32) + bf_ref[...]
    log_f = jnp.minimum(fz, 0.0) - jnp.log1p(jnp.exp(-jnp.abs(fz)))
    r_i = lax.broadcasted_iota(jnp.int32, (ts, ts), 0)
    c_i = lax.broadcasted_iota(jnp.int32, (ts, ts), 1)
    tri = jnp.where(c_i <= r_i, 1.0, 0.0).astype(BF16)
    p0 = log_f.astype(BF16)
    r1 = log_f - p0.astype(F32)
    p1 = r1.astype(BF16)
    p2 = (r1 - p1.astype(F32)).astype(BF16)
    cum = (jnp.dot(tri, p0, preferred_element_type=F32) + jnp.dot(tri, p1, preferred_element_type=F32)
           + jnp.dot(tri, p2, preferred_element_type=F32)) + carry_ref[0:1, :]
    carry_ref[0:1, :] = cum[ts - 1:ts, :]
    cum_ref[0] = cum[:, :n_heads]


def _shared_kv(x, mod, w_k, w_v, w_f, b_f):
    bsz, seq, d = x.shape
    ts = SEQ_TILE
    return pl.pallas_call(
        _kv_kernel,
        grid=(bsz, seq // ts),
        in_specs=[pl.BlockSpec((1, ts, d), lambda b, j: (b, j, 0)),
                  pl.BlockSpec((1, 2, d), lambda b, j: (b, 0, 0)),
                  pl.BlockSpec((d, d), lambda b, j: (0, 0)),
                  pl.BlockSpec((d, d), lambda b, j: (0, 0)),
                  pl.BlockSpec((d, LANES), lambda b, j: (0, 0)),
                  pl.BlockSpec((1, LANES), lambda b, j: (0, 0))],
        out_specs=[pl.BlockSpec((1, ts, d), lambda b, j: (b, j, 0)),
                   pl.BlockSpec((1, ts, d), lambda b, j: (b, j, 0)),
                   pl.BlockSpec((1, ts, N_HEADS), lambda b, j: (b, j, 0))],
        out_shape=[jax.ShapeDtypeStruct((bsz, seq, d), BF16),
                   jax.ShapeDtypeStruct((bsz, seq, d), BF16),
                   jax.ShapeDtypeStruct((bsz, seq, N_HEADS), F32)],
        scratch_shapes=[pltpu.VMEM((8, LANES), F32)],
        compiler_params=_params("arbitrary", "arbitrary"),
        name="shared_kv",
    )(x, mod, w_k, w_v, w_f, b_f)


def _q_proj_kernel(x_ref, mod_ref, wq_ref, q_ref, *, scale):
    h = _modulate(x_ref[0], mod_ref).astype(BF16)
    q_ref[0] = (jnp.dot(h, wq_ref[...], preferred_element_type=F32) * scale).astype(BF16)


def _q_proj(x, mod, w_q, scale):
    bsz, seq, d = x.shape
    ts = SEQ_TILE
    return pl.pallas_call(
        functools.partial(_q_proj_kernel, scale=scale),
        grid=(bsz, seq // ts),
        in_specs=[pl.BlockSpec((1, ts, d), lambda b, j: (b, j, 0)),
                  pl.BlockSpec((1, 3, d), lambda b, j: (b, 0, 0)),
                  pl.BlockSpec((d, d), lambda b, j: (0, 0))],
        out_specs=pl.BlockSpec((1, ts, d), lambda b, j: (b, j, 0)),
        out_shape=jax.ShapeDtypeStruct((bsz, seq, d), BF16),
        compiler_params=_params("arbitrary", "arbitrary"),
        name="q_proj",
    )(x, mod, w_q)


def _attn_kernel(q_ref, k_ref, v_ref, cq_ref, ck_ref, o_ref, acc_ref, m_ref, l_ref):
    tq = q_ref.shape[1]
    tk = tq
    i = pl.program_id(2)
    lane = lax.broadcasted_iota(jnp.int32, (1, LANES), 1)
    first = lane < (LANES // HEAD_PAIR)
    q = q_ref[0]
    zero = jnp.zeros_like(q)
    q_heads = (jnp.where(first, q, zero), jnp.where(first, zero, q))
    dq = (cq_ref[0, 0, :, 0:1], cq_ref[0, 0, :, 1:2])
    acc_ref[...] = jnp.zeros_like(acc_ref)
    m_ref[...] = jnp.full_like(m_ref, NEG_INF)
    l_ref[...] = jnp.zeros_like(l_ref)

    def chunk(j, diagonal):
        start = pl.multiple_of(j * tk, tk)
        ks = k_ref[0, pl.ds(start, tk), :]
        vs = v_ref[0, pl.ds(start, tk), :]
        ck = ck_ref[0, 0, j]
        vzero = jnp.zeros_like(vs)
        v_heads = (jnp.where(first, vs, vzero), jnp.where(first, vzero, vs))
        pv = None
        alphas = []
        for hd in range(HEAD_PAIR):
            s = lax.dot_general(q_heads[hd], ks, (((1,), (1,)), ((), ())),
                                preferred_element_type=F32)
            s = s + (dq[hd] - ck[hd:hd + 1, :])
            if diagonal:
                r_i = lax.broadcasted_iota(jnp.int32, (tq, tk), 0)
                c_i = lax.broadcasted_iota(jnp.int32, (tq, tk), 1)
                s = jnp.where(r_i >= c_i, s, NEG_INF)
            m_old = m_ref[hd]
            m_new = jnp.maximum(m_old, jnp.max(s, axis=-1, keepdims=True))
            a = jnp.exp(m_old - m_new)
            p = jnp.exp(s - m_new)
            l_ref[hd] = a * l_ref[hd] + jnp.sum(p, axis=-1, keepdims=True)
            m_ref[hd] = m_new
            alphas.append(a)
            contrib = jnp.dot(p.astype(BF16), v_heads[hd], preferred_element_type=F32)
            pv = contrib if pv is None else pv + contrib
        acc_ref[...] = acc_ref[...] * jnp.where(first, alphas[0], alphas[1]) + pv

    def body(j, carry):
        chunk(j, False)
        return carry
    lax.fori_loop(0, i, body, 0)
    chunk(i, True)
    o_ref[0] = (acc_ref[...] / jnp.where(first, l_ref[0], l_ref[1])).astype(o_ref.dtype)


def _attention(q, k, v, cq, ck):
    bsz, seq, d = q.shape
    n_pairs = d // LANES
    tq = ATTN_TILE
    return pl.pallas_call(
        _attn_kernel,
        grid=(bsz, n_pairs, seq // tq),
        in_specs=[pl.BlockSpec((1, tq, LANES), lambda b, p, i: (b, i, p)),
                  pl.BlockSpec((1, seq, LANES), lambda b, p, i: (b, 0, p)),
                  pl.BlockSpec((1, seq, LANES), lambda b, p, i: (b, 0, p)),
                  pl.BlockSpec((1, 1, tq, HEAD_PAIR), lambda b, p, i: (b, p, i, 0)),
                  pl.BlockSpec((1, 1, seq // tq, HEAD_PAIR, tq), lambda b, p, i: (b, p, 0, 0, 0))],
        out_specs=pl.BlockSpec((1, tq, LANES), lambda b, p, i: (b, i, p)),
        out_shape=jax.ShapeDtypeStruct((bsz, seq, d), BF16),
        scratch_shapes=[pltpu.VMEM((tq, LANES), F32),
                        pltpu.VMEM((HEAD_PAIR, tq, 1), F32),
                        pltpu.VMEM((HEAD_PAIR, tq, 1), F32)],
        compiler_params=_params("arbitrary", "arbitrary", "arbitrary"),
        name="fox_attention",
    )(q, k, v, cq, ck)


def _out_proj_kernel(o_ref, x_ref, mod_ref, wo_ref, g_ref, b_ref, out_ref, *, alpha):
    y = jnp.dot(o_ref[0], wo_ref[...], preferred_element_type=F32)
    r = alpha * x_ref[0] + (1.0 + mod_ref[0, 2:3, :]) * y
    out_ref[0] = _layer_norm(r, g_ref[...], b_ref[...])


def _out_proj(o, x, mod, w_o, ln_g, ln_b, alpha):
    bsz, seq, d = x.shape
    ts = SEQ_TILE
    return pl.pallas_call(
        functools.partial(_out_proj_kernel, alpha=alpha),
        grid=(bsz, seq // ts),
        in_specs=[pl.BlockSpec((1, ts, d), lambda b, j: (b, j, 0)),
                  pl.BlockSpec((1, ts, d), lambda b, j: (b, j, 0)),
                  pl.BlockSpec((1, 3, d), lambda b, j: (b, 0, 0)),
                  pl.BlockSpec((d, d), lambda b, j: (0, 0)),
                  pl.BlockSpec((1, d), lambda b, j: (0, 0)),
                  pl.BlockSpec((1, d), lambda b, j: (0, 0))],
        out_specs=pl.BlockSpec((1, ts, d), lambda b, j: (b, j, 0)),
        out_shape=jax.ShapeDtypeStruct(x.shape, F32),
        compiler_params=_params("arbitrary", "arbitrary"),
        name="attn_out_proj",
    )(o, x, mod, w_o, ln_g, ln_b)


def kernel(x, c, conv_w_in, conv_w, conv_w_out, kv_ada_w, kv_ada_b, w_kvf, b_f, attn_w_q, attn_w_o,
           ada_w, ada_b, ln_g, ln_b, router_w, router_b, exp_w_gu, exp_b_gu, exp_w_down, exp_b_down):
    bsz, seq, d = x.shape
    depth = ada_w.shape[0]
    n_conv = conv_w_in.shape[0]
    n_exp = router_w.shape[-1]
    alpha = (2.0 * depth) ** 0.25
    head_dim = d // N_HEADS
    assert head_dim * HEAD_PAIR == LANES and seq % SEQ_TILE == 0 and seq % ATTN_TILE == 0

    c_pad = jnp.pad(c, ((0, 8 - bsz), (0, 0)))
    mods = _ada_params(c_pad, ada_w.reshape(depth * 2, d, 3 * d), ada_b.reshape(depth * 2, 1, 3 * d))
    mods = mods[:, :bsz, :].reshape(depth, 2, bsz, 3, d)
    kv_mod = _ada_params(c_pad, kv_ada_w[None], kv_ada_b[None, None])[0, :bsz].reshape(bsz, 2, d)

    w_r = jnp.pad(router_w, ((0, 0), (0, 0), (0, LANES - n_exp)))
    b_r = jnp.pad(router_b, ((0, 0), (0, LANES - n_exp)), constant_values=-1e30)[:, None, :]
    w_gu = exp_w_gu.astype(BF16)
    w_dn = exp_w_down.astype(BF16)
    k = v = cq = ck = None
    for l in range(depth):
        g0, b0 = ln_g[l, 0][None], ln_b[l, 0][None]
        if l < n_conv:
            x = _conv_layer(x, mods[l, 0], conv_w_in[l].astype(BF16), conv_w[l],
                            conv_w_out[l].astype(BF16), g0, b0, alpha)
        else:
            j = l - n_conv
            q = _q_proj(x, mods[l, 0], attn_w_q[j].astype(BF16), head_dim ** -0.5)
            o = _attention(q, k, v, cq, ck)
            x = _out_proj(o, x, mods[l, 0], attn_w_o[j].astype(BF16), g0, b0, alpha)
        x = _moe_layer(x.reshape(bsz * seq, d), mods[l, 1], w_r[l], b_r[l], w_gu[l],
                       exp_b_gu[l][:, None, :], w_dn[l], exp_b_down[l][:, None, :],
                       ln_g[l, 1][None], ln_b[l, 1][None], alpha, seq).reshape(bsz, seq, d)
        if l == n_conv - 1:
            w_f = jnp.pad(w_kvf[:, 2 * d:], ((0, 0), (0, LANES - N_HEADS))).astype(BF16)
            bias_f = jnp.pad(b_f, (0, LANES - N_HEADS))[None]
            k, v, cum = _shared_kv(x, kv_mod, w_kvf[:, :d].astype(BF16),
                                   w_kvf[:, d:2 * d].astype(BF16), w_f, bias_f)
            cum = cum.reshape(bsz, seq, N_HEADS // HEAD_PAIR, HEAD_PAIR)
            cq = cum.transpose(0, 2, 1, 3)
            ck = cum.reshape(bsz, seq // ATTN_TILE, ATTN_TILE, N_HEADS // HEAD_PAIR, HEAD_PAIR)
            ck = ck.transpose(0, 3, 1, 4, 2)
    return x
```

```python
import functools

import numpy as np
import jax
import jax.numpy as jnp
from jax import lax
from jax.experimental import pallas as pl
from jax.experimental.pallas import tpu as pltpu

N_HEADS = 16
TOP_K = 4
SWIGLU_LIMIT = 7.0
SWIGLU_ALPHA = 1.702
LN_EPS = 1e-5
LANES = 128
HEAD_PAIR = 2
SEQ_TILE = 512
MOE_TILE = 512
ATTN_TILE = 512
VMEM_LIMIT = 56 * 1024 * 1024

F32 = jnp.float32
BF16 = jnp.bfloat16
NEG_INF = float("-inf")


def _params(*sem):
    return pltpu.CompilerParams(dimension_semantics=sem, vmem_limit_bytes=VMEM_LIMIT)


def _layer_norm(r, g, b):
    mu = jnp.mean(r, axis=-1, keepdims=True)
    d = r - mu
    var = jnp.mean(d * d, axis=-1, keepdims=True)
    return d * lax.rsqrt(var + LN_EPS) * g + b


def _modulate(x, mod_ref):
    return x * (1.0 + mod_ref[0, 1:2, :]) + mod_ref[0, 0:1, :]


def _ada_kernel(c_ref, w_ref, b_ref, o_ref):
    c = c_ref[...]
    cond = c * jax.nn.sigmoid(c)
    o_ref[0] = jnp.dot(cond, w_ref[0], precision=lax.Precision.HIGHEST,
                       preferred_element_type=F32) + b_ref[0]


def _ada_params(c_pad, w, b):
    g, d, n = w.shape
    tn = 1024 if n % 1024 == 0 else n
    return pl.pallas_call(
        _ada_kernel,
        grid=(g, n // tn),
        in_specs=[pl.BlockSpec((8, d), lambda i, j: (0, 0)),
                  pl.BlockSpec((1, d, tn), lambda i, j: (i, 0, j)),
                  pl.BlockSpec((1, 1, tn), lambda i, j: (i, 0, j))],
        out_specs=pl.BlockSpec((1, 8, tn), lambda i, j: (i, 0, j)),
        out_shape=jax.ShapeDtypeStruct((g, 8, n), F32),
        compiler_params=_params("arbitrary", "arbitrary"),
        name="ada_params",
    )(c_pad, w, b)


def _conv_layer_kernel(x_ref, mod_ref, win_ref, wc_ref, wout_ref, g_ref, b_ref, o_ref,
                       carry_ref, a_ref, *, alpha, col_chunk):
    ts, d = x_ref.shape[1], x_ref.shape[2]

    @pl.when(pl.program_id(1) == 0)
    def _():
        carry_ref[...] = jnp.zeros_like(carry_ref)

    x = x_ref[0]
    h = _modulate(x, mod_ref).astype(BF16)
    row = lax.broadcasted_iota(jnp.int32, (ts, col_chunk), 0)
    for c in range(0, d, col_chunk):
        gate_c = jnp.dot(h, win_ref[:, c:c + col_chunk], preferred_element_type=F32)
        u = jnp.dot(h, win_ref[:, 2 * d + c:2 * d + c + col_chunk], preferred_element_type=F32)
        z = gate_c * u
        z_m1 = carry_ref[1:2, c:c + col_chunk]
        z_m2 = carry_ref[0:1, c:c + col_chunk]
        z1 = jnp.where(row == 0, z_m1, pltpu.roll(z, 1, 0))
        z2 = jnp.where(row == 0, z_m2, jnp.where(row == 1, z_m1, pltpu.roll(z, 2, 0)))
        carry_ref[0:2, c:c + col_chunk] = z[ts - 2:ts, :]
        conv = (wc_ref[0:1, c:c + col_chunk] * z2 + wc_ref[1:2, c:c + col_chunk] * z1
                + wc_ref[2:3, c:c + col_chunk] * z)
        gate_b = jnp.dot(h, win_ref[:, d + c:d + c + col_chunk], preferred_element_type=F32)
        a_ref[:, c:c + col_chunk] = (gate_b * conv).astype(BF16)
    y = jnp.dot(a_ref[...], wout_ref[...], preferred_element_type=F32)
    r = alpha * x + (1.0 + mod_ref[0, 2:3, :]) * y
    o_ref[0] = _layer_norm(r, g_ref[...], b_ref[...])


def _conv_layer(x, mod, w_in, w_conv, w_out, ln_g, ln_b, alpha):
    bsz, seq, d = x.shape
    ts = SEQ_TILE
    kern = functools.partial(_conv_layer_kernel, alpha=alpha, col_chunk=256)
    return pl.pallas_call(
        kern,
        grid=(bsz, seq // ts),
        in_specs=[pl.BlockSpec((1, ts, d), lambda b, j: (b, j, 0)),
                  pl.BlockSpec((1, 3, d), lambda b, j: (b, 0, 0)),
                  pl.BlockSpec((d, 3 * d), lambda b, j: (0, 0)),
                  pl.BlockSpec((3, d), lambda b, j: (0, 0)),
                  pl.BlockSpec((d, d), lambda b, j: (0, 0)),
                  pl.BlockSpec((1, d), lambda b, j: (0, 0)),
                  pl.BlockSpec((1, d), lambda b, j: (0, 0))],
        out_specs=pl.BlockSpec((1, ts, d), lambda b, j: (b, j, 0)),
        out_shape=jax.ShapeDtypeStruct(x.shape, F32),
        scratch_shapes=[pltpu.VMEM((8, d), F32), pltpu.VMEM((ts, d), BF16)],
        compiler_params=_params("arbitrary", "arbitrary"),
        name="conv_layer",
    )(x, mod, w_in, w_conv, w_out, ln_g, ln_b)


def _pack_bf16_pairs(h):
    half = h.shape[1] // 2
    lo = lax.bitcast_convert_type(h[:, :half].astype(BF16).astype(F32), jnp.int32)
    hi = lax.bitcast_convert_type(h[:, half:].astype(BF16).astype(F32), jnp.int32)
    return lax.shift_right_logical(lo, 16) | hi


def _unpack_bf16_pairs(w):
    lo = lax.bitcast_convert_type(w << 16, F32).astype(BF16)
    hi = lax.bitcast_convert_type(w & jnp.int32(-65536), F32).astype(BF16)
    return lo, hi


def _router_kernel(x_ref, mod_ref, wr_ref, br_ref, hpk_ref, route_ref, cnt_ref, carry_ref):
    tr = x_ref.shape[0]

    @pl.when(pl.program_id(0) == 0)
    def _():
        carry_ref[...] = jnp.zeros_like(carry_ref)

    h = _modulate(x_ref[...], mod_ref)
    hpk_ref[...] = _pack_bf16_pairs(h)
    logits = jnp.dot(h, wr_ref[...], precision=lax.Precision.HIGHEST,
                     preferred_element_type=F32) + br_ref[...]
    lane = lax.broadcasted_iota(jnp.int32, (tr, LANES), 1)
    lane_f = lane.astype(F32)
    work = logits
    vals, idxs, sels = [], [], []
    for _ in range(TOP_K):
        m = jnp.max(work, axis=-1, keepdims=True)
        idx = jnp.min(jnp.where(work == m, lane_f, float(LANES)), axis=-1, keepdims=True)
        sel = lane_f == idx
        vals.append(m)
        idxs.append(idx.astype(jnp.int32))
        sels.append(sel)
        work = jnp.where(sel, NEG_INF, work)
    exps = [jnp.exp(v - vals[0]) for v in vals]
    denom = exps[0] + exps[1] + exps[2] + exps[3]
    chosen = sels[0] | sels[1] | sels[2] | sels[3]
    onehot = jnp.where(chosen, 1.0, 0.0).astype(BF16)
    r_i = lax.broadcasted_iota(jnp.int32, (tr, tr), 0)
    c_i = lax.broadcasted_iota(jnp.int32, (tr, tr), 1)
    tri = jnp.where(c_i < r_i, 1.0, 0.0).astype(BF16)
    before = jnp.dot(tri, onehot, preferred_element_type=F32) + carry_ref[0:1, :]
    out = jnp.zeros((tr, LANES), jnp.int32)
    for k in range(TOP_K):
        rank = jnp.sum(jnp.where(sels[k], before, 0.0), axis=-1, keepdims=True).astype(jnp.int32)
        wgt = lax.bitcast_convert_type(exps[k] / denom, jnp.int32)
        out = jnp.where(lane == k, idxs[k], out)
        out = jnp.where(lane == TOP_K + k, rank, out)
        out = jnp.where(lane == 2 * TOP_K + k, wgt, out)
    route_ref[...] = out
    carry_ref[0:1, :] = carry_ref[0:1, :] + jnp.sum(onehot.astype(F32), axis=0, keepdims=True)
    cnt_ref[...] = carry_ref[...].astype(jnp.int32)


def _router(x2, mod, w_r, b_r, seq):
    t, d = x2.shape
    tr = SEQ_TILE
    per_b = seq // tr
    return pl.pallas_call(
        _router_kernel,
        grid=(t // tr,),
        in_specs=[pl.BlockSpec((tr, d), lambda i: (i, 0)),
                  pl.BlockSpec((1, 3, d), lambda i: (i // per_b, 0, 0)),
                  pl.BlockSpec((d, LANES), lambda i: (0, 0)),
                  pl.BlockSpec((1, LANES), lambda i: (0, 0))],
        out_specs=[pl.BlockSpec((tr, d // 2), lambda i: (i, 0)),
                   pl.BlockSpec((tr, LANES), lambda i: (i, 0)),
                   pl.BlockSpec((8, LANES), lambda i: (0, 0))],
        out_shape=[jax.ShapeDtypeStruct((t, d // 2), jnp.int32),
                   jax.ShapeDtypeStruct((t, LANES), jnp.int32),
                   jax.ShapeDtypeStruct((8, LANES), jnp.int32)],
        scratch_shapes=[pltpu.VMEM((8, LANES), F32)],
        compiler_params=_params("arbitrary"),
        name="moe_router",
    )(x2, mod, w_r, b_r)


def _row_copy(src_hbm, dst_vmem, sem, src_row, dst_row):
    return pltpu.make_async_copy(src_hbm.at[pl.ds(src_row, 1)], dst_vmem.at[pl.ds(dst_row, 1)], sem)


def _moe_gemm_kernel(te_ref, nused_ref, tok_ref, hpk_hbm, wgu_ref, bgu_ref, wd_ref, bd_ref,
                     y_ref, xbuf, sem):
    tm = xbuf.shape[0]
    half = xbuf.shape[1]
    f = wd_ref.shape[1]

    @pl.when(pl.program_id(0) < nused_ref[0])
    def _():
        def issue(r, carry):
            _row_copy(hpk_hbm, xbuf, sem, tok_ref[0, 0, r], r).start()
            return carry
        lax.fori_loop(0, tm, issue, 0, unroll=8)

        def wait(r, carry):
            _row_copy(hpk_hbm, xbuf, sem, 0, r).wait()
            return carry
        lax.fori_loop(0, tm, wait, 0, unroll=8)

        lo, hi = _unpack_bf16_pairs(xbuf[...])
        gu = (jnp.dot(lo, wgu_ref[0, :half, :], preferred_element_type=F32)
              + jnp.dot(hi, wgu_ref[0, half:, :], preferred_element_type=F32) + bgu_ref[0])
        g = jnp.minimum(gu[:, :f], SWIGLU_LIMIT)
        u = jnp.clip(gu[:, f:], -SWIGLU_LIMIT, SWIGLU_LIMIT)
        a = g * jax.nn.sigmoid(SWIGLU_ALPHA * g) * (u + 1.0)
        y_ref[...] = jnp.dot(a.astype(BF16), wd_ref[0], preferred_element_type=F32) + bd_ref[0]

    @pl.when(pl.program_id(0) >= nused_ref[0])
    def _():
        y_ref[...] = jnp.zeros_like(y_ref)


def _moe_gemm(tile_expert, n_used, slot_tok, hpk, w_gu, b_gu, w_down, b_down):
    n_tiles = tile_expert.shape[0]
    tm = MOE_TILE
    e, d, f2 = w_gu.shape
    f = f2 // 2
    last = lambda i, te, nu: jnp.minimum(i, nu[0] - 1)
    grid_spec = pltpu.PrefetchScalarGridSpec(
        num_scalar_prefetch=2,
        grid=(n_tiles,),
        in_specs=[pl.BlockSpec((1, 1, tm), lambda i, te, nu: (last(i, te, nu), 0, 0),
                               memory_space=pltpu.SMEM),
                  pl.BlockSpec(memory_space=pl.ANY),
                  pl.BlockSpec((1, d, f2), lambda i, te, nu: (te[i], 0, 0)),
                  pl.BlockSpec((1, 1, f2), lambda i, te, nu: (te[i], 0, 0)),
                  pl.BlockSpec((1, f, d), lambda i, te, nu: (te[i], 0, 0)),
                  pl.BlockSpec((1, 1, d), lambda i, te, nu: (te[i], 0, 0))],
        out_specs=pl.BlockSpec((tm, d), lambda i, te, nu: (i, 0)),
        scratch_shapes=[pltpu.VMEM((tm, d // 2), jnp.int32), pltpu.SemaphoreType.DMA(())],
    )
    return pl.pallas_call(
        _moe_gemm_kernel,
        grid_spec=grid_spec,
        out_shape=jax.ShapeDtypeStruct((n_tiles * tm, d), F32),
        compiler_params=_params("arbitrary"),
        name="moe_experts",
    )(tile_expert, n_used, slot_tok, hpk, w_gu, b_gu, w_down, b_down)


def _combine_kernel(dest_ref, y_hbm, x_ref, mod_ref, route_ref, g_ref, b_ref, o_ref, ybuf, sem, *, alpha):
    tr = x_ref.shape[0]

    def issue(r, carry):
        for k in range(TOP_K):
            _row_copy(y_hbm, ybuf.at[k], sem, dest_ref[0, 0, r * TOP_K + k], r).start()
        return carry
    lax.fori_loop(0, tr, issue, 0, unroll=4)

    def wait(r, carry):
        for k in range(TOP_K):
            _row_copy(y_hbm, ybuf.at[k], sem, 0, r).wait()
        return carry
    lax.fori_loop(0, tr, wait, 0, unroll=4)

    wts = lax.bitcast_convert_type(route_ref[...], F32)
    sub = wts[:, 2 * TOP_K:2 * TOP_K + 1] * ybuf[0]
    for k in range(1, TOP_K):
        sub = sub + wts[:, 2 * TOP_K + k:2 * TOP_K + k + 1] * ybuf[k]
    x = x_ref[...]
    r = alpha * x + (1.0 + mod_ref[0, 2:3, :]) * sub
    o_ref[...] = _layer_norm(r, g_ref[...], b_ref[...])


def _combine(dest, y, x2, mod, route, ln_g, ln_b, alpha, seq):
    t, d = x2.shape
    tr = SEQ_TILE
    per_b = seq // tr
    kern = functools.partial(_combine_kernel, alpha=alpha)
    return pl.pallas_call(
        kern,
        grid=(t // tr,),
        in_specs=[pl.BlockSpec((1, 1, tr * TOP_K), lambda i: (i, 0, 0), memory_space=pltpu.SMEM),
                  pl.BlockSpec(memory_space=pl.ANY),
                  pl.BlockSpec((tr, d), lambda i: (i, 0)),
                  pl.BlockSpec((1, 3, d), lambda i: (i // per_b, 0, 0)),
                  pl.BlockSpec((tr, LANES), lambda i: (i, 0)),
                  pl.BlockSpec((1, d), lambda i: (0, 0)),
                  pl.BlockSpec((1, d), lambda i: (0, 0))],
        out_specs=pl.BlockSpec((tr, d), lambda i: (i, 0)),
        out_shape=jax.ShapeDtypeStruct((t, d), F32),
        scratch_shapes=[pltpu.VMEM((TOP_K, tr, d), F32), pltpu.SemaphoreType.DMA(())],
        compiler_params=_params("arbitrary"),
        name="moe_combine",
    )(dest, y, x2, mod, route, ln_g, ln_b)


def _moe_layer(x2, mod, w_r, b_r, w_gu, b_gu, w_down, b_down, ln_g, ln_b, alpha, seq):
    t, d = x2.shape
    n_exp = w_gu.shape[0]
    tm = MOE_TILE
    hpk, route, cnt = _router(x2, mod, w_r, b_r, seq)
    top_idx = route[:, 0:TOP_K]
    rank = route[:, TOP_K:2 * TOP_K]
    counts = cnt[0, :n_exp]
    padded = (counts + tm - 1) // tm * tm
    pend = jnp.cumsum(padded)
    pstart = pend - padded
    dest = pstart[top_idx] + rank
    n_tiles = t * TOP_K // tm + n_exp
    n_used = (pend[-1] // tm).astype(jnp.int32).reshape(1)
    tile_expert = jnp.minimum(
        jnp.searchsorted(pend, jnp.arange(n_tiles, dtype=jnp.int32) * tm, side="right"),
        n_exp - 1).astype(jnp.int32)
    tile_expert = jnp.where(jnp.arange(n_tiles) < n_used[0], tile_expert,
                            tile_expert[jnp.maximum(n_used[0] - 1, 0)])
    tok = jnp.arange(t * TOP_K, dtype=jnp.int32) // TOP_K
    slot_tok = jnp.zeros((n_tiles * tm,), jnp.int32).at[dest.reshape(-1)].set(tok)
    y = _moe_gemm(tile_expert, n_used, slot_tok.reshape(n_tiles, 1, tm), hpk,
                  w_gu, b_gu, w_down, b_down)
    dest3 = dest.reshape(t // SEQ_TILE, 1, SEQ_TILE * TOP_K)
    return _combine(dest3, y, x2, mod, route, ln_g, ln_b, alpha, seq)


def _split3(v):
    p0 = v.astype(BF16)
    r1 = v - p0.astype(F32)
    p1 = r1.astype(BF16)
    p2 = (r1 - p1.astype(F32)).astype(BF16)
    return p0, p1, p2


AUG_STRIDE = 8
AUG_PARTS = 3


def _aug_constants(d):
    n_pairs = d // LANES
    sel = np.zeros((AUG_PARTS, LANES, d), np.float32)
    ones = np.zeros((1, d), np.float32)
    for p in range(n_pairs):
        for hd in range(HEAD_PAIR):
            base = LANES * p + AUG_STRIDE * hd
            for part in range(AUG_PARTS):
                sel[part, HEAD_PAIR * p + hd, base + part] = 1.0
                ones[0, base + AUG_PARTS + part] = 1.0
    return jnp.asarray(sel, BF16), jnp.asarray(ones, F32)


def _kv_kernel(x_ref, mod_ref, wk_ref, wv_ref, wf_ref, bf_ref, sel_ref, ones_ref,
               k_ref, kaug_ref, vt_ref, cum_ref, carry_ref):
    ts = x_ref.shape[1]
    n_heads = cum_ref.shape[2]
    n_pairs = kaug_ref.shape[1]

    @pl.when(pl.program_id(1) == 0)
    def _():
        carry_ref[...] = jnp.zeros_like(carry_ref)

    h = _modulate(x_ref[0], mod_ref).astype(BF16)
    k_ref[0] = jnp.dot(h, wk_ref[...], preferred_element_type=F32).astype(BF16)
    vt = jnp.dot(h, wv_ref[...], preferred_element_type=F32).T.astype(BF16)
    for p in range(n_pairs):
        vt_ref[0, p, 0] = vt[LANES * p:LANES * (p + 1), :]
    fz = jnp.dot(h, wf_ref[...], preferred_element_type=F32) + bf_ref[...]
    log_f = jnp.minimum(fz, 0.0) - jnp.log1p(jnp.exp(-jnp.abs(fz)))
    r_i = lax.broadcasted_iota(jnp.int32, (ts, ts), 0)
    c_i = lax.broadcasted_iota(jnp.int32, (ts, ts), 1)
    tri = jnp.where(c_i <= r_i, 1.0, 0.0).astype(BF16)
    cum = carry_ref[0:1, :]
    for part in _split3(log_f):
        cum = cum + jnp.dot(tri, part, preferred_element_type=F32)
    carry_ref[0:1, :] = cum[ts - 1:ts, :]
    cum_ref[0] = cum[:, :n_heads]
    aug = ones_ref[...]
    for i, part in enumerate(_split3(cum)):
        aug = aug + jnp.dot(part, sel_ref[i], preferred_element_type=F32)
    aug = aug.astype(BF16)
    for p in range(n_pairs):
        kaug_ref[0, p] = aug[:, LANES * p:LANES * (p + 1)]


def _shared_kv(x, mod, w_k, w_v, w_f, b_f):
    bsz, seq, d = x.shape
    ts = ATTN_TILE
    n_pairs = d // LANES
    sel, ones = _aug_constants(d)
    return pl.pallas_call(
        _kv_kernel,
        grid=(bsz, seq // ts),
        in_specs=[pl.BlockSpec((1, ts, d), lambda b, j: (b, j, 0)),
                  pl.BlockSpec((1, 2, d), lambda b, j: (b, 0, 0)),
                  pl.BlockSpec((d, d), lambda b, j: (0, 0)),
                  pl.BlockSpec((d, d), lambda b, j: (0, 0)),
                  pl.BlockSpec((d, LANES), lambda b, j: (0, 0)),
                  pl.BlockSpec((1, LANES), lambda b, j: (0, 0)),
                  pl.BlockSpec((AUG_PARTS, LANES, d), lambda b, j: (0, 0, 0)),
                  pl.BlockSpec((1, d), lambda b, j: (0, 0))],
        out_specs=[pl.BlockSpec((1, ts, d), lambda b, j: (b, j, 0)),
                   pl.BlockSpec((1, n_pairs, ts, LANES), lambda b, j: (b, 0, j, 0)),
                   pl.BlockSpec((1, n_pairs, 1, LANES, ts), lambda b, j: (b, 0, j, 0, 0)),
                   pl.BlockSpec((1, ts, N_HEADS), lambda b, j: (b, j, 0))],
        out_shape=[jax.ShapeDtypeStruct((bsz, seq, d), BF16),
                   jax.ShapeDtypeStruct((bsz, n_pairs, seq, LANES), BF16),
                   jax.ShapeDtypeStruct((bsz, n_pairs, seq // ts, LANES, ts), BF16),
                   jax.ShapeDtypeStruct((bsz, seq, N_HEADS), F32)],
        scratch_shapes=[pltpu.VMEM((8, LANES), F32)],
        compiler_params=_params("arbitrary", "arbitrary"),
        name="shared_kv",
    )(x, mod, w_k, w_v, w_f, b_f, sel, ones)


def _q_proj_kernel(x_ref, mod_ref, wq_ref, q_ref, *, scale):
    h = _modulate(x_ref[0], mod_ref).astype(BF16)
    q_ref[0] = (jnp.dot(h, wq_ref[...], preferred_element_type=F32) * scale).astype(BF16)


def _q_proj(x, mod, w_q, scale):
    bsz, seq, d = x.shape
    ts = SEQ_TILE
    return pl.pallas_call(
        functools.partial(_q_proj_kernel, scale=scale),
        grid=(bsz, seq // ts),
        in_specs=[pl.BlockSpec((1, ts, d), lambda b, j: (b, j, 0)),
                  pl.BlockSpec((1, 3, d), lambda b, j: (b, 0, 0)),
                  pl.BlockSpec((d, d), lambda b, j: (0, 0))],
        out_specs=pl.BlockSpec((1, ts, d), lambda b, j: (b, j, 0)),
        out_shape=jax.ShapeDtypeStruct((bsz, seq, d), BF16),
        compiler_params=_params("arbitrary", "arbitrary"),
        name="q_proj",
    )(x, mod, w_q)


def _attn_kernel(q_ref, k_ref, kaug_ref, vt_ref, cq_ref, o_ref, acc_ref, m_ref, l_ref):
    tq = q_ref.shape[1]
    tk = tq
    head_dim = LANES // HEAD_PAIR
    i = pl.program_id(2)
    q_t = q_ref[0].astype(F32).T
    row = lax.broadcasted_iota(jnp.int32, (LANES, tq), 0)
    rhs = []
    for hd in range(HEAD_PAIR):
        own = (row >= head_dim * hd) & (row < head_dim * (hd + 1))
        parts = _split3(cq_ref[0, 0, hd:hd + 1, :])
        base = AUG_STRIDE * hd
        aug = jnp.where((row >= base) & (row < base + AUG_PARTS), -1.0, 0.0)
        for n, part in enumerate(parts):
            aug = jnp.where(row == base + AUG_PARTS + n, part.astype(F32), aug)
        rhs.append(jnp.concatenate([jnp.where(own, q_t, 0.0).astype(BF16), aug.astype(BF16)], axis=0))
    acc_ref[...] = jnp.zeros_like(acc_ref)
    m_ref[...] = jnp.full_like(m_ref, NEG_INF)
    l_ref[...] = jnp.zeros_like(l_ref)

    def chunk(j, diagonal):
        start = pl.multiple_of(j * tk, tk)
        keys = jnp.concatenate([k_ref[0, pl.ds(start, tk), :], kaug_ref[0, 0, pl.ds(start, tk), :]],
                               axis=1)
        v_t = vt_ref[0, 0, j]
        for hd in range(HEAD_PAIR):
            s_t = jnp.dot(keys, rhs[hd], preferred_element_type=F32)
            if diagonal:
                k_i = lax.broadcasted_iota(jnp.int32, (tk, tq), 0)
                q_i = lax.broadcasted_iota(jnp.int32, (tk, tq), 1)
                s_t = jnp.where(k_i <= q_i, s_t, NEG_INF)
            m_old = m_ref[hd:hd + 1, :]
            m_new = jnp.maximum(m_old, jnp.max(s_t, axis=0, keepdims=True))
            a = jnp.exp(m_old - m_new)
            p_t = jnp.exp(s_t - m_new)
            l_ref[hd:hd + 1, :] = a * l_ref[hd:hd + 1, :] + jnp.sum(p_t, axis=0, keepdims=True)
            m_ref[hd:hd + 1, :] = m_new
            rows = slice(head_dim * hd, head_dim * (hd + 1))
            pv = jnp.dot(v_t[rows, :], p_t.astype(BF16), preferred_element_type=F32)
            acc_ref[rows, :] = acc_ref[rows, :] * a + pv

    def body(j, carry):
        chunk(j, False)
        return carry
    lax.fori_loop(0, i, body, 0)
    chunk(i, True)
    inv = jnp.concatenate([jnp.broadcast_to(1.0 / l_ref[hd:hd + 1, :], (head_dim, tq))
                           for hd in range(HEAD_PAIR)], axis=0)
    o_ref[0] = (acc_ref[...] * inv).T.astype(o_ref.dtype)


def _attention(q, k, kaug, v_t, cq):
    bsz, seq, d = q.shape
    n_pairs = d // LANES
    tq = ATTN_TILE
    return pl.pallas_call(
        _attn_kernel,
        grid=(bsz, n_pairs, seq // tq),
        in_specs=[pl.BlockSpec((1, tq, LANES), lambda b, p, i: (b, i, p)),
                  pl.BlockSpec((1, seq, LANES), lambda b, p, i: (b, 0, p)),
                  pl.BlockSpec((1, 1, seq, LANES), lambda b, p, i: (b, p, 0, 0)),
                  pl.BlockSpec((1, 1, seq // tq, LANES, tq), lambda b, p, i: (b, p, 0, 0, 0)),
                  pl.BlockSpec((1, 1, HEAD_PAIR, tq), lambda b, p, i: (b, p, 0, i))],
        out_specs=pl.BlockSpec((1, tq, LANES), lambda b, p, i: (b, i, p)),
        out_shape=jax.ShapeDtypeStruct((bsz, seq, d), BF16),
        scratch_shapes=[pltpu.VMEM((LANES, tq), F32),
                        pltpu.VMEM((8, tq), F32),
                        pltpu.VMEM((8, tq), F32)],
        compiler_params=_params("arbitrary", "arbitrary", "arbitrary"),
        name="fox_attention",
    )(q, k, kaug, v_t, cq)


def _out_proj_kernel(o_ref, x_ref, mod_ref, wo_ref, g_ref, b_ref, out_ref, *, alpha):
    y = jnp.dot(o_ref[0], wo_ref[...], preferred_element_type=F32)
    r = alpha * x_ref[0] + (1.0 + mod_ref[0, 2:3, :]) * y
    out_ref[0] = _layer_norm(r, g_ref[...], b_ref[...])


def _out_proj(o, x, mod, w_o, ln_g, ln_b, alpha):
    bsz, seq, d = x.shape
    ts = SEQ_TILE
    return pl.pallas_call(
        functools.partial(_out_proj_kernel, alpha=alpha),
        grid=(bsz, seq // ts),
        in_specs=[pl.BlockSpec((1, ts, d), lambda b, j: (b, j, 0)),
                  pl.BlockSpec((1, ts, d), lambda b, j: (b, j, 0)),
                  pl.BlockSpec((1, 3, d), lambda b, j: (b, 0, 0)),
                  pl.BlockSpec((d, d), lambda b, j: (0, 0)),
                  pl.BlockSpec((1, d), lambda b, j: (0, 0)),
                  pl.BlockSpec((1, d), lambda b, j: (0, 0))],
        out_specs=pl.BlockSpec((1, ts, d), lambda b, j: (b, j, 0)),
        out_shape=jax.ShapeDtypeStruct(x.shape, F32),
        compiler_params=_params("arbitrary", "arbitrary"),
        name="attn_out_proj",
    )(o, x, mod, w_o, ln_g, ln_b)


def kernel(x, c, conv_w_in, conv_w, conv_w_out, kv_ada_w, kv_ada_b, w_kvf, b_f, attn_w_q, attn_w_o,
           ada_w, ada_b, ln_g, ln_b, router_w, router_b, exp_w_gu, exp_b_gu, exp_w_down, exp_b_down):
    bsz, seq, d = x.shape
    depth = ada_w.shape[0]
    n_conv = conv_w_in.shape[0]
    n_exp = router_w.shape[-1]
    alpha = (2.0 * depth) ** 0.25
    head_dim = d // N_HEADS
    assert head_dim * HEAD_PAIR == LANES and seq % SEQ_TILE == 0 and seq % ATTN_TILE == 0

    c_pad = jnp.pad(c, ((0, 8 - bsz), (0, 0)))
    mods = _ada_params(c_pad, ada_w.reshape(depth * 2, d, 3 * d), ada_b.reshape(depth * 2, 1, 3 * d))
    mods = mods[:, :bsz, :].reshape(depth, 2, bsz, 3, d)
    kv_mod = _ada_params(c_pad, kv_ada_w[None], kv_ada_b[None, None])[0, :bsz].reshape(bsz, 2, d)

    w_r = jnp.pad(router_w, ((0, 0), (0, 0), (0, LANES - n_exp)))
    b_r = jnp.pad(router_b, ((0, 0), (0, LANES - n_exp)), constant_values=-1e30)[:, None, :]
    w_gu = exp_w_gu.astype(BF16)
    w_dn = exp_w_down.astype(BF16)
    k = kaug = v_t = cq = None
    for l in range(depth):
        g0, b0 = ln_g[l, 0][None], ln_b[l, 0][None]
        if l < n_conv:
            x = _conv_layer(x, mods[l, 0], conv_w_in[l].astype(BF16), conv_w[l],
                            conv_w_out[l].astype(BF16), g0, b0, alpha)
        else:
            j = l - n_conv
            q = _q_proj(x, mods[l, 0], attn_w_q[j].astype(BF16), head_dim ** -0.5)
            o = _attention(q, k, kaug, v_t, cq)
            x = _out_proj(o, x, mods[l, 0], attn_w_o[j].astype(BF16), g0, b0, alpha)
        x = _moe_layer(x.reshape(bsz * seq, d), mods[l, 1], w_r[l], b_r[l], w_gu[l],
                       exp_b_gu[l][:, None, :], w_dn[l], exp_b_down[l][:, None, :],
                       ln_g[l, 1][None], ln_b[l, 1][None], alpha, seq).reshape(bsz, seq, d)
        if l == n_conv - 1:
            w_f = jnp.pad(w_kvf[:, 2 * d:], ((0, 0), (0, LANES - N_HEADS))).astype(BF16)
            bias_f = jnp.pad(b_f, (0, LANES - N_HEADS))[None]
            k, kaug, v_t, cum = _shared_kv(x, kv_mod, w_kvf[:, :d].astype(BF16),
                                           w_kvf[:, d:2 * d].astype(BF16), w_f, bias_f)
            cq = cum.reshape(bsz, seq, N_HEADS // HEAD_PAIR, HEAD_PAIR).transpose(0, 2, 3, 1)
    return x
```

```python
import functools

import numpy as np
import jax
import jax.numpy as jnp
from jax import lax
from jax.experimental import pallas as pl
from jax.experimental.pallas import tpu as pltpu

N_HEADS = 16
TOP_K = 4
SWIGLU_LIMIT = 7.0
SWIGLU_ALPHA = 1.702
LN_EPS = 1e-5
LANES = 128
HEAD_PAIR = 2
PAIR_GROUP = 2
LOG2E = 1.4426950408889634
SEQ_TILE = 512
MOE_TILE = 512
ATTN_TILE = 512
ATTN_KEY_TILE = 512
VMEM_LIMIT = 56 * 1024 * 1024

F32 = jnp.float32
BF16 = jnp.bfloat16
NEG_INF = float("-inf")


def _params(*sem):
    return pltpu.CompilerParams(dimension_semantics=sem, vmem_limit_bytes=VMEM_LIMIT)


def _layer_norm(r, g, b):
    mu = jnp.mean(r, axis=-1, keepdims=True)
    d = r - mu
    var = jnp.mean(d * d, axis=-1, keepdims=True)
    return d * lax.rsqrt(var + LN_EPS) * g + b


def _modulate(x, mod_ref):
    return x * (1.0 + mod_ref[0, 1:2, :]) + mod_ref[0, 0:1, :]


def _ada_kernel(c_ref, w_ref, b_ref, o_ref):
    c = c_ref[...]
    cond = c * jax.nn.sigmoid(c)
    o_ref[0] = jnp.dot(cond, w_ref[0], precision=lax.Precision.HIGHEST,
                       preferred_element_type=F32) + b_ref[0]


def _ada_params(c_pad, w, b):
    g, d, n = w.shape
    tn = 1024 if n % 1024 == 0 else n
    return pl.pallas_call(
        _ada_kernel,
        grid=(g, n // tn),
        in_specs=[pl.BlockSpec((8, d), lambda i, j: (0, 0)),
                  pl.BlockSpec((1, d, tn), lambda i, j: (i, 0, j)),
                  pl.BlockSpec((1, 1, tn), lambda i, j: (i, 0, j))],
        out_specs=pl.BlockSpec((1, 8, tn), lambda i, j: (i, 0, j)),
        out_shape=jax.ShapeDtypeStruct((g, 8, n), F32),
        compiler_params=_params("arbitrary", "arbitrary"),
        name="ada_params",
    )(c_pad, w, b)


def _conv_layer_kernel(x_ref, mod_ref, win_ref, wc_ref, wout_ref, g_ref, b_ref, o_ref,
                       carry_ref, a_ref, *, alpha, col_chunk):
    ts, d = x_ref.shape[1], x_ref.shape[2]

    @pl.when(pl.program_id(1) == 0)
    def _():
        carry_ref[...] = jnp.zeros_like(carry_ref)

    x = x_ref[0]
    h = _modulate(x, mod_ref).astype(BF16)
    row = lax.broadcasted_iota(jnp.int32, (ts, col_chunk), 0)
    for c in range(0, d, col_chunk):
        gate_c = jnp.dot(h, win_ref[:, c:c + col_chunk], preferred_element_type=F32)
        u = jnp.dot(h, win_ref[:, 2 * d + c:2 * d + c + col_chunk], preferred_element_type=F32)
        z = gate_c * u
        z_m1 = carry_ref[1:2, c:c + col_chunk]
        z_m2 = carry_ref[0:1, c:c + col_chunk]
        z1 = jnp.where(row == 0, z_m1, pltpu.roll(z, 1, 0))
        z2 = jnp.where(row == 0, z_m2, jnp.where(row == 1, z_m1, pltpu.roll(z, 2, 0)))
        carry_ref[0:2, c:c + col_chunk] = z[ts - 2:ts, :]
        conv = (wc_ref[0:1, c:c + col_chunk] * z2 + wc_ref[1:2, c:c + col_chunk] * z1
                + wc_ref[2:3, c:c + col_chunk] * z)
        gate_b = jnp.dot(h, win_ref[:, d + c:d + c + col_chunk], preferred_element_type=F32)
        a_ref[:, c:c + col_chunk] = (gate_b * conv).astype(BF16)
    y = jnp.dot(a_ref[...], wout_ref[...], preferred_element_type=F32)
    r = alpha * x + (1.0 + mod_ref[0, 2:3, :]) * y
    o_ref[0] = _layer_norm(r, g_ref[...], b_ref[...])


def _conv_layer(x, mod, w_in, w_conv, w_out, ln_g, ln_b, alpha):
    bsz, seq, d = x.shape
    ts = SEQ_TILE
    kern = functools.partial(_conv_layer_kernel, alpha=alpha, col_chunk=256)
    return pl.pallas_call(
        kern,
        grid=(bsz, seq // ts),
        in_specs=[pl.BlockSpec((1, ts, d), lambda b, j: (b, j, 0)),
                  pl.BlockSpec((1, 3, d), lambda b, j: (b, 0, 0)),
                  pl.BlockSpec((d, 3 * d), lambda b, j: (0, 0)),
                  pl.BlockSpec((3, d), lambda b, j: (0, 0)),
                  pl.BlockSpec((d, d), lambda b, j: (0, 0)),
                  pl.BlockSpec((1, d), lambda b, j: (0, 0)),
                  pl.BlockSpec((1, d), lambda b, j: (0, 0))],
        out_specs=pl.BlockSpec((1, ts, d), lambda b, j: (b, j, 0)),
        out_shape=jax.ShapeDtypeStruct(x.shape, F32),
        scratch_shapes=[pltpu.VMEM((8, d), F32), pltpu.VMEM((ts, d), BF16)],
        compiler_params=_params("arbitrary", "arbitrary"),
        name="conv_layer",
    )(x, mod, w_in, w_conv, w_out, ln_g, ln_b)


def _pack_bf16_pairs(h):
    half = h.shape[1] // 2
    lo = lax.bitcast_convert_type(h[:, :half].astype(BF16).astype(F32), jnp.int32)
    hi = lax.bitcast_convert_type(h[:, half:].astype(BF16).astype(F32), jnp.int32)
    return lax.shift_right_logical(lo, 16) | hi


def _unpack_bf16_pairs(w):
    lo = lax.bitcast_convert_type(w << 16, F32).astype(BF16)
    hi = lax.bitcast_convert_type(w & jnp.int32(-65536), F32).astype(BF16)
    return lo, hi


def _router_kernel(x_ref, mod_ref, wr_ref, br_ref, hpk_ref, route_ref, cnt_ref, carry_ref):
    tr = x_ref.shape[0]

    @pl.when(pl.program_id(0) == 0)
    def _():
        carry_ref[...] = jnp.zeros_like(carry_ref)

    h = _modulate(x_ref[...], mod_ref)
    hpk_ref[...] = _pack_bf16_pairs(h)
    logits = jnp.dot(h, wr_ref[...], precision=lax.Precision.HIGHEST,
                     preferred_element_type=F32) + br_ref[...]
    lane = lax.broadcasted_iota(jnp.int32, (tr, LANES), 1)
    lane_f = lane.astype(F32)
    work = logits
    vals, idxs, sels = [], [], []
    for _ in range(TOP_K):
        m = jnp.max(work, axis=-1, keepdims=True)
        idx = jnp.min(jnp.where(work == m, lane_f, float(LANES)), axis=-1, keepdims=True)
        sel = lane_f == idx
        vals.append(m)
        idxs.append(idx.astype(jnp.int32))
        sels.append(sel)
        work = jnp.where(sel, NEG_INF, work)
    exps = [jnp.exp(v - vals[0]) for v in vals]
    denom = exps[0] + exps[1] + exps[2] + exps[3]
    chosen = sels[0] | sels[1] | sels[2] | sels[3]
    onehot = jnp.where(chosen, 1.0, 0.0).astype(BF16)
    r_i = lax.broadcasted_iota(jnp.int32, (tr, tr), 0)
    c_i = lax.broadcasted_iota(jnp.int32, (tr, tr), 1)
    tri = jnp.where(c_i < r_i, 1.0, 0.0).astype(BF16)
    before = jnp.dot(tri, onehot, preferred_element_type=F32) + carry_ref[0:1, :]
    out = jnp.zeros((tr, LANES), jnp.int32)
    for k in range(TOP_K):
        rank = jnp.sum(jnp.where(sels[k], before, 0.0), axis=-1, keepdims=True).astype(jnp.int32)
        wgt = lax.bitcast_convert_type(exps[k] / denom, jnp.int32)
        out = jnp.where(lane == k, idxs[k], out)
        out = jnp.where(lane == TOP_K + k, rank, out)
        out = jnp.where(lane == 2 * TOP_K + k, wgt, out)
    route_ref[...] = out
    carry_ref[0:1, :] = carry_ref[0:1, :] + jnp.sum(onehot.astype(F32), axis=0, keepdims=True)
    cnt_ref[...] = carry_ref[...].astype(jnp.int32)


def _router(x2, mod, w_r, b_r, seq):
    t, d = x2.shape
    tr = SEQ_TILE
    per_b = seq // tr
    return pl.pallas_call(
        _router_kernel,
        grid=(t // tr,),
        in_specs=[pl.BlockSpec((tr, d), lambda i: (i, 0)),
                  pl.BlockSpec((1, 3, d), lambda i: (i // per_b, 0, 0)),
                  pl.BlockSpec((d, LANES), lambda i: (0, 0)),
                  pl.BlockSpec((1, LANES), lambda i: (0, 0))],
        out_specs=[pl.BlockSpec((tr, d // 2), lambda i: (i, 0)),
                   pl.BlockSpec((tr, LANES), lambda i: (i, 0)),
                   pl.BlockSpec((8, LANES), lambda i: (0, 0))],
        out_shape=[jax.ShapeDtypeStruct((t, d // 2), jnp.int32),
                   jax.ShapeDtypeStruct((t, LANES), jnp.int32),
                   jax.ShapeDtypeStruct((8, LANES), jnp.int32)],
        scratch_shapes=[pltpu.VMEM((8, LANES), F32)],
        compiler_params=_params("arbitrary"),
        name="moe_router",
    )(x2, mod, w_r, b_r)


def _row_copy(src_hbm, dst_vmem, sem, src_row, dst_row):
    return pltpu.make_async_copy(src_hbm.at[pl.ds(src_row, 1)], dst_vmem.at[pl.ds(dst_row, 1)], sem)


def _moe_gemm_kernel(te_ref, nused_ref, tok_ref, tok_next_ref, hpk_hbm, wgu_ref, bgu_ref, wd_ref, bd_ref,
                     y_ref, xbuf, wgu_bf, wd_bf, sem):
    tm = xbuf.shape[1]
    half = xbuf.shape[2]
    f = wd_bf.shape[0]
    i = pl.program_id(0)
    n_used = nused_ref[0]
    slot = i % 2

    def gather(rows_ref, buf):
        def issue(r, carry):
            _row_copy(hpk_hbm, xbuf.at[buf], sem.at[buf], rows_ref[0, 0, r], r).start()
            return carry
        lax.fori_loop(0, tm, issue, 0, unroll=8)

    @pl.when(i == 0)
    def _():
        gather(tok_ref, 0)

    @pl.when(i + 1 < n_used)
    def _():
        gather(tok_next_ref, 1 - slot)

    @pl.when((i == 0) | (te_ref[i] != te_ref[jnp.maximum(i - 1, 0)]))
    def _():
        wgu_bf[...] = wgu_ref[0, 0].astype(BF16)
        wd_bf[...] = wd_ref[0, 0].astype(BF16)

    @pl.when(i < n_used)
    def _():
        def wait(r, carry):
            _row_copy(hpk_hbm, xbuf.at[slot], sem.at[slot], 0, r).wait()
            return carry
        lax.fori_loop(0, tm, wait, 0, unroll=8)

        lo, hi = _unpack_bf16_pairs(xbuf[slot])
        gu = (jnp.dot(lo, wgu_bf[:half, :], preferred_element_type=F32)
              + jnp.dot(hi, wgu_bf[half:, :], preferred_element_type=F32) + bgu_ref[0, 0])
        g = jnp.minimum(gu[:, :f], SWIGLU_LIMIT)
        u = jnp.clip(gu[:, f:], -SWIGLU_LIMIT, SWIGLU_LIMIT)
        a = g * jax.nn.sigmoid(SWIGLU_ALPHA * g) * (u + 1.0)
        y_ref[...] = jnp.dot(a.astype(BF16), wd_bf[...], preferred_element_type=F32) + bd_ref[0, 0]

    @pl.when(i >= n_used)
    def _():
        y_ref[...] = jnp.zeros_like(y_ref)


def _moe_gemm(layer, tile_expert, n_used, slot_tok, hpk, w_gu, b_gu, w_down, b_down):
    n_tiles = tile_expert.shape[0]
    tm = MOE_TILE
    _, _, d, f2 = w_gu.shape
    f = f2 // 2
    last = lambda i, nu: jnp.minimum(i, nu[0] - 1)
    grid_spec = pltpu.PrefetchScalarGridSpec(
        num_scalar_prefetch=2,
        grid=(n_tiles,),
        in_specs=[pl.BlockSpec((1, 1, tm), lambda i, te, nu: (last(i, nu), 0, 0),
                               memory_space=pltpu.SMEM),
                  pl.BlockSpec((1, 1, tm), lambda i, te, nu: (last(i + 1, nu), 0, 0),
                               memory_space=pltpu.SMEM),
                  pl.BlockSpec(memory_space=pl.ANY),
                  pl.BlockSpec((1, 1, d, f2), lambda i, te, nu: (layer, te[i], 0, 0)),
                  pl.BlockSpec((1, 1, 1, f2), lambda i, te, nu: (layer, te[i], 0, 0)),
                  pl.BlockSpec((1, 1, f, d), lambda i, te, nu: (layer, te[i], 0, 0)),
                  pl.BlockSpec((1, 1, 1, d), lambda i, te, nu: (layer, te[i], 0, 0))],
        out_specs=pl.BlockSpec((tm, d), lambda i, te, nu: (i, 0)),
        scratch_shapes=[pltpu.VMEM((2, tm, d // 2), jnp.int32),
                        pltpu.VMEM((d, f2), BF16),
                        pltpu.VMEM((f, d), BF16),
                        pltpu.SemaphoreType.DMA((2,))],
    )
    return pl.pallas_call(
        _moe_gemm_kernel,
        grid_spec=grid_spec,
        out_shape=jax.ShapeDtypeStruct((n_tiles * tm, d), F32),
        compiler_params=_params("arbitrary"),
        name="moe_experts",
    )(tile_expert, n_used, slot_tok, slot_tok, hpk, w_gu, b_gu, w_down, b_down)


def _combine_kernel(dest_ref, y_hbm, x_ref, mod_ref, route_ref, g_ref, b_ref, o_ref, ybuf, sem, *, alpha):
    tr = x_ref.shape[0]

    def issue(r, carry):
        for k in range(TOP_K):
            _row_copy(y_hbm, ybuf.at[k], sem, dest_ref[0, 0, r * TOP_K + k], r).start()
        return carry
    lax.fori_loop(0, tr, issue, 0, unroll=4)

    def wait(r, carry):
        for k in range(TOP_K):
            _row_copy(y_hbm, ybuf.at[k], sem, 0, r).wait()
        return carry
    lax.fori_loop(0, tr, wait, 0, unroll=4)

    wts = lax.bitcast_convert_type(route_ref[...], F32)
    sub = wts[:, 2 * TOP_K:2 * TOP_K + 1] * ybuf[0]
    for k in range(1, TOP_K):
        sub = sub + wts[:, 2 * TOP_K + k:2 * TOP_K + k + 1] * ybuf[k]
    x = x_ref[...]
    r = alpha * x + (1.0 + mod_ref[0, 2:3, :]) * sub
    o_ref[...] = _layer_norm(r, g_ref[...], b_ref[...])


def _combine(dest, y, x2, mod, route, ln_g, ln_b, alpha, seq):
    t, d = x2.shape
    tr = SEQ_TILE
    per_b = seq // tr
    kern = functools.partial(_combine_kernel, alpha=alpha)
    return pl.pallas_call(
        kern,
        grid=(t // tr,),
        in_specs=[pl.BlockSpec((1, 1, tr * TOP_K), lambda i: (i, 0, 0), memory_space=pltpu.SMEM),
                  pl.BlockSpec(memory_space=pl.ANY),
                  pl.BlockSpec((tr, d), lambda i: (i, 0)),
                  pl.BlockSpec((1, 3, d), lambda i: (i // per_b, 0, 0)),
                  pl.BlockSpec((tr, LANES), lambda i: (i, 0)),
                  pl.BlockSpec((1, d), lambda i: (0, 0)),
                  pl.BlockSpec((1, d), lambda i: (0, 0))],
        out_specs=pl.BlockSpec((tr, d), lambda i: (i, 0)),
        out_shape=jax.ShapeDtypeStruct((t, d), F32),
        scratch_shapes=[pltpu.VMEM((TOP_K, tr, d), F32), pltpu.SemaphoreType.DMA(())],
        compiler_params=_params("arbitrary"),
        name="moe_combine",
    )(dest, y, x2, mod, route, ln_g, ln_b)


def _moe_layer(layer, x2, mod, w_r, b_r, w_gu, b_gu, w_down, b_down, ln_g, ln_b, alpha, seq):
    t, d = x2.shape
    n_exp = w_gu.shape[1]
    tm = MOE_TILE
    hpk, route, cnt = _router(x2, mod, w_r, b_r, seq)
    top_idx = route[:, 0:TOP_K]
    rank = route[:, TOP_K:2 * TOP_K]
    counts = cnt[0, :n_exp]
    padded = (counts + tm - 1) // tm * tm
    pend = jnp.cumsum(padded)
    pstart = pend - padded
    dest = pstart[top_idx] + rank
    n_tiles = t * TOP_K // tm + n_exp
    n_used = (pend[-1] // tm).astype(jnp.int32).reshape(1)
    tile_expert = jnp.minimum(
        jnp.searchsorted(pend, jnp.arange(n_tiles, dtype=jnp.int32) * tm, side="right"),
        n_exp - 1).astype(jnp.int32)
    tile_expert = jnp.where(jnp.arange(n_tiles) < n_used[0], tile_expert,
                            tile_expert[jnp.maximum(n_used[0] - 1, 0)])
    tok = jnp.arange(t * TOP_K, dtype=jnp.int32) // TOP_K
    slot_tok = jnp.zeros((n_tiles * tm,), jnp.int32).at[dest.reshape(-1)].set(tok)
    y = _moe_gemm(layer, tile_expert, n_used, slot_tok.reshape(n_tiles, 1, tm), hpk,
                  w_gu, b_gu, w_down, b_down)
    dest3 = dest.reshape(t // SEQ_TILE, 1, SEQ_TILE * TOP_K)
    return _combine(dest3, y, x2, mod, route, ln_g, ln_b, alpha, seq)


def _split3(v):
    p0 = v.astype(BF16)
    r1 = v - p0.astype(F32)
    p1 = r1.astype(BF16)
    p2 = (r1 - p1.astype(F32)).astype(BF16)
    return p0, p1, p2


AUG_STRIDE = 8
AUG_PARTS = 3


def _aug_constants(d):
    n_pairs = d // LANES
    sel = np.zeros((AUG_PARTS, LANES, d), np.float32)
    ones = np.zeros((1, d), np.float32)
    for p in range(n_pairs):
        for hd in range(HEAD_PAIR):
            base = LANES * p + AUG_STRIDE * hd
            for part in range(AUG_PARTS):
                sel[part, HEAD_PAIR * p + hd, base + part] = 1.0
                ones[0, base + AUG_PARTS + part] = 1.0
    return jnp.asarray(sel, BF16), jnp.asarray(ones, F32)


def _kv_kernel(x_ref, mod_ref, wk_ref, wv_ref, wf_ref, bf_ref, sel_ref, ones_ref,
               k_ref, kaug_ref, vt_ref, cum_ref, carry_ref):
    ts = x_ref.shape[1]
    n_heads = cum_ref.shape[2]
    n_pairs = kaug_ref.shape[1]

    @pl.when(pl.program_id(1) == 0)
    def _():
        carry_ref[...] = jnp.zeros_like(carry_ref)

    h = _modulate(x_ref[0], mod_ref).astype(BF16)
    k_ref[0] = jnp.dot(h, wk_ref[...], preferred_element_type=F32).astype(BF16)
    vt = jnp.dot(h, wv_ref[...], preferred_element_type=F32).T.astype(BF16)
    tk = vt_ref.shape[4]
    for p in range(n_pairs):
        for c in range(ts // tk):
            vt_ref[0, p, c] = vt[LANES * p:LANES * (p + 1), tk * c:tk * (c + 1)]
    fz = jnp.dot(h, wf_ref[...], preferred_element_type=F32) + bf_ref[...]
    log_f = jnp.minimum(fz, 0.0) - jnp.log1p(jnp.exp(-jnp.abs(fz)))
    r_i = lax.broadcasted_iota(jnp.int32, (ts, ts), 0)
    c_i = lax.broadcasted_iota(jnp.int32, (ts, ts), 1)
    tri = jnp.where(c_i <= r_i, 1.0, 0.0).astype(BF16)
    cum = carry_ref[0:1, :]
    for part in _split3(log_f):
        cum = cum + jnp.dot(tri, part, preferred_element_type=F32)
    carry_ref[0:1, :] = cum[ts - 1:ts, :]
    cum = cum * LOG2E
    cum_ref[0] = cum[:, :n_heads]
    aug = ones_ref[...]
    for i, part in enumerate(_split3(cum)):
        aug = aug + jnp.dot(part, sel_ref[i], preferred_element_type=F32)
    aug = aug.astype(BF16)
    for p in range(n_pairs):
        kaug_ref[0, p] = aug[:, LANES * p:LANES * (p + 1)]


def _shared_kv(x, mod, w_k, w_v, w_f, b_f):
    bsz, seq, d = x.shape
    ts = ATTN_TILE
    tk = ATTN_KEY_TILE
    n_pairs = d // LANES
    sel, ones = _aug_constants(d)
    return pl.pallas_call(
        _kv_kernel,
        grid=(bsz, seq // ts),
        in_specs=[pl.BlockSpec((1, ts, d), lambda b, j: (b, j, 0)),
                  pl.BlockSpec((1, 2, d), lambda b, j: (b, 0, 0)),
                  pl.BlockSpec((d, d), lambda b, j: (0, 0)),
                  pl.BlockSpec((d, d), lambda b, j: (0, 0)),
                  pl.BlockSpec((d, LANES), lambda b, j: (0, 0)),
                  pl.BlockSpec((1, LANES), lambda b, j: (0, 0)),
                  pl.BlockSpec((AUG_PARTS, LANES, d), lambda b, j: (0, 0, 0)),
                  pl.BlockSpec((1, d), lambda b, j: (0, 0))],
        out_specs=[pl.BlockSpec((1, ts, d), lambda b, j: (b, j, 0)),
                   pl.BlockSpec((1, n_pairs, ts, LANES), lambda b, j: (b, 0, j, 0)),
                   pl.BlockSpec((1, n_pairs, ts // tk, LANES, tk), lambda b, j: (b, 0, j, 0, 0)),
                   pl.BlockSpec((1, ts, N_HEADS), lambda b, j: (b, j, 0))],
        out_shape=[jax.ShapeDtypeStruct((bsz, seq, d), BF16),
                   jax.ShapeDtypeStruct((bsz, n_pairs, seq, LANES), BF16),
                   jax.ShapeDtypeStruct((bsz, n_pairs, seq // tk, LANES, tk), BF16),
                   jax.ShapeDtypeStruct((bsz, seq, N_HEADS), F32)],
        scratch_shapes=[pltpu.VMEM((8, LANES), F32)],
        compiler_params=_params("arbitrary", "arbitrary"),
        name="shared_kv",
    )(x, mod, w_k, w_v, w_f, b_f, sel, ones)


def _q_proj_kernel(x_ref, mod_ref, wq_ref, q_ref, *, scale):
    h = _modulate(x_ref[0], mod_ref).astype(BF16)
    q_ref[0] = (jnp.dot(h, wq_ref[...], preferred_element_type=F32) * scale).astype(BF16)


def _q_proj(x, mod, w_q, scale):
    bsz, seq, d = x.shape
    ts = SEQ_TILE
    return pl.pallas_call(
        functools.partial(_q_proj_kernel, scale=scale),
        grid=(bsz, seq // ts),
        in_specs=[pl.BlockSpec((1, ts, d), lambda b, j: (b, j, 0)),
                  pl.BlockSpec((1, 3, d), lambda b, j: (b, 0, 0)),
                  pl.BlockSpec((d, d), lambda b, j: (0, 0))],
        out_specs=pl.BlockSpec((1, ts, d), lambda b, j: (b, j, 0)),
        out_shape=jax.ShapeDtypeStruct((bsz, seq, d), BF16),
        compiler_params=_params("arbitrary", "arbitrary"),
        name="q_proj",
    )(x, mod, w_q)


def _attn_kernel(q_ref, k_ref, kaug_ref, vt_ref, cq_ref, o_ref, acc_ref, m_ref, l_ref):
    tq = q_ref.shape[1]
    tk = vt_ref.shape[4]
    per_q = tq // tk
    head_dim = LANES // HEAD_PAIR
    i = pl.program_id(2)
    q_t = q_ref[0].astype(F32).T
    row = lax.broadcasted_iota(jnp.int32, (LANES, tq), 0)
    rhs = []
    for g in range(PAIR_GROUP):
        q_pair = q_t[LANES * g:LANES * (g + 1), :]
        for hd in range(HEAD_PAIR):
            own = (row >= head_dim * hd) & (row < head_dim * (hd + 1))
            parts = _split3(cq_ref[0, g, hd:hd + 1, :])
            base = AUG_STRIDE * hd
            aug = jnp.where((row >= base) & (row < base + AUG_PARTS), -1.0, 0.0)
            for n, part in enumerate(parts):
                aug = jnp.where(row == base + AUG_PARTS + n, part.astype(F32), aug)
            rhs.append(jnp.concatenate([jnp.where(own, q_pair, 0.0).astype(BF16), aug.astype(BF16)],
                                       axis=0))
    acc_ref[...] = jnp.zeros_like(acc_ref)
    m_ref[...] = jnp.full_like(m_ref, NEG_INF)
    l_ref[...] = jnp.zeros_like(l_ref)

    def chunk(j, diagonal, offset=0):
        start = pl.multiple_of(j * tk, tk)
        keys = [jnp.concatenate([k_ref[0, pl.ds(start, tk), LANES * g:LANES * (g + 1)],
                                 kaug_ref[0, g, pl.ds(start, tk), :]], axis=1)
                for g in range(PAIR_GROUP)]

        def scores(n):
            s_t = jnp.dot(keys[n // HEAD_PAIR], rhs[n], preferred_element_type=F32)
            if diagonal:
                k_i = lax.broadcasted_iota(jnp.int32, (tk, tq), 0) + offset
                q_i = lax.broadcasted_iota(jnp.int32, (tk, tq), 1)
                s_t = jnp.where(k_i <= q_i, s_t, NEG_INF)
            return s_t

        n_heads = PAIR_GROUP * HEAD_PAIR
        s_next = scores(0)
        for n in range(n_heads):
            g, hd = divmod(n, HEAD_PAIR)
            s_t = s_next
            if n + 1 < n_heads:
                s_next = scores(n + 1)
            m_old = m_ref[n:n + 1, :]
            m_new = jnp.maximum(m_old, jnp.max(s_t, axis=0, keepdims=True))
            a = jnp.exp2(m_old - m_new)
            p_t = jnp.exp2(s_t - m_new)
            l_ref[n:n + 1, :] = a * l_ref[n:n + 1, :] + jnp.sum(p_t, axis=0, keepdims=True)
            m_ref[n:n + 1, :] = m_new
            rows = slice(head_dim * n, head_dim * (n + 1))
            pv = jnp.dot(vt_ref[0, g, j, head_dim * hd:head_dim * (hd + 1), :], p_t.astype(BF16),
                         preferred_element_type=F32)
            acc_ref[rows, :] = acc_ref[rows, :] * a + pv

    def body(j, carry):
        chunk(j, False)
        return carry
    lax.fori_loop(0, i * per_q, body, 0)
    for sub in range(per_q):
        chunk(i * per_q + sub, True, sub * tk)
    inv = jnp.concatenate([jnp.broadcast_to(1.0 / l_ref[n:n + 1, :], (head_dim, tq))
                           for n in range(PAIR_GROUP * HEAD_PAIR)], axis=0)
    o_ref[0] = (acc_ref[...] * inv).T.astype(o_ref.dtype)


def _attention(q, k, kaug, v_t, cq):
    bsz, seq, d = q.shape
    width = PAIR_GROUP * LANES
    tq = ATTN_TILE
    return pl.pallas_call(
        _attn_kernel,
        grid=(bsz, d // width, seq // tq),
        in_specs=[pl.BlockSpec((1, tq, width), lambda b, p, i: (b, i, p)),
                  pl.BlockSpec((1, seq, width), lambda b, p, i: (b, 0, p)),
                  pl.BlockSpec((1, PAIR_GROUP, seq, LANES), lambda b, p, i: (b, p, 0, 0)),
                  pl.BlockSpec((1, PAIR_GROUP) + v_t.shape[2:], lambda b, p, i: (b, p, 0, 0, 0)),
                  pl.BlockSpec((1, PAIR_GROUP, HEAD_PAIR, tq), lambda b, p, i: (b, p, 0, i))],
        out_specs=pl.BlockSpec((1, tq, width), lambda b, p, i: (b, i, p)),
        out_shape=jax.ShapeDtypeStruct((bsz, seq, d), BF16),
        scratch_shapes=[pltpu.VMEM((width, tq), F32),
                        pltpu.VMEM((8, tq), F32),
                        pltpu.VMEM((8, tq), F32)],
        compiler_params=_params("arbitrary", "arbitrary", "arbitrary"),
        name="fox_attention",
    )(q, k, kaug, v_t, cq)


def _out_proj_kernel(o_ref, x_ref, mod_ref, wo_ref, g_ref, b_ref, out_ref, *, alpha):
    y = jnp.dot(o_ref[0], wo_ref[...], preferred_element_type=F32)
    r = alpha * x_ref[0] + (1.0 + mod_ref[0, 2:3, :]) * y
    out_ref[0] = _layer_norm(r, g_ref[...], b_ref[...])


def _out_proj(o, x, mod, w_o, ln_g, ln_b, alpha):
    bsz, seq, d = x.shape
    ts = SEQ_TILE
    return pl.pallas_call(
        functools.partial(_out_proj_kernel, alpha=alpha),
        grid=(bsz, seq // ts),
        in_specs=[pl.BlockSpec((1, ts, d), lambda b, j: (b, j, 0)),
                  pl.BlockSpec((1, ts, d), lambda b, j: (b, j, 0)),
                  pl.BlockSpec((1, 3, d), lambda b, j: (b, 0, 0)),
                  pl.BlockSpec((d, d), lambda b, j: (0, 0)),
                  pl.BlockSpec((1, d), lambda b, j: (0, 0)),
                  pl.BlockSpec((1, d), lambda b, j: (0, 0))],
        out_specs=pl.BlockSpec((1, ts, d), lambda b, j: (b, j, 0)),
        out_shape=jax.ShapeDtypeStruct(x.shape, F32),
        compiler_params=_params("arbitrary", "arbitrary"),
        name="attn_out_proj",
    )(o, x, mod, w_o, ln_g, ln_b)


def kernel(x, c, conv_w_in, conv_w, conv_w_out, kv_ada_w, kv_ada_b, w_kvf, b_f, attn_w_q, attn_w_o,
           ada_w, ada_b, ln_g, ln_b, router_w, router_b, exp_w_gu, exp_b_gu, exp_w_down, exp_b_down):
    bsz, seq, d = x.shape
    depth = ada_w.shape[0]
    n_conv = conv_w_in.shape[0]
    n_exp = router_w.shape[-1]
    alpha = (2.0 * depth) ** 0.25
    head_dim = d // N_HEADS
    assert head_dim * HEAD_PAIR == LANES and seq % SEQ_TILE == 0 and seq % ATTN_TILE == 0

    c_pad = jnp.pad(c, ((0, 8 - bsz), (0, 0)))
    mods = _ada_params(c_pad, ada_w.reshape(depth * 2, d, 3 * d), ada_b.reshape(depth * 2, 1, 3 * d))
    mods = mods[:, :bsz, :].reshape(depth, 2, bsz, 3, d)
    kv_mod = _ada_params(c_pad, kv_ada_w[None], kv_ada_b[None, None])[0, :bsz].reshape(bsz, 2, d)

    w_r = jnp.pad(router_w, ((0, 0), (0, 0), (0, LANES - n_exp)))
    b_r = jnp.pad(router_b, ((0, 0), (0, LANES - n_exp)), constant_values=-1e30)[:, None, :]
    b_gu = exp_b_gu[:, :, None, :]
    b_dn = exp_b_down[:, :, None, :]
    k = kaug = v_t = cq = None
    for l in range(depth):
        g0, b0 = ln_g[l, 0][None], ln_b[l, 0][None]
        if l < n_conv:
            x = _conv_layer(x, mods[l, 0], conv_w_in[l].astype(BF16), conv_w[l],
                            conv_w_out[l].astype(BF16), g0, b0, alpha)
        else:
            j = l - n_conv
            q = _q_proj(x, mods[l, 0], attn_w_q[j].astype(BF16), head_dim ** -0.5 * LOG2E)
            o = _attention(q, k, kaug, v_t, cq)
            x = _out_proj(o, x, mods[l, 0], attn_w_o[j].astype(BF16), g0, b0, alpha)
        x = _moe_layer(l, x.reshape(bsz * seq, d), mods[l, 1], w_r[l], b_r[l], exp_w_gu, b_gu,
                       exp_w_down, b_dn,
                       ln_g[l, 1][None], ln_b[l, 1][None], alpha, seq).reshape(bsz, seq, d)
        if l == n_conv - 1:
            w_f = jnp.pad(w_kvf[:, 2 * d:], ((0, 0), (0, LANES - N_HEADS))).astype(BF16)
            bias_f = jnp.pad(b_f, (0, LANES - N_HEADS))[None]
            k, kaug, v_t, cum = _shared_kv(x, kv_mod, w_kvf[:, :d].astype(BF16),
                                           w_kvf[:, d:2 * d].astype(BF16), w_f, bias_f)
            cq = cum.reshape(bsz, seq, N_HEADS // HEAD_PAIR, HEAD_PAIR).transpose(0, 2, 3, 1)
    return x
```

```python
import functools

import numpy as np
import jax
import jax.numpy as jnp
from jax import lax
from jax.experimental import pallas as pl
from jax.experimental.pallas import tpu as pltpu

N_HEADS = 16
TOP_K = 4
SWIGLU_LIMIT = 7.0
SWIGLU_ALPHA = 1.702
LN_EPS = 1e-5
LANES = 128
HEAD_PAIR = 2
PAIR_GROUP = 2
LOG2E = 1.4426950408889634
SEQ_TILE = 512
MOE_TILE = 512
ATTN_TILE = 512
ATTN_KEY_TILE = 512
ISSUE_GROUP = 64
VMEM_LIMIT = 56 * 1024 * 1024

F32 = jnp.float32
BF16 = jnp.bfloat16
NEG_INF = float("-inf")


def _params(*sem):
    return pltpu.CompilerParams(dimension_semantics=sem, vmem_limit_bytes=VMEM_LIMIT)


def _layer_norm(r, g, b):
    mu = jnp.mean(r, axis=-1, keepdims=True)
    d = r - mu
    var = jnp.mean(d * d, axis=-1, keepdims=True)
    return d * lax.rsqrt(var + LN_EPS) * g + b


def _modulate(x, mod_ref):
    return x * (1.0 + mod_ref[0, 1:2, :]) + mod_ref[0, 0:1, :]


def _ada_kernel(c_ref, w_ref, b_ref, o_ref):
    c = c_ref[...]
    cond = c * jax.nn.sigmoid(c)
    o_ref[0] = jnp.dot(cond, w_ref[0], precision=lax.Precision.HIGHEST,
                       preferred_element_type=F32) + b_ref[0]


def _ada_params(c_pad, w, b):
    g, d, n = w.shape
    tn = 1024 if n % 1024 == 0 else n
    return pl.pallas_call(
        _ada_kernel,
        grid=(g, n // tn),
        in_specs=[pl.BlockSpec((8, d), lambda i, j: (0, 0)),
                  pl.BlockSpec((1, d, tn), lambda i, j: (i, 0, j)),
                  pl.BlockSpec((1, 1, tn), lambda i, j: (i, 0, j))],
        out_specs=pl.BlockSpec((1, 8, tn), lambda i, j: (i, 0, j)),
        out_shape=jax.ShapeDtypeStruct((g, 8, n), F32),
        compiler_params=_params("arbitrary", "arbitrary"),
        name="ada_params",
    )(c_pad, w, b)


def _conv_layer_kernel(x_ref, mod_ref, win_ref, wc_ref, wout_ref, g_ref, b_ref, o_ref,
                       carry_ref, a_ref, *, alpha, col_chunk):
    ts, d = x_ref.shape[1], x_ref.shape[2]

    @pl.when(pl.program_id(1) == 0)
    def _():
        carry_ref[...] = jnp.zeros_like(carry_ref)

    x = x_ref[0]
    h = _modulate(x, mod_ref).astype(BF16)
    row = lax.broadcasted_iota(jnp.int32, (ts, col_chunk), 0)
    for c in range(0, d, col_chunk):
        gate_c = jnp.dot(h, win_ref[:, c:c + col_chunk], preferred_element_type=F32)
        u = jnp.dot(h, win_ref[:, 2 * d + c:2 * d + c + col_chunk], preferred_element_type=F32)
        z = gate_c * u
        z_m1 = carry_ref[1:2, c:c + col_chunk]
        z_m2 = carry_ref[0:1, c:c + col_chunk]
        z1 = jnp.where(row == 0, z_m1, pltpu.roll(z, 1, 0))
        z2 = jnp.where(row == 0, z_m2, jnp.where(row == 1, z_m1, pltpu.roll(z, 2, 0)))
        carry_ref[0:2, c:c + col_chunk] = z[ts - 2:ts, :]
        conv = (wc_ref[0:1, c:c + col_chunk] * z2 + wc_ref[1:2, c:c + col_chunk] * z1
                + wc_ref[2:3, c:c + col_chunk] * z)
        gate_b = jnp.dot(h, win_ref[:, d + c:d + c + col_chunk], preferred_element_type=F32)
        a_ref[:, c:c + col_chunk] = (gate_b * conv).astype(BF16)
    y = jnp.dot(a_ref[...], wout_ref[...], preferred_element_type=F32)
    r = alpha * x + (1.0 + mod_ref[0, 2:3, :]) * y
    o_ref[0] = _layer_norm(r, g_ref[...], b_ref[...])


def _conv_layer(x, mod, w_in, w_conv, w_out, ln_g, ln_b, alpha):
    bsz, seq, d = x.shape
    ts = SEQ_TILE
    kern = functools.partial(_conv_layer_kernel, alpha=alpha, col_chunk=256)
    return pl.pallas_call(
        kern,
        grid=(bsz, seq // ts),
        in_specs=[pl.BlockSpec((1, ts, d), lambda b, j: (b, j, 0)),
                  pl.BlockSpec((1, 3, d), lambda b, j: (b, 0, 0)),
                  pl.BlockSpec((d, 3 * d), lambda b, j: (0, 0)),
                  pl.BlockSpec((3, d), lambda b, j: (0, 0)),
                  pl.BlockSpec((d, d), lambda b, j: (0, 0)),
                  pl.BlockSpec((1, d), lambda b, j: (0, 0)),
                  pl.BlockSpec((1, d), lambda b, j: (0, 0))],
        out_specs=pl.BlockSpec((1, ts, d), lambda b, j: (b, j, 0)),
        out_shape=jax.ShapeDtypeStruct(x.shape, F32),
        scratch_shapes=[pltpu.VMEM((8, d), F32), pltpu.VMEM((ts, d), BF16)],
        compiler_params=_params("arbitrary", "arbitrary"),
        name="conv_layer",
    )(x, mod, w_in, w_conv, w_out, ln_g, ln_b)


def _pack_bf16_pairs(h):
    half = h.shape[1] // 2
    lo = lax.bitcast_convert_type(h[:, :half].astype(BF16).astype(F32), jnp.int32)
    hi = lax.bitcast_convert_type(h[:, half:].astype(BF16).astype(F32), jnp.int32)
    return lax.shift_right_logical(lo, 16) | hi


def _unpack_bf16_pairs(w):
    lo = lax.bitcast_convert_type(w << 16, F32).astype(BF16)
    hi = lax.bitcast_convert_type(w & jnp.int32(-65536), F32).astype(BF16)
    return lo, hi


def _router_kernel(x_ref, mod_ref, wr_ref, br_ref, hpk_ref, route_ref, cnt_ref, carry_ref):
    tr = x_ref.shape[0]

    @pl.when(pl.program_id(0) == 0)
    def _():
        carry_ref[...] = jnp.zeros_like(carry_ref)

    h = _modulate(x_ref[...], mod_ref)
    hpk_ref[...] = _pack_bf16_pairs(h)
    logits = jnp.dot(h, wr_ref[...], precision=lax.Precision.HIGHEST,
                     preferred_element_type=F32) + br_ref[...]
    lane = lax.broadcasted_iota(jnp.int32, (tr, LANES), 1)
    lane_f = lane.astype(F32)
    work = logits
    vals, idxs, sels = [], [], []
    for _ in range(TOP_K):
        m = jnp.max(work, axis=-1, keepdims=True)
        idx = jnp.min(jnp.where(work == m, lane_f, float(LANES)), axis=-1, keepdims=True)
        sel = lane_f == idx
        vals.append(m)
        idxs.append(idx.astype(jnp.int32))
        sels.append(sel)
        work = jnp.where(sel, NEG_INF, work)
    exps = [jnp.exp(v - vals[0]) for v in vals]
    denom = exps[0] + exps[1] + exps[2] + exps[3]
    chosen = sels[0] | sels[1] | sels[2] | sels[3]
    onehot = jnp.where(chosen, 1.0, 0.0).astype(BF16)
    r_i = lax.broadcasted_iota(jnp.int32, (tr, tr), 0)
    c_i = lax.broadcasted_iota(jnp.int32, (tr, tr), 1)
    tri = jnp.where(c_i < r_i, 1.0, 0.0).astype(BF16)
    before = jnp.dot(tri, onehot, preferred_element_type=F32) + carry_ref[0:1, :]
    out = jnp.zeros((tr, LANES), jnp.int32)
    for k in range(TOP_K):
        rank = jnp.sum(jnp.where(sels[k], before, 0.0), axis=-1, keepdims=True).astype(jnp.int32)
        wgt = lax.bitcast_convert_type(exps[k] / denom, jnp.int32)
        out = jnp.where(lane == k, idxs[k], out)
        out = jnp.where(lane == TOP_K + k, rank, out)
        out = jnp.where(lane == 2 * TOP_K + k, wgt, out)
    route_ref[...] = out
    carry_ref[0:1, :] = carry_ref[0:1, :] + jnp.sum(onehot.astype(F32), axis=0, keepdims=True)
    cnt_ref[...] = carry_ref[...].astype(jnp.int32)


def _router(x2, mod, w_r, b_r, seq):
    t, d = x2.shape
    tr = SEQ_TILE
    per_b = seq // tr
    return pl.pallas_call(
        _router_kernel,
        grid=(t // tr,),
        in_specs=[pl.BlockSpec((tr, d), lambda i: (i, 0)),
                  pl.BlockSpec((1, 3, d), lambda i: (i // per_b, 0, 0)),
                  pl.BlockSpec((d, LANES), lambda i: (0, 0)),
                  pl.BlockSpec((1, LANES), lambda i: (0, 0))],
        out_specs=[pl.BlockSpec((tr, d // 2), lambda i: (i, 0)),
                   pl.BlockSpec((tr, LANES), lambda i: (i, 0)),
                   pl.BlockSpec((8, LANES), lambda i: (0, 0))],
        out_shape=[jax.ShapeDtypeStruct((t, d // 2), jnp.int32),
                   jax.ShapeDtypeStruct((t, LANES), jnp.int32),
                   jax.ShapeDtypeStruct((8, LANES), jnp.int32)],
        scratch_shapes=[pltpu.VMEM((8, LANES), F32)],
        compiler_params=_params("arbitrary"),
        name="moe_router",
    )(x2, mod, w_r, b_r)


def _row_copy(src_hbm, dst_vmem, sem, src_row, dst_row):
    return pltpu.make_async_copy(src_hbm.at[pl.ds(src_row, 1)], dst_vmem.at[pl.ds(dst_row, 1)], sem)


def _moe_gemm_kernel(te_ref, nused_ref, tok_ref, tok_next_ref, hpk_hbm, wgu_ref, bgu_ref, wd_ref, bd_ref,
                     y_ref, xbuf, wgu_bf, wd_bf, sem):
    tm = xbuf.shape[1]
    half = xbuf.shape[2]
    f = wd_bf.shape[0]
    i = pl.program_id(0)
    n_used = nused_ref[0]
    slot = i % 2

    def wait_rows(buf):
        def wait(r, carry):
            _row_copy(hpk_hbm, xbuf.at[buf], sem.at[buf], 0, r).wait()
            return carry
        lax.fori_loop(0, tm, wait, 0, unroll=8)

    @pl.when(i == 0)
    def _():
        def issue(r, carry):
            _row_copy(hpk_hbm, xbuf.at[0], sem.at[0], tok_ref[0, 0, r], r).start()
            return carry
        lax.fori_loop(0, tm, issue, 0, unroll=8)

    @pl.when((i == 0) | (te_ref[i] != te_ref[jnp.maximum(i - 1, 0)]))
    def _():
        wgu_bf[...] = wgu_ref[0, 0].astype(BF16)
        wd_bf[...] = wd_ref[0, 0].astype(BF16)

    @pl.when(i < n_used)
    def _():
        wait_rows(slot)
        for r in range(tm):
            _row_copy(hpk_hbm, xbuf.at[1 - slot], sem.at[1 - slot], tok_next_ref[0, 0, r], r).start()

        lo, hi = _unpack_bf16_pairs(xbuf[slot])
        gu = (jnp.dot(lo, wgu_bf[:half, :], preferred_element_type=F32)
              + jnp.dot(hi, wgu_bf[half:, :], preferred_element_type=F32) + bgu_ref[0, 0])
        g = jnp.minimum(gu[:, :f], SWIGLU_LIMIT)
        u = jnp.clip(gu[:, f:], -SWIGLU_LIMIT, SWIGLU_LIMIT)
        a = g * jax.nn.sigmoid(SWIGLU_ALPHA * g) * (u + 1.0)
        y_ref[...] = jnp.dot(a.astype(BF16), wd_bf[...], preferred_element_type=F32) + bd_ref[0, 0]

    @pl.when(i == n_used - 1)
    def _():
        wait_rows(1 - slot)

    @pl.when(i >= n_used)
    def _():
        y_ref[...] = jnp.zeros_like(y_ref)


def _moe_gemm(layer, tile_expert, n_used, slot_tok, hpk, w_gu, b_gu, w_down, b_down):
    n_tiles = tile_expert.shape[0]
    tm = MOE_TILE
    _, _, d, f2 = w_gu.shape
    f = f2 // 2
    last = lambda i, nu: jnp.minimum(i, nu[0] - 1)
    grid_spec = pltpu.PrefetchScalarGridSpec(
        num_scalar_prefetch=2,
        grid=(n_tiles,),
        in_specs=[pl.BlockSpec((1, 1, tm), lambda i, te, nu: (last(i, nu), 0, 0),
                               memory_space=pltpu.SMEM),
                  pl.BlockSpec((1, 1, tm), lambda i, te, nu: (last(i + 1, nu), 0, 0),
                               memory_space=pltpu.SMEM),
                  pl.BlockSpec(memory_space=pl.ANY),
                  pl.BlockSpec((1, 1, d, f2), lambda i, te, nu: (layer, te[i], 0, 0)),
                  pl.BlockSpec((1, 1, 1, f2), lambda i, te, nu: (layer, te[i], 0, 0)),
                  pl.BlockSpec((1, 1, f, d), lambda i, te, nu: (layer, te[i], 0, 0)),
                  pl.BlockSpec((1, 1, 1, d), lambda i, te, nu: (layer, te[i], 0, 0))],
        out_specs=pl.BlockSpec((tm, d), lambda i, te, nu: (i, 0)),
        scratch_shapes=[pltpu.VMEM((2, tm, d // 2), jnp.int32),
                        pltpu.VMEM((d, f2), BF16),
                        pltpu.VMEM((f, d), BF16),
                        pltpu.SemaphoreType.DMA((2,))],
    )
    return pl.pallas_call(
        _moe_gemm_kernel,
        grid_spec=grid_spec,
        out_shape=jax.ShapeDtypeStruct((n_tiles * tm, d), F32),
        compiler_params=_params("arbitrary"),
        name="moe_experts",
    )(tile_expert, n_used, slot_tok, slot_tok, hpk, w_gu, b_gu, w_down, b_down)


def _combine_kernel(dest_ref, y_hbm, x_ref, mod_ref, route_ref, g_ref, b_ref, o_ref, ybuf, sem, *, alpha):
    tr = x_ref.shape[0]

    rows_per_group = ISSUE_GROUP // TOP_K

    def issue_group(c, carry):
        base = pl.multiple_of(c * rows_per_group, rows_per_group)
        for r in range(rows_per_group):
            for k in range(TOP_K):
                _row_copy(y_hbm, ybuf.at[k], sem, dest_ref[0, 0, (base + r) * TOP_K + k], base + r).start()
        return carry
    lax.fori_loop(0, tr // rows_per_group, issue_group, 0)

    def wait(r, carry):
        for k in range(TOP_K):
            _row_copy(y_hbm, ybuf.at[k], sem, 0, r).wait()
        return carry
    lax.fori_loop(0, tr, wait, 0, unroll=4)

    wts = lax.bitcast_convert_type(route_ref[...], F32)
    sub = wts[:, 2 * TOP_K:2 * TOP_K + 1] * ybuf[0]
    for k in range(1, TOP_K):
        sub = sub + wts[:, 2 * TOP_K + k:2 * TOP_K + k + 1] * ybuf[k]
    x = x_ref[...]
    r = alpha * x + (1.0 + mod_ref[0, 2:3, :]) * sub
    o_ref[...] = _layer_norm(r, g_ref[...], b_ref[...])


def _combine(dest, y, x2, mod, route, ln_g, ln_b, alpha, seq):
    t, d = x2.shape
    tr = SEQ_TILE
    per_b = seq // tr
    kern = functools.partial(_combine_kernel, alpha=alpha)
    return pl.pallas_call(
        kern,
        grid=(t // tr,),
        in_specs=[pl.BlockSpec((1, 1, tr * TOP_K), lambda i: (i, 0, 0), memory_space=pltpu.SMEM),
                  pl.BlockSpec(memory_space=pl.ANY),
                  pl.BlockSpec((tr, d), lambda i: (i, 0)),
                  pl.BlockSpec((1, 3, d), lambda i: (i // per_b, 0, 0)),
                  pl.BlockSpec((tr, LANES), lambda i: (i, 0)),
                  pl.BlockSpec((1, d), lambda i: (0, 0)),
                  pl.BlockSpec((1, d), lambda i: (0, 0))],
        out_specs=pl.BlockSpec((tr, d), lambda i: (i, 0)),
        out_shape=jax.ShapeDtypeStruct((t, d), F32),
        scratch_shapes=[pltpu.VMEM((TOP_K, tr, d), F32), pltpu.SemaphoreType.DMA(())],
        compiler_params=_params("arbitrary"),
        name="moe_combine",
    )(dest, y, x2, mod, route, ln_g, ln_b)


def _moe_layer(layer, x2, mod, w_r, b_r, w_gu, b_gu, w_down, b_down, ln_g, ln_b, alpha, seq):
    t, d = x2.shape
    n_exp = w_gu.shape[1]
    tm = MOE_TILE
    hpk, route, cnt = _router(x2, mod, w_r, b_r, seq)
    top_idx = route[:, 0:TOP_K]
    rank = route[:, TOP_K:2 * TOP_K]
    counts = cnt[0, :n_exp]
    padded = (counts + tm - 1) // tm * tm
    pend = jnp.cumsum(padded)
    pstart = pend - padded
    experts = jnp.arange(n_exp, dtype=jnp.int32)
    dest = rank + jnp.sum(jnp.where(top_idx[:, :, None] == experts, pstart, 0), axis=-1)
    n_tiles = t * TOP_K // tm + n_exp
    n_used = (pend[-1] // tm).astype(jnp.int32).reshape(1)
    tile_start = jnp.minimum(jnp.arange(n_tiles, dtype=jnp.int32), n_used[0] - 1) * tm
    tile_expert = jnp.sum(pend[None, :] <= tile_start[:, None], axis=1).astype(jnp.int32)
    pad_ids = jnp.arange(n_exp * tm, dtype=jnp.int32)
    pad_keys = jnp.where(pad_ids % tm < (padded - counts)[pad_ids // tm], pad_ids // tm, n_exp)
    keys = jnp.concatenate([top_idx.reshape(-1), pad_keys])
    toks = jnp.concatenate([jnp.arange(t * TOP_K, dtype=jnp.int32) // TOP_K,
                            jnp.zeros((n_exp * tm,), jnp.int32)])
    _, slot_tok = lax.sort((keys, toks), num_keys=1, is_stable=True)
    y = _moe_gemm(layer, tile_expert, n_used, slot_tok.reshape(n_tiles, 1, tm), hpk,
                  w_gu, b_gu, w_down, b_down)
    dest3 = dest.reshape(t // SEQ_TILE, 1, SEQ_TILE * TOP_K)
    return _combine(dest3, y, x2, mod, route, ln_g, ln_b, alpha, seq)


def _split3(v):
    p0 = v.astype(BF16)
    r1 = v - p0.astype(F32)
    p1 = r1.astype(BF16)
    p2 = (r1 - p1.astype(F32)).astype(BF16)
    return p0, p1, p2


AUG_STRIDE = 8
AUG_PARTS = 3


def _aug_constants(d):
    n_pairs = d // LANES
    sel = np.zeros((AUG_PARTS, LANES, d), np.float32)
    ones = np.zeros((1, d), np.float32)
    for p in range(n_pairs):
        for hd in range(HEAD_PAIR):
            base = LANES * p + AUG_STRIDE * hd
            for part in range(AUG_PARTS):
                sel[part, HEAD_PAIR * p + hd, base + part] = 1.0
                ones[0, base + AUG_PARTS + part] = 1.0
    return jnp.asarray(sel, BF16), jnp.asarray(ones, F32)


def _kv_kernel(x_ref, mod_ref, wk_ref, wv_ref, wf_ref, bf_ref, sel_ref, ones_ref,
               k_ref, kaug_ref, vt_ref, cum_ref, carry_ref):
    ts = x_ref.shape[1]
    n_heads = cum_ref.shape[2]
    n_pairs = kaug_ref.shape[1]

    @pl.when(pl.program_id(1) == 0)
    def _():
        carry_ref[...] = jnp.zeros_like(carry_ref)

    h = _modulate(x_ref[0], mod_ref).astype(BF16)
    k_ref[0] = jnp.dot(h, wk_ref[...], preferred_element_type=F32).astype(BF16)
    vt = jnp.dot(h, wv_ref[...], preferred_element_type=F32).T.astype(BF16)
    tk = vt_ref.shape[4]
    for p in range(n_pairs):
        for c in range(ts // tk):
            vt_ref[0, p, c] = vt[LANES * p:LANES * (p + 1), tk * c:tk * (c + 1)]
    fz = jnp.dot(h, wf_ref[...], preferred_element_type=F32) + bf_ref[...]
    log_f = jnp.minimum(fz, 0.0) - jnp.log1p(jnp.exp(-jnp.abs(fz)))
    r_i = lax.broadcasted_iota(jnp.int32, (ts, ts), 0)
    c_i = lax.broadcasted_iota(jnp.int32, (ts, ts), 1)
    tri = jnp.where(c_i <= r_i, 1.0, 0.0).astype(BF16)
    cum = carry_ref[0:1, :]
    for part in _split3(log_f):
        cum = cum + jnp.dot(tri, part, preferred_element_type=F32)
    carry_ref[0:1, :] = cum[ts - 1:ts, :]
    cum = cum * LOG2E
    cum_ref[0] = cum[:, :n_heads]
    aug = ones_ref[...]
    for i, part in enumerate(_split3(cum)):
        aug = aug + jnp.dot(part, sel_ref[i], preferred_element_type=F32)
    aug = aug.astype(BF16)
    for p in range(n_pairs):
        kaug_ref[0, p] = aug[:, LANES * p:LANES * (p + 1)]


def _shared_kv(x, mod, w_k, w_v, w_f, b_f):
    bsz, seq, d = x.shape
    ts = ATTN_TILE
    tk = ATTN_KEY_TILE
    n_pairs = d // LANES
    sel, ones = _aug_constants(d)
    return pl.pallas_call(
        _kv_kernel,
        grid=(bsz, seq // ts),
        in_specs=[pl.BlockSpec((1, ts, d), lambda b, j: (b, j, 0)),
                  pl.BlockSpec((1, 2, d), lambda b, j: (b, 0, 0)),
                  pl.BlockSpec((d, d), lambda b, j: (0, 0)),
                  pl.BlockSpec((d, d), lambda b, j: (0, 0)),
                  pl.BlockSpec((d, LANES), lambda b, j: (0, 0)),
                  pl.BlockSpec((1, LANES), lambda b, j: (0, 0)),
                  pl.BlockSpec((AUG_PARTS, LANES, d), lambda b, j: (0, 0, 0)),
                  pl.BlockSpec((1, d), lambda b, j: (0, 0))],
        out_specs=[pl.BlockSpec((1, ts, d), lambda b, j: (b, j, 0)),
                   pl.BlockSpec((1, n_pairs, ts, LANES), lambda b, j: (b, 0, j, 0)),
                   pl.BlockSpec((1, n_pairs, ts // tk, LANES, tk), lambda b, j: (b, 0, j, 0, 0)),
                   pl.BlockSpec((1, ts, N_HEADS), lambda b, j: (b, j, 0))],
        out_shape=[jax.ShapeDtypeStruct((bsz, seq, d), BF16),
                   jax.ShapeDtypeStruct((bsz, n_pairs, seq, LANES), BF16),
                   jax.ShapeDtypeStruct((bsz, n_pairs, seq // tk, LANES, tk), BF16),
                   jax.ShapeDtypeStruct((bsz, seq, N_HEADS), F32)],
        scratch_shapes=[pltpu.VMEM((8, LANES), F32)],
        compiler_params=_params("arbitrary", "arbitrary"),
        name="shared_kv",
    )(x, mod, w_k, w_v, w_f, b_f, sel, ones)


def _q_proj_kernel(x_ref, mod_ref, wq_ref, q_ref, *, scale):
    h = _modulate(x_ref[0], mod_ref).astype(BF16)
    q_ref[0] = (jnp.dot(h, wq_ref[...], preferred_element_type=F32) * scale).astype(BF16)


def _q_proj(x, mod, w_q, scale):
    bsz, seq, d = x.shape
    ts = SEQ_TILE
    return pl.pallas_call(
        functools.partial(_q_proj_kernel, scale=scale),
        grid=(bsz, seq // ts),
        in_specs=[pl.BlockSpec((1, ts, d), lambda b, j: (b, j, 0)),
                  pl.BlockSpec((1, 3, d), lambda b, j: (b, 0, 0)),
                  pl.BlockSpec((d, d), lambda b, j: (0, 0))],
        out_specs=pl.BlockSpec((1, ts, d), lambda b, j: (b, j, 0)),
        out_shape=jax.ShapeDtypeStruct((bsz, seq, d), BF16),
        compiler_params=_params("arbitrary", "arbitrary"),
        name="q_proj",
    )(x, mod, w_q)


def _attn_kernel(q_ref, k_ref, kaug_ref, vt_ref, cq_ref, o_ref, acc_ref, m_ref, l_ref, s_ref):
    tq = q_ref.shape[1]
    tk = vt_ref.shape[4]
    per_q = tq // tk
    head_dim = LANES // HEAD_PAIR
    i = pl.program_id(2)
    q_t = q_ref[0].astype(F32).T
    row = lax.broadcasted_iota(jnp.int32, (LANES, tq), 0)
    rhs = []
    for g in range(PAIR_GROUP):
        q_pair = q_t[LANES * g:LANES * (g + 1), :]
        for hd in range(HEAD_PAIR):
            own = (row >= head_dim * hd) & (row < head_dim * (hd + 1))
            parts = _split3(cq_ref[0, g, hd:hd + 1, :])
            base = AUG_STRIDE * hd
            aug = jnp.where((row >= base) & (row < base + AUG_PARTS), -1.0, 0.0)
            for n, part in enumerate(parts):
                aug = jnp.where(row == base + AUG_PARTS + n, part.astype(F32), aug)
            rhs.append(jnp.concatenate([jnp.where(own, q_pair, 0.0).astype(BF16), aug.astype(BF16)],
                                       axis=0))
    acc_ref[...] = jnp.zeros_like(acc_ref)
    m_ref[...] = jnp.full_like(m_ref, NEG_INF)
    l_ref[...] = jnp.zeros_like(l_ref)

    def keys_of(j, g):
        start = pl.multiple_of(j * tk, tk)
        return jnp.concatenate([k_ref[0, pl.ds(start, tk), LANES * g:LANES * (g + 1)],
                                kaug_ref[0, g, pl.ds(start, tk), :]], axis=1)

    s_ref[...] = jnp.dot(keys_of(0, 0), rhs[0], preferred_element_type=F32)

    def chunk(j, diagonal, offset=0, has_next=True):
        keys = [keys_of(j, g) for g in range(PAIR_GROUP)]

        def masked(s_t):
            if diagonal:
                k_i = lax.broadcasted_iota(jnp.int32, (tk, tq), 0) + offset
                q_i = lax.broadcasted_iota(jnp.int32, (tk, tq), 1)
                s_t = jnp.where(k_i <= q_i, s_t, NEG_INF)
            return s_t

        n_heads = PAIR_GROUP * HEAD_PAIR
        s_next = s_ref[...]
        for n in range(n_heads):
            g, hd = divmod(n, HEAD_PAIR)
            s_t = masked(s_next)
            if n + 1 < n_heads:
                s_next = jnp.dot(keys[(n + 1) // HEAD_PAIR], rhs[n + 1], preferred_element_type=F32)
            elif has_next:
                s_ref[...] = jnp.dot(keys_of(j + 1, 0), rhs[0], preferred_element_type=F32)
            m_old = m_ref[n:n + 1, :]
            m_new = jnp.maximum(m_old, jnp.max(s_t, axis=0, keepdims=True))
            a = jnp.exp2(m_old - m_new)
            p_t = jnp.exp2(s_t - m_new)
            l_ref[n:n + 1, :] = a * l_ref[n:n + 1, :] + jnp.sum(p_t, axis=0, keepdims=True)
            m_ref[n:n + 1, :] = m_new
            rows = slice(head_dim * n, head_dim * (n + 1))
            pv = jnp.dot(vt_ref[0, g, j, head_dim * hd:head_dim * (hd + 1), :], p_t.astype(BF16),
                         preferred_element_type=F32)
            acc_ref[rows, :] = acc_ref[rows, :] * a + pv

    def body(j, carry):
        chunk(j, False)
        return carry
    lax.fori_loop(0, i * per_q, body, 0)
    for sub in range(per_q):
        chunk(i * per_q + sub, True, sub * tk, has_next=sub + 1 < per_q)
    inv = jnp.concatenate([jnp.broadcast_to(1.0 / l_ref[n:n + 1, :], (head_dim, tq))
                           for n in range(PAIR_GROUP * HEAD_PAIR)], axis=0)
    o_ref[0] = (acc_ref[...] * inv).T.astype(o_ref.dtype)


def _attention(q, k, kaug, v_t, cq):
    bsz, seq, d = q.shape
    width = PAIR_GROUP * LANES
    tq = ATTN_TILE
    return pl.pallas_call(
        _attn_kernel,
        grid=(bsz, d // width, seq // tq),
        in_specs=[pl.BlockSpec((1, tq, width), lambda b, p, i: (b, i, p)),
                  pl.BlockSpec((1, seq, width), lambda b, p, i: (b, 0, p)),
                  pl.BlockSpec((1, PAIR_GROUP, seq, LANES), lambda b, p, i: (b, p, 0, 0)),
                  pl.BlockSpec((1, PAIR_GROUP) + v_t.shape[2:], lambda b, p, i: (b, p, 0, 0, 0)),
                  pl.BlockSpec((1, PAIR_GROUP, HEAD_PAIR, tq), lambda b, p, i: (b, p, 0, i))],
        out_specs=pl.BlockSpec((1, tq, width), lambda b, p, i: (b, i, p)),
        out_shape=jax.ShapeDtypeStruct((bsz, seq, d), BF16),
        scratch_shapes=[pltpu.VMEM((width, tq), F32),
                        pltpu.VMEM((8, tq), F32),
                        pltpu.VMEM((8, tq), F32),
                        pltpu.VMEM((v_t.shape[4], tq), F32)],
        compiler_params=_params("arbitrary", "arbitrary", "arbitrary"),
        name="fox_attention",
    )(q, k, kaug, v_t, cq)


def _out_proj_kernel(o_ref, x_ref, mod_ref, wo_ref, g_ref, b_ref, out_ref, *, alpha):
    y = jnp.dot(o_ref[0], wo_ref[...], preferred_element_type=F32)
    r = alpha * x_ref[0] + (1.0 + mod_ref[0, 2:3, :]) * y
    out_ref[0] = _layer_norm(r, g_ref[...], b_ref[...])


def _out_proj(o, x, mod, w_o, ln_g, ln_b, alpha):
    bsz, seq, d = x.shape
    ts = SEQ_TILE
    return pl.pallas_call(
        functools.partial(_out_proj_kernel, alpha=alpha),
        grid=(bsz, seq // ts),
        in_specs=[pl.BlockSpec((1, ts, d), lambda b, j: (b, j, 0)),
                  pl.BlockSpec((1, ts, d), lambda b, j: (b, j, 0)),
                  pl.BlockSpec((1, 3, d), lambda b, j: (b, 0, 0)),
                  pl.BlockSpec((d, d), lambda b, j: (0, 0)),
                  pl.BlockSpec((1, d), lambda b, j: (0, 0)),
                  pl.BlockSpec((1, d), lambda b, j: (0, 0))],
        out_specs=pl.BlockSpec((1, ts, d), lambda b, j: (b, j, 0)),
        out_shape=jax.ShapeDtypeStruct(x.shape, F32),
        compiler_params=_params("arbitrary", "arbitrary"),
        name="attn_out_proj",
    )(o, x, mod, w_o, ln_g, ln_b)


def kernel(x, c, conv_w_in, conv_w, conv_w_out, kv_ada_w, kv_ada_b, w_kvf, b_f, attn_w_q, attn_w_o,
           ada_w, ada_b, ln_g, ln_b, router_w, router_b, exp_w_gu, exp_b_gu, exp_w_down, exp_b_down):
    bsz, seq, d = x.shape
    depth = ada_w.shape[0]
    n_conv = conv_w_in.shape[0]
    n_exp = router_w.shape[-1]
    alpha = (2.0 * depth) ** 0.25
    head_dim = d // N_HEADS
    assert head_dim * HEAD_PAIR == LANES and seq % SEQ_TILE == 0 and seq % ATTN_TILE == 0

    c_pad = jnp.pad(c, ((0, 8 - bsz), (0, 0)))
    mods = _ada_params(c_pad, ada_w.reshape(depth * 2, d, 3 * d), ada_b.reshape(depth * 2, 1, 3 * d))
    mods = mods[:, :bsz, :].reshape(depth, 2, bsz, 3, d)
    kv_mod = _ada_params(c_pad, kv_ada_w[None], kv_ada_b[None, None])[0, :bsz].reshape(bsz, 2, d)

    w_r = jnp.pad(router_w, ((0, 0), (0, 0), (0, LANES - n_exp)))
    b_r = jnp.pad(router_b, ((0, 0), (0, LANES - n_exp)), constant_values=-1e30)[:, None, :]
    b_gu = exp_b_gu[:, :, None, :]
    b_dn = exp_b_down[:, :, None, :]
    k = kaug = v_t = cq = None
    for l in range(depth):
        g0, b0 = ln_g[l, 0][None], ln_b[l, 0][None]
        if l < n_conv:
            x = _conv_layer(x, mods[l, 0], conv_w_in[l].astype(BF16), conv_w[l],
                            conv_w_out[l].astype(BF16), g0, b0, alpha)
        else:
            j = l - n_conv
            q = _q_proj(x, mods[l, 0], attn_w_q[j].astype(BF16), head_dim ** -0.5 * LOG2E)
            o = _attention(q, k, kaug, v_t, cq)
            x = _out_proj(o, x, mods[l, 0], attn_w_o[j].astype(BF16), g0, b0, alpha)
        x = _moe_layer(l, x.reshape(bsz * seq, d), mods[l, 1], w_r[l], b_r[l], exp_w_gu, b_gu,
                       exp_w_down, b_dn,
                       ln_g[l, 1][None], ln_b[l, 1][None], alpha, seq).reshape(bsz, seq, d)
        if l == n_conv - 1:
            w_f = jnp.pad(w_kvf[:, 2 * d:], ((0, 0), (0, LANES - N_HEADS))).astype(BF16)
            bias_f = jnp.pad(b_f, (0, LANES - N_HEADS))[None]
            k, kaug, v_t, cum = _shared_kv(x, kv_mod, w_kvf[:, :d].astype(BF16),
                                           w_kvf[:, d:2 * d].astype(BF16), w_f, bias_f)
            cq = cum.reshape(bsz, seq, N_HEADS // HEAD_PAIR, HEAD_PAIR).transpose(0, 2, 3, 1)
    return x
```

```python
import functools

import numpy as np
import jax
import jax.numpy as jnp
from jax import lax
from jax.experimental import pallas as pl
from jax.experimental.pallas import tpu as pltpu

N_HEADS = 16
TOP_K = 4
SWIGLU_LIMIT = 7.0
SWIGLU_ALPHA = 1.702
LN_EPS = 1e-5
LANES = 128
HEAD_PAIR = 2
PAIR_GROUP = 2
LOG2E = 1.4426950408889634
SEQ_TILE = 512
MOE_TILE = 512
ATTN_TILE = 512
ATTN_KEY_TILE = 512
ISSUE_GROUP = 64
VMEM_LIMIT = 56 * 1024 * 1024

F32 = jnp.float32
BF16 = jnp.bfloat16
NEG_INF = float("-inf")


def _params(*sem):
    return pltpu.CompilerParams(dimension_semantics=sem, vmem_limit_bytes=VMEM_LIMIT)


def _layer_norm(r, g, b):
    mu = jnp.mean(r, axis=-1, keepdims=True)
    d = r - mu
    var = jnp.mean(d * d, axis=-1, keepdims=True)
    return d * lax.rsqrt(var + LN_EPS) * g + b


def _modulate(x, mod_ref):
    return x * (1.0 + mod_ref[0, 1:2, :]) + mod_ref[0, 0:1, :]


def _ada_kernel(c_ref, w_ref, b_ref, o_ref):
    c = c_ref[...]
    cond = c * jax.nn.sigmoid(c)
    o_ref[0] = jnp.dot(cond, w_ref[0], precision=lax.Precision.HIGHEST,
                       preferred_element_type=F32) + b_ref[0]


def _ada_params(c_pad, w, b):
    g, d, n = w.shape
    tn = 1024 if n % 1024 == 0 else n
    return pl.pallas_call(
        _ada_kernel,
        grid=(g, n // tn),
        in_specs=[pl.BlockSpec((8, d), lambda i, j: (0, 0)),
                  pl.BlockSpec((1, d, tn), lambda i, j: (i, 0, j)),
                  pl.BlockSpec((1, 1, tn), lambda i, j: (i, 0, j))],
        out_specs=pl.BlockSpec((1, 8, tn), lambda i, j: (i, 0, j)),
        out_shape=jax.ShapeDtypeStruct((g, 8, n), F32),
        compiler_params=_params("arbitrary", "arbitrary"),
        name="ada_params",
    )(c_pad, w, b)


def _conv_layer_kernel(x_ref, mod_ref, win_ref, wc_ref, wout_ref, g_ref, b_ref, o_ref,
                       carry_ref, a_ref, *, alpha, col_chunk):
    ts, d = x_ref.shape[1], x_ref.shape[2]

    @pl.when(pl.program_id(1) == 0)
    def _():
        carry_ref[...] = jnp.zeros_like(carry_ref)

    x = x_ref[0]
    h = _modulate(x, mod_ref).astype(BF16)
    row = lax.broadcasted_iota(jnp.int32, (ts, col_chunk), 0)
    for c in range(0, d, col_chunk):
        gate_c = jnp.dot(h, win_ref[:, c:c + col_chunk], preferred_element_type=F32)
        u = jnp.dot(h, win_ref[:, 2 * d + c:2 * d + c + col_chunk], preferred_element_type=F32)
        z = gate_c * u
        z_m1 = carry_ref[1:2, c:c + col_chunk]
        z_m2 = carry_ref[0:1, c:c + col_chunk]
        z1 = jnp.where(row == 0, z_m1, pltpu.roll(z, 1, 0))
        z2 = jnp.where(row == 0, z_m2, jnp.where(row == 1, z_m1, pltpu.roll(z, 2, 0)))
        carry_ref[0:2, c:c + col_chunk] = z[ts - 2:ts, :]
        conv = (wc_ref[0:1, c:c + col_chunk] * z2 + wc_ref[1:2, c:c + col_chunk] * z1
                + wc_ref[2:3, c:c + col_chunk] * z)
        gate_b = jnp.dot(h, win_ref[:, d + c:d + c + col_chunk], preferred_element_type=F32)
        a_ref[:, c:c + col_chunk] = (gate_b * conv).astype(BF16)
    y = jnp.dot(a_ref[...], wout_ref[...], preferred_element_type=F32)
    r = alpha * x + (1.0 + mod_ref[0, 2:3, :]) * y
    o_ref[0] = _layer_norm(r, g_ref[...], b_ref[...])


def _conv_layer(x, mod, w_in, w_conv, w_out, ln_g, ln_b, alpha):
    bsz, seq, d = x.shape
    ts = SEQ_TILE
    kern = functools.partial(_conv_layer_kernel, alpha=alpha, col_chunk=256)
    return pl.pallas_call(
        kern,
        grid=(bsz, seq // ts),
        in_specs=[pl.BlockSpec((1, ts, d), lambda b, j: (b, j, 0)),
                  pl.BlockSpec((1, 3, d), lambda b, j: (b, 0, 0)),
                  pl.BlockSpec((d, 3 * d), lambda b, j: (0, 0)),
                  pl.BlockSpec((3, d), lambda b, j: (0, 0)),
                  pl.BlockSpec((d, d), lambda b, j: (0, 0)),
                  pl.BlockSpec((1, d), lambda b, j: (0, 0)),
                  pl.BlockSpec((1, d), lambda b, j: (0, 0))],
        out_specs=pl.BlockSpec((1, ts, d), lambda b, j: (b, j, 0)),
        out_shape=jax.ShapeDtypeStruct(x.shape, F32),
        scratch_shapes=[pltpu.VMEM((8, d), F32), pltpu.VMEM((ts, d), BF16)],
        compiler_params=_params("arbitrary", "arbitrary"),
        name="conv_layer",
    )(x, mod, w_in, w_conv, w_out, ln_g, ln_b)


def _pack_bf16_pairs(h):
    half = h.shape[1] // 2
    lo = lax.bitcast_convert_type(h[:, :half].astype(BF16).astype(F32), jnp.int32)
    hi = lax.bitcast_convert_type(h[:, half:].astype(BF16).astype(F32), jnp.int32)
    return lax.shift_right_logical(lo, 16) | hi


def _unpack_bf16_pairs(w):
    lo = lax.bitcast_convert_type(w << 16, F32).astype(BF16)
    hi = lax.bitcast_convert_type(w & jnp.int32(-65536), F32).astype(BF16)
    return lo, hi


def _router_kernel(x_ref, mod_ref, wr_ref, br_ref, hpk_ref, route_ref, cnt_ref, carry_ref):
    tr = x_ref.shape[0]

    @pl.when(pl.program_id(0) == 0)
    def _():
        carry_ref[...] = jnp.zeros_like(carry_ref)

    h = _modulate(x_ref[...], mod_ref)
    hpk_ref[...] = _pack_bf16_pairs(h)
    logits = jnp.dot(h, wr_ref[...], precision=lax.Precision.HIGHEST,
                     preferred_element_type=F32) + br_ref[...]
    lane = lax.broadcasted_iota(jnp.int32, (tr, LANES), 1)
    lane_f = lane.astype(F32)
    work = logits
    vals, idxs, sels = [], [], []
    for _ in range(TOP_K):
        m = jnp.max(work, axis=-1, keepdims=True)
        idx = jnp.min(jnp.where(work == m, lane_f, float(LANES)), axis=-1, keepdims=True)
        sel = lane_f == idx
        vals.append(m)
        idxs.append(idx.astype(jnp.int32))
        sels.append(sel)
        work = jnp.where(sel, NEG_INF, work)
    exps = [jnp.exp(v - vals[0]) for v in vals]
    denom = exps[0] + exps[1] + exps[2] + exps[3]
    chosen = sels[0] | sels[1] | sels[2] | sels[3]
    onehot = jnp.where(chosen, 1.0, 0.0).astype(BF16)
    r_i = lax.broadcasted_iota(jnp.int32, (tr, tr), 0)
    c_i = lax.broadcasted_iota(jnp.int32, (tr, tr), 1)
    tri = jnp.where(c_i < r_i, 1.0, 0.0).astype(BF16)
    before = jnp.dot(tri, onehot, preferred_element_type=F32) + carry_ref[0:1, :]
    out = jnp.zeros((tr, LANES), jnp.int32)
    for k in range(TOP_K):
        rank = jnp.sum(jnp.where(sels[k], before, 0.0), axis=-1, keepdims=True).astype(jnp.int32)
        wgt = lax.bitcast_convert_type(exps[k] / denom, jnp.int32)
        out = jnp.where(lane == k, idxs[k], out)
        out = jnp.where(lane == TOP_K + k, rank, out)
        out = jnp.where(lane == 2 * TOP_K + k, wgt, out)
    route_ref[...] = out
    carry_ref[0:1, :] = carry_ref[0:1, :] + jnp.sum(onehot.astype(F32), axis=0, keepdims=True)
    cnt_ref[...] = carry_ref[...].astype(jnp.int32)


def _router(x2, mod, w_r, b_r, seq):
    t, d = x2.shape
    tr = SEQ_TILE
    per_b = seq // tr
    return pl.pallas_call(
        _router_kernel,
        grid=(t // tr,),
        in_specs=[pl.BlockSpec((tr, d), lambda i: (i, 0)),
                  pl.BlockSpec((1, 3, d), lambda i: (i // per_b, 0, 0)),
                  pl.BlockSpec((d, LANES), lambda i: (0, 0)),
                  pl.BlockSpec((1, LANES), lambda i: (0, 0))],
        out_specs=[pl.BlockSpec((tr, d // 2), lambda i: (i, 0)),
                   pl.BlockSpec((tr, LANES), lambda i: (i, 0)),
                   pl.BlockSpec((8, LANES), lambda i: (0, 0))],
        out_shape=[jax.ShapeDtypeStruct((t, d // 2), jnp.int32),
                   jax.ShapeDtypeStruct((t, LANES), jnp.int32),
                   jax.ShapeDtypeStruct((8, LANES), jnp.int32)],
        scratch_shapes=[pltpu.VMEM((8, LANES), F32)],
        compiler_params=_params("arbitrary"),
        name="moe_router",
    )(x2, mod, w_r, b_r)


def _row_copy(src_hbm, dst_vmem, sem, src_row, dst_row):
    return pltpu.make_async_copy(src_hbm.at[pl.ds(src_row, 1)], dst_vmem.at[pl.ds(dst_row, 1)], sem)


def _moe_gemm_kernel(te_ref, nused_ref, tok_ref, tok_next_ref, hpk_hbm, wgu_ref, bgu_ref, wd_ref, bd_ref,
                     y_ref, xbuf, wgu_bf, wd_bf, sem):
    tm = xbuf.shape[1]
    half = xbuf.shape[2]
    f = wd_bf.shape[0]
    i = pl.program_id(0)
    n_used = nused_ref[0]
    slot = i % 2

    def wait_rows(buf):
        pltpu.make_async_copy(hpk_hbm.at[pl.ds(0, tm)], xbuf.at[buf], sem.at[buf]).wait()

    @pl.when(i == 0)
    def _():
        def issue(r, carry):
            _row_copy(hpk_hbm, xbuf.at[0], sem.at[0], tok_ref[0, 0, r], r).start()
            return carry
        lax.fori_loop(0, tm, issue, 0, unroll=8)

    @pl.when((i == 0) | (te_ref[i] != te_ref[jnp.maximum(i - 1, 0)]))
    def _():
        wgu_bf[...] = wgu_ref[0, 0].astype(BF16)
        wd_bf[...] = wd_ref[0, 0].astype(BF16)

    @pl.when(i < n_used)
    def _():
        wait_rows(slot)
        for r in range(tm):
            _row_copy(hpk_hbm, xbuf.at[1 - slot], sem.at[1 - slot], tok_next_ref[0, 0, r], r).start()

        lo, hi = _unpack_bf16_pairs(xbuf[slot])
        gu = (jnp.dot(lo, wgu_bf[:half, :], preferred_element_type=F32)
              + jnp.dot(hi, wgu_bf[half:, :], preferred_element_type=F32) + bgu_ref[0, 0])
        g = jnp.minimum(gu[:, :f], SWIGLU_LIMIT)
        u = jnp.clip(gu[:, f:], -SWIGLU_LIMIT, SWIGLU_LIMIT)
        a = g * jax.nn.sigmoid(SWIGLU_ALPHA * g) * (u + 1.0)
        y_ref[...] = jnp.dot(a.astype(BF16), wd_bf[...], preferred_element_type=F32) + bd_ref[0, 0]

    @pl.when(i == n_used - 1)
    def _():
        wait_rows(1 - slot)

    @pl.when(i >= n_used)
    def _():
        y_ref[...] = jnp.zeros_like(y_ref)


def _moe_gemm(layer, tile_expert, n_used, slot_tok, hpk, w_gu, b_gu, w_down, b_down):
    n_tiles = tile_expert.shape[0]
    tm = MOE_TILE
    _, _, d, f2 = w_gu.shape
    f = f2 // 2
    last = lambda i, nu: jnp.minimum(i, nu[0] - 1)
    grid_spec = pltpu.PrefetchScalarGridSpec(
        num_scalar_prefetch=2,
        grid=(n_tiles,),
        in_specs=[pl.BlockSpec((1, 1, tm), lambda i, te, nu: (last(i, nu), 0, 0),
                               memory_space=pltpu.SMEM),
                  pl.BlockSpec((1, 1, tm), lambda i, te, nu: (last(i + 1, nu), 0, 0),
                               memory_space=pltpu.SMEM),
                  pl.BlockSpec(memory_space=pl.ANY),
                  pl.BlockSpec((1, 1, d, f2), lambda i, te, nu: (layer, te[i], 0, 0)),
                  pl.BlockSpec((1, 1, 1, f2), lambda i, te, nu: (layer, te[i], 0, 0)),
                  pl.BlockSpec((1, 1, f, d), lambda i, te, nu: (layer, te[i], 0, 0)),
                  pl.BlockSpec((1, 1, 1, d), lambda i, te, nu: (layer, te[i], 0, 0))],
        out_specs=pl.BlockSpec((tm, d), lambda i, te, nu: (i, 0)),
        scratch_shapes=[pltpu.VMEM((2, tm, d // 2), jnp.int32),
                        pltpu.VMEM((d, f2), BF16),
                        pltpu.VMEM((f, d), BF16),
                        pltpu.SemaphoreType.DMA((2,))],
    )
    return pl.pallas_call(
        _moe_gemm_kernel,
        grid_spec=grid_spec,
        out_shape=jax.ShapeDtypeStruct((n_tiles * tm, d), F32),
        compiler_params=_params("arbitrary"),
        name="moe_experts",
    )(tile_expert, n_used, slot_tok, slot_tok, hpk, w_gu, b_gu, w_down, b_down)


def _combine_kernel(dest_ref, y_hbm, x_ref, mod_ref, route_ref, g_ref, b_ref, o_ref, ybuf, sem, *, alpha):
    tr = x_ref.shape[0]

    rows_per_group = ISSUE_GROUP // TOP_K

    def issue_group(c, carry):
        base = pl.multiple_of(c * rows_per_group, rows_per_group)
        for r in range(rows_per_group):
            for k in range(TOP_K):
                _row_copy(y_hbm, ybuf.at[k], sem, dest_ref[0, 0, (base + r) * TOP_K + k], base + r).start()
        return carry
    lax.fori_loop(0, tr // rows_per_group, issue_group, 0)

    for k in range(TOP_K):
        pltpu.make_async_copy(y_hbm.at[pl.ds(0, tr)], ybuf.at[k], sem).wait()

    wts = lax.bitcast_convert_type(route_ref[...], F32)
    sub = wts[:, 2 * TOP_K:2 * TOP_K + 1] * ybuf[0]
    for k in range(1, TOP_K):
        sub = sub + wts[:, 2 * TOP_K + k:2 * TOP_K + k + 1] * ybuf[k]
    x = x_ref[...]
    r = alpha * x + (1.0 + mod_ref[0, 2:3, :]) * sub
    o_ref[...] = _layer_norm(r, g_ref[...], b_ref[...])


def _combine(dest, y, x2, mod, route, ln_g, ln_b, alpha, seq):
    t, d = x2.shape
    tr = SEQ_TILE
    per_b = seq // tr
    kern = functools.partial(_combine_kernel, alpha=alpha)
    return pl.pallas_call(
        kern,
        grid=(t // tr,),
        in_specs=[pl.BlockSpec((1, 1, tr * TOP_K), lambda i: (i, 0, 0), memory_space=pltpu.SMEM),
                  pl.BlockSpec(memory_space=pl.ANY),
                  pl.BlockSpec((tr, d), lambda i: (i, 0)),
                  pl.BlockSpec((1, 3, d), lambda i: (i // per_b, 0, 0)),
                  pl.BlockSpec((tr, LANES), lambda i: (i, 0)),
                  pl.BlockSpec((1, d), lambda i: (0, 0)),
                  pl.BlockSpec((1, d), lambda i: (0, 0))],
        out_specs=pl.BlockSpec((tr, d), lambda i: (i, 0)),
        out_shape=jax.ShapeDtypeStruct((t, d), F32),
        scratch_shapes=[pltpu.VMEM((TOP_K, tr, d), F32), pltpu.SemaphoreType.DMA(())],
        compiler_params=_params("arbitrary"),
        name="moe_combine",
    )(dest, y, x2, mod, route, ln_g, ln_b)


def _moe_layer(layer, x2, mod, w_r, b_r, w_gu, b_gu, w_down, b_down, ln_g, ln_b, alpha, seq):
    t, d = x2.shape
    n_exp = w_gu.shape[1]
    tm = MOE_TILE
    hpk, route, cnt = _router(x2, mod, w_r, b_r, seq)
    top_idx = route[:, 0:TOP_K]
    rank = route[:, TOP_K:2 * TOP_K]
    counts = cnt[0, :n_exp]
    padded = (counts + tm - 1) // tm * tm
    pend = jnp.cumsum(padded)
    pstart = pend - padded
    experts = jnp.arange(n_exp, dtype=jnp.int32)
    dest = rank + jnp.sum(jnp.where(top_idx[:, :, None] == experts, pstart, 0), axis=-1)
    n_tiles = t * TOP_K // tm + n_exp
    n_used = (pend[-1] // tm).astype(jnp.int32).reshape(1)
    tile_start = jnp.minimum(jnp.arange(n_tiles, dtype=jnp.int32), n_used[0] - 1) * tm
    tile_expert = jnp.sum(pend[None, :] <= tile_start[:, None], axis=1).astype(jnp.int32)
    pad_ids = jnp.arange(n_exp * tm, dtype=jnp.int32)
    pad_keys = jnp.where(pad_ids % tm < (padded - counts)[pad_ids // tm], pad_ids // tm, n_exp)
    keys = jnp.concatenate([top_idx.reshape(-1), pad_keys])
    toks = jnp.concatenate([jnp.arange(t * TOP_K, dtype=jnp.int32) // TOP_K,
                            jnp.zeros((n_exp * tm,), jnp.int32)])
    _, slot_tok = lax.sort((keys, toks), num_keys=1, is_stable=True)
    y = _moe_gemm(layer, tile_expert, n_used, slot_tok.reshape(n_tiles, 1, tm), hpk,
                  w_gu, b_gu, w_down, b_down)
    dest3 = dest.reshape(t // SEQ_TILE, 1, SEQ_TILE * TOP_K)
    return _combine(dest3, y, x2, mod, route, ln_g, ln_b, alpha, seq)


def _split3(v):
    p0 = v.astype(BF16)
    r1 = v - p0.astype(F32)
    p1 = r1.astype(BF16)
    p2 = (r1 - p1.astype(F32)).astype(BF16)
    return p0, p1, p2


AUG_STRIDE = 8
AUG_PARTS = 3


def _aug_constants(d):
    n_pairs = d // LANES
    sel = np.zeros((AUG_PARTS, LANES, d), np.float32)
    ones = np.zeros((1, d), np.float32)
    for p in range(n_pairs):
        for hd in range(HEAD_PAIR):
            base = LANES * p + AUG_STRIDE * hd
            for part in range(AUG_PARTS):
                sel[part, HEAD_PAIR * p + hd, base + part] = 1.0
                ones[0, base + AUG_PARTS + part] = 1.0
    return jnp.asarray(sel, BF16), jnp.asarray(ones, F32)


def _kv_kernel(x_ref, mod_ref, wk_ref, wv_ref, wf_ref, bf_ref, sel_ref, ones_ref,
               k_ref, kaug_ref, vt_ref, cum_ref, carry_ref):
    ts = x_ref.shape[1]
    n_heads = cum_ref.shape[2]
    n_pairs = kaug_ref.shape[1]

    @pl.when(pl.program_id(1) == 0)
    def _():
        carry_ref[...] = jnp.zeros_like(carry_ref)

    h = _modulate(x_ref[0], mod_ref).astype(BF16)
    k_ref[0] = jnp.dot(h, wk_ref[...], preferred_element_type=F32).astype(BF16)
    vt = jnp.dot(h, wv_ref[...], preferred_element_type=F32).T.astype(BF16)
    tk = vt_ref.shape[4]
    for p in range(n_pairs):
        for c in range(ts // tk):
            vt_ref[0, p, c] = vt[LANES * p:LANES * (p + 1), tk * c:tk * (c + 1)]
    fz = jnp.dot(h, wf_ref[...], preferred_element_type=F32) + bf_ref[...]
    log_f = jnp.minimum(fz, 0.0) - jnp.log1p(jnp.exp(-jnp.abs(fz)))
    r_i = lax.broadcasted_iota(jnp.int32, (ts, ts), 0)
    c_i = lax.broadcasted_iota(jnp.int32, (ts, ts), 1)
    tri = jnp.where(c_i <= r_i, 1.0, 0.0).astype(BF16)
    cum = carry_ref[0:1, :]
    for part in _split3(log_f):
        cum = cum + jnp.dot(tri, part, preferred_element_type=F32)
    carry_ref[0:1, :] = cum[ts - 1:ts, :]
    cum = cum * LOG2E
    cum_ref[0] = cum[:, :n_heads]
    aug = ones_ref[...]
    for i, part in enumerate(_split3(cum)):
        aug = aug + jnp.dot(part, sel_ref[i], preferred_element_type=F32)
    aug = aug.astype(BF16)
    for p in range(n_pairs):
        kaug_ref[0, p] = aug[:, LANES * p:LANES * (p + 1)]


def _shared_kv(x, mod, w_k, w_v, w_f, b_f):
    bsz, seq, d = x.shape
    ts = ATTN_TILE
    tk = ATTN_KEY_TILE
    n_pairs = d // LANES
    sel, ones = _aug_constants(d)
    return pl.pallas_call(
        _kv_kernel,
        grid=(bsz, seq // ts),
        in_specs=[pl.BlockSpec((1, ts, d), lambda b, j: (b, j, 0)),
                  pl.BlockSpec((1, 2, d), lambda b, j: (b, 0, 0)),
                  pl.BlockSpec((d, d), lambda b, j: (0, 0)),
                  pl.BlockSpec((d, d), lambda b, j: (0, 0)),
                  pl.BlockSpec((d, LANES), lambda b, j: (0, 0)),
                  pl.BlockSpec((1, LANES), lambda b, j: (0, 0)),
                  pl.BlockSpec((AUG_PARTS, LANES, d), lambda b, j: (0, 0, 0)),
                  pl.BlockSpec((1, d), lambda b, j: (0, 0))],
        out_specs=[pl.BlockSpec((1, ts, d), lambda b, j: (b, j, 0)),
                   pl.BlockSpec((1, n_pairs, ts, LANES), lambda b, j: (b, 0, j, 0)),
                   pl.BlockSpec((1, n_pairs, ts // tk, LANES, tk), lambda b, j: (b, 0, j, 0, 0)),
                   pl.BlockSpec((1, ts, N_HEADS), lambda b, j: (b, j, 0))],
        out_shape=[jax.ShapeDtypeStruct((bsz, seq, d), BF16),
                   jax.ShapeDtypeStruct((bsz, n_pairs, seq, LANES), BF16),
                   jax.ShapeDtypeStruct((bsz, n_pairs, seq // tk, LANES, tk), BF16),
                   jax.ShapeDtypeStruct((bsz, seq, N_HEADS), F32)],
        scratch_shapes=[pltpu.VMEM((8, LANES), F32)],
        compiler_params=_params("arbitrary", "arbitrary"),
        name="shared_kv",
    )(x, mod, w_k, w_v, w_f, b_f, sel, ones)


def _q_proj_kernel(x_ref, mod_ref, wq_ref, q_ref, *, scale):
    h = _modulate(x_ref[0], mod_ref).astype(BF16)
    q_ref[0] = (jnp.dot(h, wq_ref[...], preferred_element_type=F32) * scale).astype(BF16)


def _q_proj(x, mod, w_q, scale):
    bsz, seq, d = x.shape
    ts = SEQ_TILE
    return pl.pallas_call(
        functools.partial(_q_proj_kernel, scale=scale),
        grid=(bsz, seq // ts),
        in_specs=[pl.BlockSpec((1, ts, d), lambda b, j: (b, j, 0)),
                  pl.BlockSpec((1, 3, d), lambda b, j: (b, 0, 0)),
                  pl.BlockSpec((d, d), lambda b, j: (0, 0))],
        out_specs=pl.BlockSpec((1, ts, d), lambda b, j: (b, j, 0)),
        out_shape=jax.ShapeDtypeStruct((bsz, seq, d), BF16),
        compiler_params=_params("arbitrary", "arbitrary"),
        name="q_proj",
    )(x, mod, w_q)


def _attn_kernel(q_ref, k_ref, kaug_ref, vt_ref, cq_ref, o_ref, acc_ref, m_ref, l_ref, s_ref):
    tq = q_ref.shape[1]
    tk = vt_ref.shape[4]
    per_q = tq // tk
    head_dim = LANES // HEAD_PAIR
    i = pl.program_id(2)
    q_t = q_ref[0].astype(F32).T
    row = lax.broadcasted_iota(jnp.int32, (LANES, tq), 0)
    rhs = []
    for g in range(PAIR_GROUP):
        q_pair = q_t[LANES * g:LANES * (g + 1), :]
        for hd in range(HEAD_PAIR):
            own = (row >= head_dim * hd) & (row < head_dim * (hd + 1))
            parts = _split3(cq_ref[0, g, hd:hd + 1, :])
            base = AUG_STRIDE * hd
            aug = jnp.where((row >= base) & (row < base + AUG_PARTS), -1.0, 0.0)
            for n, part in enumerate(parts):
                aug = jnp.where(row == base + AUG_PARTS + n, part.astype(F32), aug)
            rhs.append(jnp.concatenate([jnp.where(own, q_pair, 0.0).astype(BF16), aug.astype(BF16)],
                                       axis=0))
    ones_rows = jnp.ones((16, tk), BF16)
    acc_ref[...] = jnp.zeros_like(acc_ref)
    m_ref[...] = jnp.full_like(m_ref, NEG_INF)
    l_ref[...] = jnp.zeros_like(l_ref)

    def keys_of(j, g):
        start = pl.multiple_of(j * tk, tk)
        return jnp.concatenate([k_ref[0, pl.ds(start, tk), LANES * g:LANES * (g + 1)],
                                kaug_ref[0, g, pl.ds(start, tk), :]], axis=1)

    s_ref[...] = jnp.dot(keys_of(0, 0), rhs[0], preferred_element_type=F32)

    def chunk(j, diagonal, offset=0, has_next=True):
        keys = [keys_of(j, g) for g in range(PAIR_GROUP)]

        def masked(s_t):
            if diagonal:
                k_i = lax.broadcasted_iota(jnp.int32, (tk, tq), 0) + offset
                q_i = lax.broadcasted_iota(jnp.int32, (tk, tq), 1)
                s_t = jnp.where(k_i <= q_i, s_t, NEG_INF)
            return s_t

        n_heads = PAIR_GROUP * HEAD_PAIR
        s_next = s_ref[...]
        for n in range(n_heads):
            g, hd = divmod(n, HEAD_PAIR)
            s_t = masked(s_next)
            if n + 1 < n_heads:
                s_next = jnp.dot(keys[(n + 1) // HEAD_PAIR], rhs[n + 1], preferred_element_type=F32)
            elif has_next:
                s_ref[...] = jnp.dot(keys_of(j + 1, 0), rhs[0], preferred_element_type=F32)
            m_old = m_ref[n:n + 1, :]
            m_new = jnp.maximum(m_old, jnp.max(s_t, axis=0, keepdims=True))
            a = jnp.exp2(m_old - m_new)
            p_t = jnp.exp2(s_t - m_new).astype(BF16)
            m_ref[n:n + 1, :] = m_new
            rows = slice(head_dim * n, head_dim * (n + 1))
            lhs = jnp.concatenate([vt_ref[0, g, j, head_dim * hd:head_dim * (hd + 1), :], ones_rows], axis=0)
            pv = jnp.dot(lhs, p_t, preferred_element_type=F32)
            l_ref[n:n + 1, :] = a * l_ref[n:n + 1, :] + pv[head_dim:head_dim + 1, :]
            acc_ref[rows, :] = acc_ref[rows, :] * a + pv[:head_dim, :]

    def body(j, carry):
        chunk(j, False)
        return carry
    lax.fori_loop(0, i * per_q, body, 0)
    for sub in range(per_q):
        chunk(i * per_q + sub, True, sub * tk, has_next=sub + 1 < per_q)
    inv = jnp.concatenate([jnp.broadcast_to(1.0 / l_ref[n:n + 1, :], (head_dim, tq))
                           for n in range(PAIR_GROUP * HEAD_PAIR)], axis=0)
    o_ref[0] = (acc_ref[...] * inv).T.astype(o_ref.dtype)


def _attention(q, k, kaug, v_t, cq):
    bsz, seq, d = q.shape
    width = PAIR_GROUP * LANES
    tq = ATTN_TILE
    return pl.pallas_call(
        _attn_kernel,
        grid=(bsz, d // width, seq // tq),
        in_specs=[pl.BlockSpec((1, tq, width), lambda b, p, i: (b, i, p)),
                  pl.BlockSpec((1, seq, width), lambda b, p, i: (b, 0, p)),
                  pl.BlockSpec((1, PAIR_GROUP, seq, LANES), lambda b, p, i: (b, p, 0, 0)),
                  pl.BlockSpec((1, PAIR_GROUP) + v_t.shape[2:], lambda b, p, i: (b, p, 0, 0, 0)),
                  pl.BlockSpec((1, PAIR_GROUP, HEAD_PAIR, tq), lambda b, p, i: (b, p, 0, i))],
        out_specs=pl.BlockSpec((1, tq, width), lambda b, p, i: (b, i, p)),
        out_shape=jax.ShapeDtypeStruct((bsz, seq, d), BF16),
        scratch_shapes=[pltpu.VMEM((width, tq), F32),
                        pltpu.VMEM((8, tq), F32),
                        pltpu.VMEM((8, tq), F32),
                        pltpu.VMEM((v_t.shape[4], tq), F32)],
        compiler_params=_params("arbitrary", "arbitrary", "arbitrary"),
        name="fox_attention",
    )(q, k, kaug, v_t, cq)


def _out_proj_kernel(o_ref, x_ref, mod_ref, wo_ref, g_ref, b_ref, out_ref, *, alpha):
    y = jnp.dot(o_ref[0], wo_ref[...], preferred_element_type=F32)
    r = alpha * x_ref[0] + (1.0 + mod_ref[0, 2:3, :]) * y
    out_ref[0] = _layer_norm(r, g_ref[...], b_ref[...])


def _out_proj(o, x, mod, w_o, ln_g, ln_b, alpha):
    bsz, seq, d = x.shape
    ts = SEQ_TILE
    return pl.pallas_call(
        functools.partial(_out_proj_kernel, alpha=alpha),
        grid=(bsz, seq // ts),
        in_specs=[pl.BlockSpec((1, ts, d), lambda b, j: (b, j, 0)),
                  pl.BlockSpec((1, ts, d), lambda b, j: (b, j, 0)),
                  pl.BlockSpec((1, 3, d), lambda b, j: (b, 0, 0)),
                  pl.BlockSpec((d, d), lambda b, j: (0, 0)),
                  pl.BlockSpec((1, d), lambda b, j: (0, 0)),
                  pl.BlockSpec((1, d), lambda b, j: (0, 0))],
        out_specs=pl.BlockSpec((1, ts, d), lambda b, j: (b, j, 0)),
        out_shape=jax.ShapeDtypeStruct(x.shape, F32),
        compiler_params=_params("arbitrary", "arbitrary"),
        name="attn_out_proj",
    )(o, x, mod, w_o, ln_g, ln_b)


def kernel(x, c, conv_w_in, conv_w, conv_w_out, kv_ada_w, kv_ada_b, w_kvf, b_f, attn_w_q, attn_w_o,
           ada_w, ada_b, ln_g, ln_b, router_w, router_b, exp_w_gu, exp_b_gu, exp_w_down, exp_b_down):
    bsz, seq, d = x.shape
    depth = ada_w.shape[0]
    n_conv = conv_w_in.shape[0]
    n_exp = router_w.shape[-1]
    alpha = (2.0 * depth) ** 0.25
    head_dim = d // N_HEADS
    assert head_dim * HEAD_PAIR == LANES and seq % SEQ_TILE == 0 and seq % ATTN_TILE == 0

    c_pad = jnp.pad(c, ((0, 8 - bsz), (0, 0)))
    mods = _ada_params(c_pad, ada_w.reshape(depth * 2, d, 3 * d), ada_b.reshape(depth * 2, 1, 3 * d))
    mods = mods[:, :bsz, :].reshape(depth, 2, bsz, 3, d)
    kv_mod = _ada_params(c_pad, kv_ada_w[None], kv_ada_b[None, None])[0, :bsz].reshape(bsz, 2, d)

    w_r = jnp.pad(router_w, ((0, 0), (0, 0), (0, LANES - n_exp)))
    b_r = jnp.pad(router_b, ((0, 0), (0, LANES - n_exp)), constant_values=-1e30)[:, None, :]
    b_gu = exp_b_gu[:, :, None, :]
    b_dn = exp_b_down[:, :, None, :]
    k = kaug = v_t = cq = None
    for l in range(depth):
        g0, b0 = ln_g[l, 0][None], ln_b[l, 0][None]
        if l < n_conv:
            x = _conv_layer(x, mods[l, 0], conv_w_in[l].astype(BF16), conv_w[l],
                            conv_w_out[l].astype(BF16), g0, b0, alpha)
        else:
            j = l - n_conv
            q = _q_proj(x, mods[l, 0], attn_w_q[j].astype(BF16), head_dim ** -0.5 * LOG2E)
            o = _attention(q, k, kaug, v_t, cq)
            x = _out_proj(o, x, mods[l, 0], attn_w_o[j].astype(BF16), g0, b0, alpha)
        x = _moe_layer(l, x.reshape(bsz * seq, d), mods[l, 1], w_r[l], b_r[l], exp_w_gu, b_gu,
                       exp_w_down, b_dn,
                       ln_g[l, 1][None], ln_b[l, 1][None], alpha, seq).reshape(bsz, seq, d)
        if l == n_conv - 1:
            w_f = jnp.pad(w_kvf[:, 2 * d:], ((0, 0), (0, LANES - N_HEADS))).astype(BF16)
            bias_f = jnp.pad(b_f, (0, LANES - N_HEADS))[None]
            k, kaug, v_t, cum = _shared_kv(x, kv_mod, w_kvf[:, :d].astype(BF16),
                                           w_kvf[:, d:2 * d].astype(BF16), w_f, bias_f)
            cq = cum.reshape(bsz, seq, N_HEADS // HEAD_PAIR, HEAD_PAIR).transpose(0, 2, 3, 1)
    return x
```

```python
import functools

import numpy as np
import jax
import jax.numpy as jnp
from jax import lax
from jax.experimental import pallas as pl
from jax.experimental.pallas import tpu as pltpu

N_HEADS = 16
TOP_K = 4
SWIGLU_LIMIT = 7.0
SWIGLU_ALPHA = 1.702
LN_EPS = 1e-5
LANES = 128
HEAD_PAIR = 2
PAIR_GROUP = 2
LOG2E = 1.4426950408889634
SEQ_TILE = 512
MOE_TILE = 512
ATTN_TILE = 512
ATTN_KEY_TILE = 512
ISSUE_GROUP = 64
VMEM_LIMIT = 56 * 1024 * 1024

F32 = jnp.float32
BF16 = jnp.bfloat16
NEG_INF = float("-inf")


def _params(*sem):
    return pltpu.CompilerParams(dimension_semantics=sem, vmem_limit_bytes=VMEM_LIMIT)


def _layer_norm(r, g, b):
    mu = jnp.mean(r, axis=-1, keepdims=True)
    d = r - mu
    var = jnp.mean(d * d, axis=-1, keepdims=True)
    return d * lax.rsqrt(var + LN_EPS) * g + b


def _modulate(x, mod_ref):
    return x * (1.0 + mod_ref[0, 1:2, :]) + mod_ref[0, 0:1, :]


def _ada_kernel(c_ref, w_ref, b_ref, o_ref):
    c = c_ref[...]
    cond = c * jax.nn.sigmoid(c)
    o_ref[0] = jnp.dot(cond, w_ref[0], precision=lax.Precision.HIGHEST,
                       preferred_element_type=F32) + b_ref[0]


def _ada_params(c_pad, w, b):
    g, d, n = w.shape
    tn = 1024 if n % 1024 == 0 else n
    return pl.pallas_call(
        _ada_kernel,
        grid=(g, n // tn),
        in_specs=[pl.BlockSpec((8, d), lambda i, j: (0, 0)),
                  pl.BlockSpec((1, d, tn), lambda i, j: (i, 0, j)),
                  pl.BlockSpec((1, 1, tn), lambda i, j: (i, 0, j))],
        out_specs=pl.BlockSpec((1, 8, tn), lambda i, j: (i, 0, j)),
        out_shape=jax.ShapeDtypeStruct((g, 8, n), F32),
        compiler_params=_params("arbitrary", "arbitrary"),
        name="ada_params",
    )(c_pad, w, b)


def _conv_layer_kernel(x_ref, mod_ref, win_ref, wc_ref, wout_ref, g_ref, b_ref, o_ref,
                       carry_ref, a_ref, *, alpha, col_chunk):
    ts, d = x_ref.shape[1], x_ref.shape[2]

    @pl.when(pl.program_id(1) == 0)
    def _():
        carry_ref[...] = jnp.zeros_like(carry_ref)

    x = x_ref[0]
    h = _modulate(x, mod_ref).astype(BF16)
    row = lax.broadcasted_iota(jnp.int32, (ts, col_chunk), 0)
    for c in range(0, d, col_chunk):
        gate_c = jnp.dot(h, win_ref[:, c:c + col_chunk], preferred_element_type=F32)
        u = jnp.dot(h, win_ref[:, 2 * d + c:2 * d + c + col_chunk], preferred_element_type=F32)
        z = gate_c * u
        z_m1 = carry_ref[1:2, c:c + col_chunk]
        z_m2 = carry_ref[0:1, c:c + col_chunk]
        z1 = jnp.where(row == 0, z_m1, pltpu.roll(z, 1, 0))
        z2 = jnp.where(row == 0, z_m2, jnp.where(row == 1, z_m1, pltpu.roll(z, 2, 0)))
        carry_ref[0:2, c:c + col_chunk] = z[ts - 2:ts, :]
        conv = (wc_ref[0:1, c:c + col_chunk] * z2 + wc_ref[1:2, c:c + col_chunk] * z1
                + wc_ref[2:3, c:c + col_chunk] * z)
        gate_b = jnp.dot(h, win_ref[:, d + c:d + c + col_chunk], preferred_element_type=F32)
        a_ref[:, c:c + col_chunk] = (gate_b * conv).astype(BF16)
    y = jnp.dot(a_ref[...], wout_ref[...], preferred_element_type=F32)
    r = alpha * x + (1.0 + mod_ref[0, 2:3, :]) * y
    o_ref[0] = _layer_norm(r, g_ref[...], b_ref[...])


def _conv_layer(x, mod, w_in, w_conv, w_out, ln_g, ln_b, alpha):
    bsz, seq, d = x.shape
    ts = SEQ_TILE
    kern = functools.partial(_conv_layer_kernel, alpha=alpha, col_chunk=256)
    return pl.pallas_call(
        kern,
        grid=(bsz, seq // ts),
        in_specs=[pl.BlockSpec((1, ts, d), lambda b, j: (b, j, 0)),
                  pl.BlockSpec((1, 3, d), lambda b, j: (b, 0, 0)),
                  pl.BlockSpec((d, 3 * d), lambda b, j: (0, 0)),
                  pl.BlockSpec((3, d), lambda b, j: (0, 0)),
                  pl.BlockSpec((d, d), lambda b, j: (0, 0)),
                  pl.BlockSpec((1, d), lambda b, j: (0, 0)),
                  pl.BlockSpec((1, d), lambda b, j: (0, 0))],
        out_specs=pl.BlockSpec((1, ts, d), lambda b, j: (b, j, 0)),
        out_shape=jax.ShapeDtypeStruct(x.shape, F32),
        scratch_shapes=[pltpu.VMEM((8, d), F32), pltpu.VMEM((ts, d), BF16)],
        compiler_params=_params("arbitrary", "arbitrary"),
        name="conv_layer",
    )(x, mod, w_in, w_conv, w_out, ln_g, ln_b)


def _pack_bf16_pairs(h):
    half = h.shape[1] // 2
    lo = lax.bitcast_convert_type(h[:, :half].astype(BF16).astype(F32), jnp.int32)
    hi = lax.bitcast_convert_type(h[:, half:].astype(BF16).astype(F32), jnp.int32)
    return lax.shift_right_logical(lo, 16) | hi


def _unpack_bf16_pairs(w):
    lo = lax.bitcast_convert_type(w << 16, F32).astype(BF16)
    hi = lax.bitcast_convert_type(w & jnp.int32(-65536), F32).astype(BF16)
    return lo, hi


def _router_kernel(x_ref, mod_ref, wr_ref, br_ref, hpk_ref, route_ref, cnt_ref, carry_ref):
    tr = x_ref.shape[0]

    @pl.when(pl.program_id(0) == 0)
    def _():
        carry_ref[...] = jnp.zeros_like(carry_ref)

    h = _modulate(x_ref[...], mod_ref)
    hpk_ref[...] = _pack_bf16_pairs(h)
    logits = jnp.dot(h, wr_ref[...], precision=lax.Precision.HIGHEST,
                     preferred_element_type=F32) + br_ref[...]
    lane = lax.broadcasted_iota(jnp.int32, (tr, LANES), 1)
    lane_f = lane.astype(F32)
    work = logits
    vals, idxs, sels = [], [], []
    for _ in range(TOP_K):
        m = jnp.max(work, axis=-1, keepdims=True)
        idx = jnp.min(jnp.where(work == m, lane_f, float(LANES)), axis=-1, keepdims=True)
        sel = lane_f == idx
        vals.append(m)
        idxs.append(idx.astype(jnp.int32))
        sels.append(sel)
        work = jnp.where(sel, NEG_INF, work)
    exps = [jnp.exp(v - vals[0]) for v in vals]
    denom = exps[0] + exps[1] + exps[2] + exps[3]
    chosen = sels[0] | sels[1] | sels[2] | sels[3]
    onehot = jnp.where(chosen, 1.0, 0.0).astype(BF16)
    r_i = lax.broadcasted_iota(jnp.int32, (tr, tr), 0)
    c_i = lax.broadcasted_iota(jnp.int32, (tr, tr), 1)
    tri = jnp.where(c_i < r_i, 1.0, 0.0).astype(BF16)
    before = jnp.dot(tri, onehot, preferred_element_type=F32) + carry_ref[0:1, :]
    out = jnp.zeros((tr, LANES), jnp.int32)
    for k in range(TOP_K):
        rank = jnp.sum(jnp.where(sels[k], before, 0.0), axis=-1, keepdims=True).astype(jnp.int32)
        wgt = lax.bitcast_convert_type(exps[k] / denom, jnp.int32)
        out = jnp.where(lane == k, idxs[k], out)
        out = jnp.where(lane == TOP_K + k, rank, out)
        out = jnp.where(lane == 2 * TOP_K + k, wgt, out)
    route_ref[...] = out
    carry_ref[0:1, :] = carry_ref[0:1, :] + jnp.sum(onehot.astype(F32), axis=0, keepdims=True)
    cnt_ref[...] = carry_ref[...].astype(jnp.int32)


def _router(x2, mod, w_r, b_r, seq):
    t, d = x2.shape
    tr = SEQ_TILE
    per_b = seq // tr
    return pl.pallas_call(
        _router_kernel,
        grid=(t // tr,),
        in_specs=[pl.BlockSpec((tr, d), lambda i: (i, 0)),
                  pl.BlockSpec((1, 3, d), lambda i: (i // per_b, 0, 0)),
                  pl.BlockSpec((d, LANES), lambda i: (0, 0)),
                  pl.BlockSpec((1, LANES), lambda i: (0, 0))],
        out_specs=[pl.BlockSpec((tr, d // 2), lambda i: (i, 0)),
                   pl.BlockSpec((tr, LANES), lambda i: (i, 0)),
                   pl.BlockSpec((8, LANES), lambda i: (0, 0))],
        out_shape=[jax.ShapeDtypeStruct((t, d // 2), jnp.int32),
                   jax.ShapeDtypeStruct((t, LANES), jnp.int32),
                   jax.ShapeDtypeStruct((8, LANES), jnp.int32)],
        scratch_shapes=[pltpu.VMEM((8, LANES), F32)],
        compiler_params=_params("arbitrary"),
        name="moe_router",
    )(x2, mod, w_r, b_r)


def _row_copy(src_hbm, dst_vmem, sem, src_row, dst_row):
    return pltpu.make_async_copy(src_hbm.at[pl.ds(src_row, 1)], dst_vmem.at[pl.ds(dst_row, 1)], sem)


def _moe_gemm_kernel(te_ref, nused_ref, tok_ref, tok_next_ref, hpk_hbm, wgu_ref, bgu_ref, wd_ref, bd_ref,
                     y_ref, xbuf, wgu_bf, wd_bf, sem):
    tm = xbuf.shape[1]
    half = xbuf.shape[2]
    f = wd_bf.shape[0]
    i = pl.program_id(0)
    n_used = nused_ref[0]
    slot = i % 2

    def wait_rows(buf):
        pltpu.make_async_copy(hpk_hbm.at[pl.ds(0, tm)], xbuf.at[buf], sem.at[buf]).wait()

    @pl.when(i == 0)
    def _():
        def issue(r, carry):
            _row_copy(hpk_hbm, xbuf.at[0], sem.at[0], tok_ref[0, 0, r], r).start()
            return carry
        lax.fori_loop(0, tm, issue, 0, unroll=8)

    @pl.when((i == 0) | (te_ref[i] != te_ref[jnp.maximum(i - 1, 0)]))
    def _():
        wgu_bf[...] = wgu_ref[0, 0].astype(BF16)
        wd_bf[...] = wd_ref[0, 0].astype(BF16)

    @pl.when(i < n_used)
    def _():
        wait_rows(slot)
        for r in range(tm):
            _row_copy(hpk_hbm, xbuf.at[1 - slot], sem.at[1 - slot], tok_next_ref[0, 0, r], r).start(
                priority=r % 2)

        lo, hi = _unpack_bf16_pairs(xbuf[slot])
        gu = (jnp.dot(lo, wgu_bf[:half, :], preferred_element_type=F32)
              + jnp.dot(hi, wgu_bf[half:, :], preferred_element_type=F32) + bgu_ref[0, 0])
        g = jnp.minimum(gu[:, :f], SWIGLU_LIMIT)
        u = jnp.clip(gu[:, f:], -SWIGLU_LIMIT, SWIGLU_LIMIT)
        a = g * jax.nn.sigmoid(SWIGLU_ALPHA * g) * (u + 1.0)
        y_ref[...] = jnp.dot(a.astype(BF16), wd_bf[...], preferred_element_type=F32) + bd_ref[0, 0]

    @pl.when(i == n_used - 1)
    def _():
        wait_rows(1 - slot)

    @pl.when(i >= n_used)
    def _():
        y_ref[...] = jnp.zeros_like(y_ref)


def _moe_gemm(layer, tile_expert, n_used, slot_tok, hpk, w_gu, b_gu, w_down, b_down):
    n_tiles = tile_expert.shape[0]
    tm = MOE_TILE
    _, _, d, f2 = w_gu.shape
    f = f2 // 2
    last = lambda i, nu: jnp.minimum(i, nu[0] - 1)
    grid_spec = pltpu.PrefetchScalarGridSpec(
        num_scalar_prefetch=2,
        grid=(n_tiles,),
        in_specs=[pl.BlockSpec((1, 1, tm), lambda i, te, nu: (last(i, nu), 0, 0),
                               memory_space=pltpu.SMEM),
                  pl.BlockSpec((1, 1, tm), lambda i, te, nu: (last(i + 1, nu), 0, 0),
                               memory_space=pltpu.SMEM),
                  pl.BlockSpec(memory_space=pl.ANY),
                  pl.BlockSpec((1, 1, d, f2), lambda i, te, nu: (layer, te[i], 0, 0)),
                  pl.BlockSpec((1, 1, 1, f2), lambda i, te, nu: (layer, te[i], 0, 0)),
                  pl.BlockSpec((1, 1, f, d), lambda i, te, nu: (layer, te[i], 0, 0)),
                  pl.BlockSpec((1, 1, 1, d), lambda i, te, nu: (layer, te[i], 0, 0))],
        out_specs=pl.BlockSpec((tm, d), lambda i, te, nu: (i, 0)),
        scratch_shapes=[pltpu.VMEM((2, tm, d // 2), jnp.int32),
                        pltpu.VMEM((d, f2), BF16),
                        pltpu.VMEM((f, d), BF16),
                        pltpu.SemaphoreType.DMA((2,))],
    )
    return pl.pallas_call(
        _moe_gemm_kernel,
        grid_spec=grid_spec,
        out_shape=jax.ShapeDtypeStruct((n_tiles * tm, d), F32),
        compiler_params=_params("arbitrary"),
        name="moe_experts",
    )(tile_expert, n_used, slot_tok, slot_tok, hpk, w_gu, b_gu, w_down, b_down)


def _combine_kernel(dest_ref, y_hbm, x_ref, mod_ref, route_ref, g_ref, b_ref, o_ref, ybuf, sem, *, alpha):
    tr = x_ref.shape[0]

    for r in range(tr):
        for k in range(TOP_K):
            _row_copy(y_hbm, ybuf.at[k], sem, dest_ref[0, 0, r * TOP_K + k], r).start(priority=k % 2)

    for k in range(TOP_K):
        pltpu.make_async_copy(y_hbm.at[pl.ds(0, tr)], ybuf.at[k], sem).wait()

    wts = lax.bitcast_convert_type(route_ref[...], F32)
    sub = wts[:, 2 * TOP_K:2 * TOP_K + 1] * ybuf[0]
    for k in range(1, TOP_K):
        sub = sub + wts[:, 2 * TOP_K + k:2 * TOP_K + k + 1] * ybuf[k]
    x = x_ref[...]
    r = alpha * x + (1.0 + mod_ref[0, 2:3, :]) * sub
    o_ref[...] = _layer_norm(r, g_ref[...], b_ref[...])


def _combine(dest, y, x2, mod, route, ln_g, ln_b, alpha, seq):
    t, d = x2.shape
    tr = SEQ_TILE
    per_b = seq // tr
    kern = functools.partial(_combine_kernel, alpha=alpha)
    return pl.pallas_call(
        kern,
        grid=(t // tr,),
        in_specs=[pl.BlockSpec((1, 1, tr * TOP_K), lambda i: (i, 0, 0), memory_space=pltpu.SMEM),
                  pl.BlockSpec(memory_space=pl.ANY),
                  pl.BlockSpec((tr, d), lambda i: (i, 0)),
                  pl.BlockSpec((1, 3, d), lambda i: (i // per_b, 0, 0)),
                  pl.BlockSpec((tr, LANES), lambda i: (i, 0)),
                  pl.BlockSpec((1, d), lambda i: (0, 0)),
                  pl.BlockSpec((1, d), lambda i: (0, 0))],
        out_specs=pl.BlockSpec((tr, d), lambda i: (i, 0)),
        out_shape=jax.ShapeDtypeStruct((t, d), F32),
        scratch_shapes=[pltpu.VMEM((TOP_K, tr, d), F32), pltpu.SemaphoreType.DMA(())],
        compiler_params=_params("arbitrary"),
        name="moe_combine",
    )(dest, y, x2, mod, route, ln_g, ln_b)


def _moe_layer(layer, x2, mod, w_r, b_r, w_gu, b_gu, w_down, b_down, ln_g, ln_b, alpha, seq):
    t, d = x2.shape
    n_exp = w_gu.shape[1]
    tm = MOE_TILE
    hpk, route, cnt = _router(x2, mod, w_r, b_r, seq)
    top_idx = route[:, 0:TOP_K]
    rank = route[:, TOP_K:2 * TOP_K]
    counts = cnt[0, :n_exp]
    padded = (counts + tm - 1) // tm * tm
    pend = jnp.cumsum(padded)
    pstart = pend - padded
    experts = jnp.arange(n_exp, dtype=jnp.int32)
    dest = rank + jnp.sum(jnp.where(top_idx[:, :, None] == experts, pstart, 0), axis=-1)
    n_tiles = t * TOP_K // tm + n_exp
    n_used = (pend[-1] // tm).astype(jnp.int32).reshape(1)
    tile_start = jnp.minimum(jnp.arange(n_tiles, dtype=jnp.int32), n_used[0] - 1) * tm
    tile_expert = jnp.sum(pend[None, :] <= tile_start[:, None], axis=1).astype(jnp.int32)
    pad_ids = jnp.arange(n_exp * tm, dtype=jnp.int32)
    pad_keys = jnp.where(pad_ids % tm < (padded - counts)[pad_ids // tm], pad_ids // tm, n_exp)
    keys = jnp.concatenate([top_idx.reshape(-1), pad_keys])
    toks = jnp.concatenate([jnp.arange(t * TOP_K, dtype=jnp.int32) // TOP_K,
                            jnp.zeros((n_exp * tm,), jnp.int32)])
    _, slot_tok = lax.sort((keys, toks), num_keys=1, is_stable=True)
    y = _moe_gemm(layer, tile_expert, n_used, slot_tok.reshape(n_tiles, 1, tm), hpk,
                  w_gu, b_gu, w_down, b_down)
    dest3 = dest.reshape(t // SEQ_TILE, 1, SEQ_TILE * TOP_K)
    return _combine(dest3, y, x2, mod, route, ln_g, ln_b, alpha, seq)


def _split3(v):
    p0 = v.astype(BF16)
    r1 = v - p0.astype(F32)
    p1 = r1.astype(BF16)
    p2 = (r1 - p1.astype(F32)).astype(BF16)
    return p0, p1, p2


AUG_STRIDE = 8
AUG_PARTS = 3


def _aug_constants(d):
    n_pairs = d // LANES
    sel = np.zeros((AUG_PARTS, LANES, d), np.float32)
    ones = np.zeros((1, d), np.float32)
    for p in range(n_pairs):
        for hd in range(HEAD_PAIR):
            base = LANES * p + AUG_STRIDE * hd
            for part in range(AUG_PARTS):
                sel[part, HEAD_PAIR * p + hd, base + part] = 1.0
                ones[0, base + AUG_PARTS + part] = 1.0
    return jnp.asarray(sel, BF16), jnp.asarray(ones, F32)


def _kv_kernel(x_ref, mod_ref, wk_ref, wv_ref, wf_ref, bf_ref, sel_ref, ones_ref,
               k_ref, kaug_ref, vt_ref, cum_ref, carry_ref):
    ts = x_ref.shape[1]
    n_heads = cum_ref.shape[2]
    n_pairs = kaug_ref.shape[1]

    @pl.when(pl.program_id(1) == 0)
    def _():
        carry_ref[...] = jnp.zeros_like(carry_ref)

    h = _modulate(x_ref[0], mod_ref).astype(BF16)
    k_ref[0] = jnp.dot(h, wk_ref[...], preferred_element_type=F32).astype(BF16)
    vt = jnp.dot(h, wv_ref[...], preferred_element_type=F32).T.astype(BF16)
    tk = vt_ref.shape[4]
    for p in range(n_pairs):
        for c in range(ts // tk):
            vt_ref[0, p, c] = vt[LANES * p:LANES * (p + 1), tk * c:tk * (c + 1)]
    fz = jnp.dot(h, wf_ref[...], preferred_element_type=F32) + bf_ref[...]
    log_f = jnp.minimum(fz, 0.0) - jnp.log1p(jnp.exp(-jnp.abs(fz)))
    r_i = lax.broadcasted_iota(jnp.int32, (ts, ts), 0)
    c_i = lax.broadcasted_iota(jnp.int32, (ts, ts), 1)
    tri = jnp.where(c_i <= r_i, 1.0, 0.0).astype(BF16)
    cum = carry_ref[0:1, :]
    for part in _split3(log_f):
        cum = cum + jnp.dot(tri, part, preferred_element_type=F32)
    carry_ref[0:1, :] = cum[ts - 1:ts, :]
    cum = cum * LOG2E
    cum_ref[0] = cum[:, :n_heads]
    aug = ones_ref[...]
    for i, part in enumerate(_split3(cum)):
        aug = aug + jnp.dot(part, sel_ref[i], preferred_element_type=F32)
    aug = aug.astype(BF16)
    for p in range(n_pairs):
        kaug_ref[0, p] = aug[:, LANES * p:LANES * (p + 1)]


def _shared_kv(x, mod, w_k, w_v, w_f, b_f):
    bsz, seq, d = x.shape
    ts = ATTN_TILE
    tk = ATTN_KEY_TILE
    n_pairs = d // LANES
    sel, ones = _aug_constants(d)
    return pl.pallas_call(
        _kv_kernel,
        grid=(bsz, seq // ts),
        in_specs=[pl.BlockSpec((1, ts, d), lambda b, j: (b, j, 0)),
                  pl.BlockSpec((1, 2, d), lambda b, j: (b, 0, 0)),
                  pl.BlockSpec((d, d), lambda b, j: (0, 0)),
                  pl.BlockSpec((d, d), lambda b, j: (0, 0)),
                  pl.BlockSpec((d, LANES), lambda b, j: (0, 0)),
                  pl.BlockSpec((1, LANES), lambda b, j: (0, 0)),
                  pl.BlockSpec((AUG_PARTS, LANES, d), lambda b, j: (0, 0, 0)),
                  pl.BlockSpec((1, d), lambda b, j: (0, 0))],
        out_specs=[pl.BlockSpec((1, ts, d), lambda b, j: (b, j, 0)),
                   pl.BlockSpec((1, n_pairs, ts, LANES), lambda b, j: (b, 0, j, 0)),
                   pl.BlockSpec((1, n_pairs, ts // tk, LANES, tk), lambda b, j: (b, 0, j, 0, 0)),
                   pl.BlockSpec((1, ts, N_HEADS), lambda b, j: (b, j, 0))],
        out_shape=[jax.ShapeDtypeStruct((bsz, seq, d), BF16),
                   jax.ShapeDtypeStruct((bsz, n_pairs, seq, LANES), BF16),
                   jax.ShapeDtypeStruct((bsz, n_pairs, seq // tk, LANES, tk), BF16),
                   jax.ShapeDtypeStruct((bsz, seq, N_HEADS), F32)],
        scratch_shapes=[pltpu.VMEM((8, LANES), F32)],
        compiler_params=_params("arbitrary", "arbitrary"),
        name="shared_kv",
    )(x, mod, w_k, w_v, w_f, b_f, sel, ones)


def _q_proj_kernel(x_ref, mod_ref, wq_ref, q_ref, *, scale):
    h = _modulate(x_ref[0], mod_ref).astype(BF16)
    q_ref[0] = (jnp.dot(h, wq_ref[...], preferred_element_type=F32) * scale).astype(BF16)


def _q_proj(x, mod, w_q, scale):
    bsz, seq, d = x.shape
    ts = SEQ_TILE
    return pl.pallas_call(
        functools.partial(_q_proj_kernel, scale=scale),
        grid=(bsz, seq // ts),
        in_specs=[pl.BlockSpec((1, ts, d), lambda b, j: (b, j, 0)),
                  pl.BlockSpec((1, 3, d), lambda b, j: (b, 0, 0)),
                  pl.BlockSpec((d, d), lambda b, j: (0, 0))],
        out_specs=pl.BlockSpec((1, ts, d), lambda b, j: (b, j, 0)),
        out_shape=jax.ShapeDtypeStruct((bsz, seq, d), BF16),
        compiler_params=_params("arbitrary", "arbitrary"),
        name="q_proj",
    )(x, mod, w_q)


def _attn_kernel(q_ref, k_ref, kaug_ref, vt_ref, cq_ref, o_ref, acc_ref, m_ref, l_ref, s_ref):
    tq = q_ref.shape[1]
    tk = vt_ref.shape[4]
    per_q = tq // tk
    head_dim = LANES // HEAD_PAIR
    i = pl.program_id(2)
    q_t = q_ref[0].astype(F32).T
    row = lax.broadcasted_iota(jnp.int32, (LANES, tq), 0)
    rhs = []
    for g in range(PAIR_GROUP):
        q_pair = q_t[LANES * g:LANES * (g + 1), :]
        for hd in range(HEAD_PAIR):
            own = (row >= head_dim * hd) & (row < head_dim * (hd + 1))
            parts = _split3(cq_ref[0, g, hd:hd + 1, :])
            base = AUG_STRIDE * hd
            aug = jnp.where((row >= base) & (row < base + AUG_PARTS), -1.0, 0.0)
            for n, part in enumerate(parts):
                aug = jnp.where(row == base + AUG_PARTS + n, part.astype(F32), aug)
            rhs.append(jnp.concatenate([jnp.where(own, q_pair, 0.0).astype(BF16), aug.astype(BF16)],
                                       axis=0))
    ones_rows = jnp.ones((16, tk), BF16)
    acc_ref[...] = jnp.zeros_like(acc_ref)
    m_ref[...] = jnp.full_like(m_ref, NEG_INF)
    l_ref[...] = jnp.zeros_like(l_ref)

    def keys_of(j, g):
        start = pl.multiple_of(j * tk, tk)
        return jnp.concatenate([k_ref[0, pl.ds(start, tk), LANES * g:LANES * (g + 1)],
                                kaug_ref[0, g, pl.ds(start, tk), :]], axis=1)

    s_ref[...] = jnp.dot(keys_of(0, 0), rhs[0], preferred_element_type=F32)

    def chunk(j, diagonal, offset=0, has_next=True):
        keys = [keys_of(j, g) for g in range(PAIR_GROUP)]

        def masked(s_t):
            if diagonal:
                k_i = lax.broadcasted_iota(jnp.int32, (tk, tq), 0) + offset
                q_i = lax.broadcasted_iota(jnp.int32, (tk, tq), 1)
                s_t = jnp.where(k_i <= q_i, s_t, NEG_INF)
            return s_t

        n_heads = PAIR_GROUP * HEAD_PAIR
        s_next = s_ref[...]
        for n in range(n_heads):
            g, hd = divmod(n, HEAD_PAIR)
            s_t = masked(s_next)
            if n + 1 < n_heads:
                s_next = jnp.dot(keys[(n + 1) // HEAD_PAIR], rhs[n + 1], preferred_element_type=F32)
            elif has_next:
                s_ref[...] = jnp.dot(keys_of(j + 1, 0), rhs[0], preferred_element_type=F32)
            m_old = m_ref[n:n + 1, :]
            m_new = jnp.maximum(m_old, jnp.max(s_t, axis=0, keepdims=True))
            a = jnp.exp2(m_old - m_new)
            p_t = jnp.exp2(s_t - m_new).astype(BF16)
            m_ref[n:n + 1, :] = m_new
            rows = slice(head_dim * n, head_dim * (n + 1))
            lhs = jnp.concatenate([vt_ref[0, g, j, head_dim * hd:head_dim * (hd + 1), :], ones_rows], axis=0)
            pv = jnp.dot(lhs, p_t, preferred_element_type=F32)
            l_ref[n:n + 1, :] = a * l_ref[n:n + 1, :] + pv[head_dim:head_dim + 1, :]
            acc_ref[rows, :] = acc_ref[rows, :] * a + pv[:head_dim, :]

    def body(j, carry):
        chunk(j, False)
        return carry
    lax.fori_loop(0, i * per_q, body, 0)
    for sub in range(per_q):
        chunk(i * per_q + sub, True, sub * tk, has_next=sub + 1 < per_q)
    inv = jnp.concatenate([jnp.broadcast_to(1.0 / l_ref[n:n + 1, :], (head_dim, tq))
                           for n in range(PAIR_GROUP * HEAD_PAIR)], axis=0)
    o_ref[0] = (acc_ref[...] * inv).T.astype(o_ref.dtype)


def _attention(q, k, kaug, v_t, cq):
    bsz, seq, d = q.shape
    width = PAIR_GROUP * LANES
    tq = ATTN_TILE
    return pl.pallas_call(
        _attn_kernel,
        grid=(bsz, d // width, seq // tq),
        in_specs=[pl.BlockSpec((1, tq, width), lambda b, p, i: (b, i, p)),
                  pl.BlockSpec((1, seq, width), lambda b, p, i: (b, 0, p)),
                  pl.BlockSpec((1, PAIR_GROUP, seq, LANES), lambda b, p, i: (b, p, 0, 0)),
                  pl.BlockSpec((1, PAIR_GROUP) + v_t.shape[2:], lambda b, p, i: (b, p, 0, 0, 0)),
                  pl.BlockSpec((1, PAIR_GROUP, HEAD_PAIR, tq), lambda b, p, i: (b, p, 0, i))],
        out_specs=pl.BlockSpec((1, tq, width), lambda b, p, i: (b, i, p)),
        out_shape=jax.ShapeDtypeStruct((bsz, seq, d), BF16),
        scratch_shapes=[pltpu.VMEM((width, tq), F32),
                        pltpu.VMEM((8, tq), F32),
                        pltpu.VMEM((8, tq), F32),
                        pltpu.VMEM((v_t.shape[4], tq), F32)],
        compiler_params=_params("arbitrary", "arbitrary", "arbitrary"),
        name="fox_attention",
    )(q, k, kaug, v_t, cq)


def _out_proj_kernel(o_ref, x_ref, mod_ref, wo_ref, g_ref, b_ref, out_ref, *, alpha):
    y = jnp.dot(o_ref[0], wo_ref[...], preferred_element_type=F32)
    r = alpha * x_ref[0] + (1.0 + mod_ref[0, 2:3, :]) * y
    out_ref[0] = _layer_norm(r, g_ref[...], b_ref[...])


def _out_proj(o, x, mod, w_o, ln_g, ln_b, alpha):
    bsz, seq, d = x.shape
    ts = SEQ_TILE
    return pl.pallas_call(
        functools.partial(_out_proj_kernel, alpha=alpha),
        grid=(bsz, seq // ts),
        in_specs=[pl.BlockSpec((1, ts, d), lambda b, j: (b, j, 0)),
                  pl.BlockSpec((1, ts, d), lambda b, j: (b, j, 0)),
                  pl.BlockSpec((1, 3, d), lambda b, j: (b, 0, 0)),
                  pl.BlockSpec((d, d), lambda b, j: (0, 0)),
                  pl.BlockSpec((1, d), lambda b, j: (0, 0)),
                  pl.BlockSpec((1, d), lambda b, j: (0, 0))],
        out_specs=pl.BlockSpec((1, ts, d), lambda b, j: (b, j, 0)),
        out_shape=jax.ShapeDtypeStruct(x.shape, F32),
        compiler_params=_params("arbitrary", "arbitrary"),
        name="attn_out_proj",
    )(o, x, mod, w_o, ln_g, ln_b)


def kernel(x, c, conv_w_in, conv_w, conv_w_out, kv_ada_w, kv_ada_b, w_kvf, b_f, attn_w_q, attn_w_o,
           ada_w, ada_b, ln_g, ln_b, router_w, router_b, exp_w_gu, exp_b_gu, exp_w_down, exp_b_down):
    bsz, seq, d = x.shape
    depth = ada_w.shape[0]
    n_conv = conv_w_in.shape[0]
    n_exp = router_w.shape[-1]
    alpha = (2.0 * depth) ** 0.25
    head_dim = d // N_HEADS
    assert head_dim * HEAD_PAIR == LANES and seq % SEQ_TILE == 0 and seq % ATTN_TILE == 0

    c_pad = jnp.pad(c, ((0, 8 - bsz), (0, 0)))
    mods = _ada_params(c_pad, ada_w.reshape(depth * 2, d, 3 * d), ada_b.reshape(depth * 2, 1, 3 * d))
    mods = mods[:, :bsz, :].reshape(depth, 2, bsz, 3, d)
    kv_mod = _ada_params(c_pad, kv_ada_w[None], kv_ada_b[None, None])[0, :bsz].reshape(bsz, 2, d)

    w_r = jnp.pad(router_w, ((0, 0), (0, 0), (0, LANES - n_exp)))
    b_r = jnp.pad(router_b, ((0, 0), (0, LANES - n_exp)), constant_values=-1e30)[:, None, :]
    b_gu = exp_b_gu[:, :, None, :]
    b_dn = exp_b_down[:, :, None, :]
    k = kaug = v_t = cq = None
    for l in range(depth):
        g0, b0 = ln_g[l, 0][None], ln_b[l, 0][None]
        if l < n_conv:
            x = _conv_layer(x, mods[l, 0], conv_w_in[l].astype(BF16), conv_w[l],
                            conv_w_out[l].astype(BF16), g0, b0, alpha)
        else:
            j = l - n_conv
            q = _q_proj(x, mods[l, 0], attn_w_q[j].astype(BF16), head_dim ** -0.5 * LOG2E)
            o = _attention(q, k, kaug, v_t, cq)
            x = _out_proj(o, x, mods[l, 0], attn_w_o[j].astype(BF16), g0, b0, alpha)
        x = _moe_layer(l, x.reshape(bsz * seq, d), mods[l, 1], w_r[l], b_r[l], exp_w_gu, b_gu,
                       exp_w_down, b_dn,
                       ln_g[l, 1][None], ln_b[l, 1][None], alpha, seq).reshape(bsz, seq, d)
        if l == n_conv - 1:
            w_f = jnp.pad(w_kvf[:, 2 * d:], ((0, 0), (0, LANES - N_HEADS))).astype(BF16)
            bias_f = jnp.pad(b_f, (0, LANES - N_HEADS))[None]
            k, kaug, v_t, cum = _shared_kv(x, kv_mod, w_kvf[:, :d].astype(BF16),
                                           w_kvf[:, d:2 * d].astype(BF16), w_f, bias_f)
            cq = cum.reshape(bsz, seq, N_HEADS // HEAD_PAIR, HEAD_PAIR).transpose(0, 2, 3, 1)
    return x
```

```python
import functools

import numpy as np
import jax
import jax.numpy as jnp
from jax import lax
from jax.experimental import pallas as pl
from jax.experimental.pallas import tpu as pltpu

N_HEADS = 16
TOP_K = 4
SWIGLU_LIMIT = 7.0
SWIGLU_ALPHA = 1.702
LN_EPS = 1e-5
LANES = 128
HEAD_PAIR = 2
PAIR_GROUP = 2
LOG2E = 1.4426950408889634
SEQ_TILE = 512
MOE_TILE = 512
ATTN_TILE = 512
ATTN_KEY_TILE = 512
RUN_ALIGN = 8
RUN_COPY = 128
VMEM_LIMIT = 56 * 1024 * 1024

F32 = jnp.float32
BF16 = jnp.bfloat16
NEG_INF = float("-inf")


def _params(*sem):
    return pltpu.CompilerParams(dimension_semantics=sem, vmem_limit_bytes=VMEM_LIMIT)


def _layer_norm(r, g, b):
    mu = jnp.mean(r, axis=-1, keepdims=True)
    d = r - mu
    var = jnp.mean(d * d, axis=-1, keepdims=True)
    return d * lax.rsqrt(var + LN_EPS) * g + b


def _modulate(x, mod_ref):
    return x * (1.0 + mod_ref[0, 1:2, :]) + mod_ref[0, 0:1, :]


def _ada_kernel(c_ref, w_ref, b_ref, o_ref):
    c = c_ref[...]
    cond = c * jax.nn.sigmoid(c)
    o_ref[0] = jnp.dot(cond, w_ref[0], precision=lax.Precision.HIGHEST,
                       preferred_element_type=F32) + b_ref[0]


def _ada_params(c_pad, w, b):
    g, d, n = w.shape
    tn = 1024 if n % 1024 == 0 else n
    return pl.pallas_call(
        _ada_kernel,
        grid=(g, n // tn),
        in_specs=[pl.BlockSpec((8, d), lambda i, j: (0, 0)),
                  pl.BlockSpec((1, d, tn), lambda i, j: (i, 0, j)),
                  pl.BlockSpec((1, 1, tn), lambda i, j: (i, 0, j))],
        out_specs=pl.BlockSpec((1, 8, tn), lambda i, j: (i, 0, j)),
        out_shape=jax.ShapeDtypeStruct((g, 8, n), F32),
        compiler_params=_params("arbitrary", "arbitrary"),
        name="ada_params",
    )(c_pad, w, b)


def _conv_layer_kernel(x_ref, mod_ref, win_ref, wc_ref, wout_ref, g_ref, b_ref, o_ref,
                       carry_ref, a_ref, *, alpha, col_chunk):
    ts, d = x_ref.shape[1], x_ref.shape[2]

    @pl.when(pl.program_id(1) == 0)
    def _():
        carry_ref[...] = jnp.zeros_like(carry_ref)

    x = x_ref[0]
    h = _modulate(x, mod_ref).astype(BF16)
    row = lax.broadcasted_iota(jnp.int32, (ts, col_chunk), 0)
    for c in range(0, d, col_chunk):
        gate_c = jnp.dot(h, win_ref[:, c:c + col_chunk], preferred_element_type=F32)
        u = jnp.dot(h, win_ref[:, 2 * d + c:2 * d + c + col_chunk], preferred_element_type=F32)
        z = gate_c * u
        z_m1 = carry_ref[1:2, c:c + col_chunk]
        z_m2 = carry_ref[0:1, c:c + col_chunk]
        z1 = jnp.where(row == 0, z_m1, pltpu.roll(z, 1, 0))
        z2 = jnp.where(row == 0, z_m2, jnp.where(row == 1, z_m1, pltpu.roll(z, 2, 0)))
        carry_ref[0:2, c:c + col_chunk] = z[ts - 2:ts, :]
        conv = (wc_ref[0:1, c:c + col_chunk] * z2 + wc_ref[1:2, c:c + col_chunk] * z1
                + wc_ref[2:3, c:c + col_chunk] * z)
        gate_b = jnp.dot(h, win_ref[:, d + c:d + c + col_chunk], preferred_element_type=F32)
        a_ref[:, c:c + col_chunk] = (gate_b * conv).astype(BF16)
    y = jnp.dot(a_ref[...], wout_ref[...], preferred_element_type=F32)
    r = alpha * x + (1.0 + mod_ref[0, 2:3, :]) * y
    o_ref[0] = _layer_norm(r, g_ref[...], b_ref[...])


def _conv_layer(x, mod, w_in, w_conv, w_out, ln_g, ln_b, alpha):
    bsz, seq, d = x.shape
    ts = SEQ_TILE
    kern = functools.partial(_conv_layer_kernel, alpha=alpha, col_chunk=256)
    return pl.pallas_call(
        kern,
        grid=(bsz, seq // ts),
        in_specs=[pl.BlockSpec((1, ts, d), lambda b, j: (b, j, 0)),
                  pl.BlockSpec((1, 3, d), lambda b, j: (b, 0, 0)),
                  pl.BlockSpec((d, 3 * d), lambda b, j: (0, 0)),
                  pl.BlockSpec((3, d), lambda b, j: (0, 0)),
                  pl.BlockSpec((d, d), lambda b, j: (0, 0)),
                  pl.BlockSpec((1, d), lambda b, j: (0, 0)),
                  pl.BlockSpec((1, d), lambda b, j: (0, 0))],
        out_specs=pl.BlockSpec((1, ts, d), lambda b, j: (b, j, 0)),
        out_shape=jax.ShapeDtypeStruct(x.shape, F32),
        scratch_shapes=[pltpu.VMEM((8, d), F32), pltpu.VMEM((ts, d), BF16)],
        compiler_params=_params("arbitrary", "arbitrary"),
        name="conv_layer",
    )(x, mod, w_in, w_conv, w_out, ln_g, ln_b)


def _pack_bf16_pairs(h):
    half = h.shape[1] // 2
    lo = lax.bitcast_convert_type(h[:, :half].astype(BF16).astype(F32), jnp.int32)
    hi = lax.bitcast_convert_type(h[:, half:].astype(BF16).astype(F32), jnp.int32)
    return lax.shift_right_logical(lo, 16) | hi


def _unpack_bf16_pairs(w):
    lo = lax.bitcast_convert_type(w << 16, F32).astype(BF16)
    hi = lax.bitcast_convert_type(w & jnp.int32(-65536), F32).astype(BF16)
    return lo, hi


def _router_kernel(x_ref, mod_ref, wr_ref, br_ref, h_ref, route_ref, cnt_ref):
    tr = x_ref.shape[0]
    h = _modulate(x_ref[...], mod_ref)
    h_ref[...] = h.astype(BF16)
    logits = jnp.dot(h, wr_ref[...], precision=lax.Precision.HIGHEST,
                     preferred_element_type=F32) + br_ref[...]
    lane = lax.broadcasted_iota(jnp.int32, (tr, LANES), 1)
    lane_f = lane.astype(F32)
    work = logits
    vals, idxs, sels = [], [], []
    for _ in range(TOP_K):
        m = jnp.max(work, axis=-1, keepdims=True)
        idx = jnp.min(jnp.where(work == m, lane_f, float(LANES)), axis=-1, keepdims=True)
        sel = lane_f == idx
        vals.append(m)
        idxs.append(idx.astype(jnp.int32))
        sels.append(sel)
        work = jnp.where(sel, NEG_INF, work)
    exps = [jnp.exp(v - vals[0]) for v in vals]
    denom = exps[0] + exps[1] + exps[2] + exps[3]
    chosen = sels[0] | sels[1] | sels[2] | sels[3]
    onehot = jnp.where(chosen, 1.0, 0.0).astype(BF16)
    r_i = lax.broadcasted_iota(jnp.int32, (tr, tr), 0)
    c_i = lax.broadcasted_iota(jnp.int32, (tr, tr), 1)
    tri = jnp.where(c_i < r_i, 1.0, 0.0).astype(BF16)
    before = jnp.dot(tri, onehot, preferred_element_type=F32)
    out = jnp.zeros((tr, LANES), jnp.int32)
    for k in range(TOP_K):
        rank = jnp.sum(jnp.where(sels[k], before, 0.0), axis=-1, keepdims=True).astype(jnp.int32)
        wgt = lax.bitcast_convert_type(exps[k] / denom, jnp.int32)
        out = jnp.where(lane == k, idxs[k], out)
        out = jnp.where(lane == TOP_K + k, rank, out)
        out = jnp.where(lane == 2 * TOP_K + k, wgt, out)
    route_ref[...] = out
    counts = jnp.sum(onehot.astype(F32), axis=0, keepdims=True).astype(jnp.int32)
    cnt_ref[0] = jnp.broadcast_to(counts, (8, LANES))


def _router(x2, mod, w_r, b_r, seq):
    t, d = x2.shape
    tr = SEQ_TILE
    per_b = seq // tr
    return pl.pallas_call(
        _router_kernel,
        grid=(t // tr,),
        in_specs=[pl.BlockSpec((tr, d), lambda i: (i, 0)),
                  pl.BlockSpec((1, 3, d), lambda i: (i // per_b, 0, 0)),
                  pl.BlockSpec((d, LANES), lambda i: (0, 0)),
                  pl.BlockSpec((1, LANES), lambda i: (0, 0))],
        out_specs=[pl.BlockSpec((tr, d), lambda i: (i, 0)),
                   pl.BlockSpec((tr, LANES), lambda i: (i, 0)),
                   pl.BlockSpec((1, 8, LANES), lambda i: (i, 0, 0))],
        out_shape=[jax.ShapeDtypeStruct((t, d), BF16),
                   jax.ShapeDtypeStruct((t, LANES), jnp.int32),
                   jax.ShapeDtypeStruct((t // tr, 8, LANES), jnp.int32)],
        compiler_params=_params("arbitrary"),
        name="moe_router",
    )(x2, mod, w_r, b_r)


def _dispatch_kernel(meta_ref, offdst_ref, route_ref, h_ref, xs_hbm, dest_ref, xs_buf, sem, *, n_exp):
    tr, d = h_ref.shape
    n_rows = xs_buf.shape[0] - RUN_COPY

    @pl.when(pl.program_id(0) == 0)
    def _():
        xs_buf[n_rows:, :] = jnp.zeros((RUN_COPY, d // 2), jnp.int32)

    route = route_ref[...]
    lane = lax.broadcasted_iota(jnp.int32, (tr, LANES), 1)
    off_row = offdst_ref[0, 0:1, :]
    dst_row = offdst_ref[0, 1:2, :]
    pos = jnp.full((tr, LANES), -1.0, F32)
    dest = jnp.zeros((tr, LANES), jnp.int32)
    for k in range(TOP_K):
        sel = lane == route[:, k:k + 1]
        rank = route[:, TOP_K + k:TOP_K + k + 1].astype(F32)
        pos_k = jnp.sum(jnp.where(sel, off_row, 0.0), axis=-1, keepdims=True) + rank
        dest_k = jnp.sum(jnp.where(sel, dst_row, 0.0), axis=-1, keepdims=True) + rank
        pos = jnp.where(lane == k, pos_k, pos)
        dest = jnp.where(lane == k, dest_k.astype(jnp.int32), dest)
    dest_ref[...] = dest
    pos_t = pos.T
    h = h_ref[...]
    chunk = n_rows // 4
    for c in range(4):
        slot = (lax.broadcasted_iota(jnp.int32, (chunk, tr), 0) + chunk * c).astype(F32)
        hit = slot == pos_t[0:1, :]
        for k in range(1, TOP_K):
            hit = hit | (slot == pos_t[k:k + 1, :])
        onehot = jnp.where(hit, 1.0, 0.0).astype(BF16)
        rows = jnp.dot(onehot, h, preferred_element_type=F32)
        xs_buf[chunk * c:chunk * (c + 1), :] = _pack_bf16_pairs(rows)

    def run_copies(act):
        for e in range(n_exp):
            src = pl.multiple_of(meta_ref[0, 0, e], RUN_ALIGN)
            dst = pl.multiple_of(meta_ref[0, 0, LANES // 4 + e], RUN_ALIGN)
            length = meta_ref[0, 0, LANES // 2 + e]
            for q in range(tr // RUN_COPY):
                @pl.when(length > RUN_COPY * q)
                def _():
                    act(pltpu.make_async_copy(xs_buf.at[pl.ds(src + RUN_COPY * q, RUN_COPY)],
                                              xs_hbm.at[pl.ds(dst + RUN_COPY * q, RUN_COPY)], sem))
    run_copies(lambda cp: cp.start())
    run_copies(lambda cp: cp.wait())


def _dispatch(meta, offdst, route, h, n_slots, n_exp):
    t, d = h.shape
    tr = SEQ_TILE
    n_rows = TOP_K * tr + n_exp * RUN_ALIGN
    return pl.pallas_call(
        functools.partial(_dispatch_kernel, n_exp=n_exp),
        grid=(t // tr,),
        in_specs=[pl.BlockSpec((1, 1, LANES), lambda i: (i, 0, 0), memory_space=pltpu.SMEM),
                  pl.BlockSpec((1, 8, LANES), lambda i: (i, 0, 0)),
                  pl.BlockSpec((tr, LANES), lambda i: (i, 0)),
                  pl.BlockSpec((tr, d), lambda i: (i, 0))],
        out_specs=[pl.BlockSpec(memory_space=pl.ANY),
                   pl.BlockSpec((tr, LANES), lambda i: (i, 0))],
        out_shape=[jax.ShapeDtypeStruct((n_slots, d // 2), jnp.int32),
                   jax.ShapeDtypeStruct((t, LANES), jnp.int32)],
        scratch_shapes=[pltpu.VMEM((n_rows + RUN_COPY, d // 2), jnp.int32), pltpu.SemaphoreType.DMA(())],
        compiler_params=_params("arbitrary"),
        name="moe_dispatch",
    )(meta, offdst, route, h)


def _row_copy(src_hbm, dst_vmem, sem, src_row, dst_row):
    return pltpu.make_async_copy(src_hbm.at[pl.ds(src_row, 1)], dst_vmem.at[pl.ds(dst_row, 1)], sem)


def _moe_gemm_kernel(te_ref, valid_ref, xs_ref, wgu_ref, bgu_ref, wd_ref, bd_ref, y_ref, wgu_bf, wd_bf):
    half = xs_ref.shape[1]
    f = wd_bf.shape[0]
    i = pl.program_id(0)

    @pl.when((i == 0) | (te_ref[i] != te_ref[jnp.maximum(i - 1, 0)]))
    def _():
        wgu_bf[...] = wgu_ref[0, 0].astype(BF16)
        wd_bf[...] = wd_ref[0, 0].astype(BF16)

    @pl.when(valid_ref[i] == 1)
    def _():
        lo, hi = _unpack_bf16_pairs(xs_ref[...])
        gu = (jnp.dot(lo, wgu_bf[:half, :], preferred_element_type=F32)
              + jnp.dot(hi, wgu_bf[half:, :], preferred_element_type=F32) + bgu_ref[0, 0])
        g = jnp.minimum(gu[:, :f], SWIGLU_LIMIT)
        u = jnp.clip(gu[:, f:], -SWIGLU_LIMIT, SWIGLU_LIMIT)
        a = g * jax.nn.sigmoid(SWIGLU_ALPHA * g) * (u + 1.0)
        y_ref[...] = jnp.dot(a.astype(BF16), wd_bf[...], preferred_element_type=F32) + bd_ref[0, 0]

    @pl.when(valid_ref[i] == 0)
    def _():
        y_ref[...] = jnp.zeros_like(y_ref)


def _moe_gemm(layer, tile_expert, tile_valid, xs, w_gu, b_gu, w_down, b_down):
    n_tiles = tile_expert.shape[0]
    tm = MOE_TILE
    _, _, d, f2 = w_gu.shape
    f = f2 // 2
    grid_spec = pltpu.PrefetchScalarGridSpec(
        num_scalar_prefetch=2,
        grid=(n_tiles,),
        in_specs=[pl.BlockSpec((tm, d // 2), lambda i, te, tv: (i, 0)),
                  pl.BlockSpec((1, 1, d, f2), lambda i, te, tv: (layer, te[i], 0, 0)),
                  pl.BlockSpec((1, 1, 1, f2), lambda i, te, tv: (layer, te[i], 0, 0)),
                  pl.BlockSpec((1, 1, f, d), lambda i, te, tv: (layer, te[i], 0, 0)),
                  pl.BlockSpec((1, 1, 1, d), lambda i, te, tv: (layer, te[i], 0, 0))],
        out_specs=pl.BlockSpec((tm, d), lambda i, te, tv: (i, 0)),
        scratch_shapes=[pltpu.VMEM((d, f2), BF16), pltpu.VMEM((f, d), BF16)],
    )
    return pl.pallas_call(
        _moe_gemm_kernel,
        grid_spec=grid_spec,
        out_shape=jax.ShapeDtypeStruct((n_tiles * tm, d), F32),
        compiler_params=_params("arbitrary"),
        name="moe_experts",
    )(tile_expert, tile_valid, xs, w_gu, b_gu, w_down, b_down)


def _combine_kernel(dest_ref, y_hbm, x_ref, mod_ref, route_ref, g_ref, b_ref, o_ref, ybuf, sem, *, alpha):
    tr = x_ref.shape[0]

    for r in range(tr):
        for k in range(TOP_K):
            _row_copy(y_hbm, ybuf.at[k], sem, dest_ref[0, 0, r * TOP_K + k], r).start(priority=k % 2)

    for k in range(TOP_K):
        pltpu.make_async_copy(y_hbm.at[pl.ds(0, tr)], ybuf.at[k], sem).wait()

    wts = lax.bitcast_convert_type(route_ref[...], F32)
    sub = wts[:, 2 * TOP_K:2 * TOP_K + 1] * ybuf[0]
    for k in range(1, TOP_K):
        sub = sub + wts[:, 2 * TOP_K + k:2 * TOP_K + k + 1] * ybuf[k]
    x = x_ref[...]
    r = alpha * x + (1.0 + mod_ref[0, 2:3, :]) * sub
    o_ref[...] = _layer_norm(r, g_ref[...], b_ref[...])


def _combine(dest, y, x2, mod, route, ln_g, ln_b, alpha, seq):
    t, d = x2.shape
    tr = SEQ_TILE
    per_b = seq // tr
    kern = functools.partial(_combine_kernel, alpha=alpha)
    return pl.pallas_call(
        kern,
        grid=(t // tr,),
        in_specs=[pl.BlockSpec((1, 1, tr * TOP_K), lambda i: (i, 0, 0), memory_space=pltpu.SMEM),
                  pl.BlockSpec(memory_space=pl.ANY),
                  pl.BlockSpec((tr, d), lambda i: (i, 0)),
                  pl.BlockSpec((1, 3, d), lambda i: (i // per_b, 0, 0)),
                  pl.BlockSpec((tr, LANES), lambda i: (i, 0)),
                  pl.BlockSpec((1, d), lambda i: (0, 0)),
                  pl.BlockSpec((1, d), lambda i: (0, 0))],
        out_specs=pl.BlockSpec((tr, d), lambda i: (i, 0)),
        out_shape=jax.ShapeDtypeStruct((t, d), F32),
        scratch_shapes=[pltpu.VMEM((TOP_K, tr, d), F32), pltpu.SemaphoreType.DMA(())],
        compiler_params=_params("arbitrary"),
        name="moe_combine",
    )(dest, y, x2, mod, route, ln_g, ln_b)


def _moe_layer(layer, x2, mod, w_r, b_r, w_gu, b_gu, w_down, b_down, ln_g, ln_b, alpha, seq):
    t, d = x2.shape
    n_exp = w_gu.shape[1]
    tm = MOE_TILE
    n_tok_tiles = t // SEQ_TILE
    assert n_exp <= LANES // 4 and SEQ_TILE % RUN_COPY == 0
    h, route, cnt = _router(x2, mod, w_r, b_r, seq)
    runs = (cnt[:, 0, :n_exp] + RUN_ALIGN - 1) // RUN_ALIGN * RUN_ALIGN
    run_off = jnp.cumsum(runs, axis=1) - runs
    run_base = jnp.cumsum(runs, axis=0) - runs
    rows = jnp.sum(runs, axis=0)
    region = (rows + RUN_COPY + tm - 1) // tm * tm
    region_end = jnp.cumsum(region)
    region_start = region_end - region
    run_dst = region_start[None, :] + run_base
    n_tiles = -(-(t * TOP_K + n_tok_tiles * n_exp * (RUN_ALIGN - 1) + n_exp * (RUN_COPY + tm - 1)) // tm)
    tile_start = jnp.arange(n_tiles, dtype=jnp.int32) * tm
    tile_expert = jnp.minimum(jnp.sum(region_end[None, :] <= tile_start[:, None], axis=1),
                              n_exp - 1).astype(jnp.int32)
    tile_valid = ((tile_start - region_start[tile_expert] < rows[tile_expert])
                  & (tile_start < region_end[-1])).astype(jnp.int32)
    pad = jnp.zeros((n_tok_tiles, LANES // 4 - n_exp), jnp.int32)
    meta = jnp.concatenate([run_off, pad, run_dst, pad, runs, pad,
                            jnp.zeros((n_tok_tiles, LANES // 4), jnp.int32)], axis=1)
    wide = lambda v: jnp.pad(v.astype(F32), ((0, 0), (0, LANES - n_exp)))
    offdst = jnp.concatenate([wide(run_off)[:, None], wide(run_dst)[:, None],
                              jnp.zeros((n_tok_tiles, 6, LANES), F32)], axis=1)
    xs, dest = _dispatch(meta[:, None, :], offdst, route, h, n_tiles * tm, n_exp)
    y = _moe_gemm(layer, tile_expert, tile_valid, xs, w_gu, b_gu, w_down, b_down)
    dest3 = dest[:, :TOP_K].reshape(n_tok_tiles, 1, SEQ_TILE * TOP_K)
    return _combine(dest3, y, x2, mod, route, ln_g, ln_b, alpha, seq)


def _split3(v):
    p0 = v.astype(BF16)
    r1 = v - p0.astype(F32)
    p1 = r1.astype(BF16)
    p2 = (r1 - p1.astype(F32)).astype(BF16)
    return p0, p1, p2


AUG_STRIDE = 8
AUG_PARTS = 3


def _aug_constants(d):
    n_pairs = d // LANES
    sel = np.zeros((AUG_PARTS, LANES, d), np.float32)
    ones = np.zeros((1, d), np.float32)
    for p in range(n_pairs):
        for hd in range(HEAD_PAIR):
            base = LANES * p + AUG_STRIDE * hd
            for part in range(AUG_PARTS):
                sel[part, HEAD_PAIR * p + hd, base + part] = 1.0
                ones[0, base + AUG_PARTS + part] = 1.0
    return jnp.asarray(sel, BF16), jnp.asarray(ones, F32)


def _kv_kernel(x_ref, mod_ref, wk_ref, wv_ref, wf_ref, bf_ref, sel_ref, ones_ref,
               k_ref, kaug_ref, vt_ref, cum_ref, carry_ref):
    ts = x_ref.shape[1]
    n_heads = cum_ref.shape[2]
    n_pairs = kaug_ref.shape[1]

    @pl.when(pl.program_id(1) == 0)
    def _():
        carry_ref[...] = jnp.zeros_like(carry_ref)

    h = _modulate(x_ref[0], mod_ref).astype(BF16)
    k_ref[0] = jnp.dot(h, wk_ref[...], preferred_element_type=F32).astype(BF16)
    vt = jnp.dot(h, wv_ref[...], preferred_element_type=F32).T.astype(BF16)
    tk = vt_ref.shape[4]
    for p in range(n_pairs):
        for c in range(ts // tk):
            vt_ref[0, p, c] = vt[LANES * p:LANES * (p + 1), tk * c:tk * (c + 1)]
    fz = jnp.dot(h, wf_ref[...], preferred_element_type=F32) + bf_ref[...]
    log_f = jnp.minimum(fz, 0.0) - jnp.log1p(jnp.exp(-jnp.abs(fz)))
    r_i = lax.broadcasted_iota(jnp.int32, (ts, ts), 0)
    c_i = lax.broadcasted_iota(jnp.int32, (ts, ts), 1)
    tri = jnp.where(c_i <= r_i, 1.0, 0.0).astype(BF16)
    cum = carry_ref[0:1, :]
    for part in _split3(log_f):
        cum = cum + jnp.dot(tri, part, preferred_element_type=F32)
    carry_ref[0:1, :] = cum[ts - 1:ts, :]
    cum = cum * LOG2E
    cum_ref[0] = cum[:, :n_heads]
    aug = ones_ref[...]
    for i, part in enumerate(_split3(cum)):
        aug = aug + jnp.dot(part, sel_ref[i], preferred_element_type=F32)
    aug = aug.astype(BF16)
    for p in range(n_pairs):
        kaug_ref[0, p] = aug[:, LANES * p:LANES * (p + 1)]


def _shared_kv(x, mod, w_k, w_v, w_f, b_f):
    bsz, seq, d = x.shape
    ts = ATTN_TILE
    tk = ATTN_KEY_TILE
    n_pairs = d // LANES
    sel, ones = _aug_constants(d)
    return pl.pallas_call(
        _kv_kernel,
        grid=(bsz, seq // ts),
        in_specs=[pl.BlockSpec((1, ts, d), lambda b, j: (b, j, 0)),
                  pl.BlockSpec((1, 2, d), lambda b, j: (b, 0, 0)),
                  pl.BlockSpec((d, d), lambda b, j: (0, 0)),
                  pl.BlockSpec((d, d), lambda b, j: (0, 0)),
                  pl.BlockSpec((d, LANES), lambda b, j: (0, 0)),
                  pl.BlockSpec((1, LANES), lambda b, j: (0, 0)),
                  pl.BlockSpec((AUG_PARTS, LANES, d), lambda b, j: (0, 0, 0)),
                  pl.BlockSpec((1, d), lambda b, j: (0, 0))],
        out_specs=[pl.BlockSpec((1, ts, d), lambda b, j: (b, j, 0)),
                   pl.BlockSpec((1, n_pairs, ts, LANES), lambda b, j: (b, 0, j, 0)),
                   pl.BlockSpec((1, n_pairs, ts // tk, LANES, tk), lambda b, j: (b, 0, j, 0, 0)),
                   pl.BlockSpec((1, ts, N_HEADS), lambda b, j: (b, j, 0))],
        out_shape=[jax.ShapeDtypeStruct((bsz, seq, d), BF16),
                   jax.ShapeDtypeStruct((bsz, n_pairs, seq, LANES), BF16),
                   jax.ShapeDtypeStruct((bsz, n_pairs, seq // tk, LANES, tk), BF16),
                   jax.ShapeDtypeStruct((bsz, seq, N_HEADS), F32)],
        scratch_shapes=[pltpu.VMEM((8, LANES), F32)],
        compiler_params=_params("arbitrary", "arbitrary"),
        name="shared_kv",
    )(x, mod, w_k, w_v, w_f, b_f, sel, ones)


def _q_proj_kernel(x_ref, mod_ref, wq_ref, q_ref, *, scale):
    h = _modulate(x_ref[0], mod_ref).astype(BF16)
    q_ref[0] = (jnp.dot(h, wq_ref[...], preferred_element_type=F32) * scale).astype(BF16)


def _q_proj(x, mod, w_q, scale):
    bsz, seq, d = x.shape
    ts = SEQ_TILE
    return pl.pallas_call(
        functools.partial(_q_proj_kernel, scale=scale),
        grid=(bsz, seq // ts),
        in_specs=[pl.BlockSpec((1, ts, d), lambda b, j: (b, j, 0)),
                  pl.BlockSpec((1, 3, d), lambda b, j: (b, 0, 0)),
                  pl.BlockSpec((d, d), lambda b, j: (0, 0))],
        out_specs=pl.BlockSpec((1, ts, d), lambda b, j: (b, j, 0)),
        out_shape=jax.ShapeDtypeStruct((bsz, seq, d), BF16),
        compiler_params=_params("arbitrary", "arbitrary"),
        name="q_proj",
    )(x, mod, w_q)


def _attn_kernel(q_ref, k_ref, kaug_ref, vt_ref, cq_ref, o_ref, acc_ref, m_ref, l_ref, s_ref):
    tq = q_ref.shape[1]
    tk = vt_ref.shape[4]
    per_q = tq // tk
    head_dim = LANES // HEAD_PAIR
    i = pl.program_id(2)
    q_t = q_ref[0].astype(F32).T
    row = lax.broadcasted_iota(jnp.int32, (LANES, tq), 0)
    rhs = []
    for g in range(PAIR_GROUP):
        q_pair = q_t[LANES * g:LANES * (g + 1), :]
        for hd in range(HEAD_PAIR):
            own = (row >= head_dim * hd) & (row < head_dim * (hd + 1))
            parts = _split3(cq_ref[0, g, hd:hd + 1, :])
            base = AUG_STRIDE * hd
            aug = jnp.where((row >= base) & (row < base + AUG_PARTS), -1.0, 0.0)
            for n, part in enumerate(parts):
                aug = jnp.where(row == base + AUG_PARTS + n, part.astype(F32), aug)
            rhs.append(jnp.concatenate([jnp.where(own, q_pair, 0.0).astype(BF16), aug.astype(BF16)],
                                       axis=0))
    ones_rows = jnp.ones((16, tk), BF16)
    acc_ref[...] = jnp.zeros_like(acc_ref)
    m_ref[...] = jnp.full_like(m_ref, NEG_INF)
    l_ref[...] = jnp.zeros_like(l_ref)

    def keys_of(j, g):
        start = pl.multiple_of(j * tk, tk)
        return jnp.concatenate([k_ref[0, pl.ds(start, tk), LANES * g:LANES * (g + 1)],
                                kaug_ref[0, g, pl.ds(start, tk), :]], axis=1)

    s_ref[...] = jnp.dot(keys_of(0, 0), rhs[0], preferred_element_type=F32)

    def chunk(j, diagonal, offset=0, has_next=True):
        keys = [keys_of(j, g) for g in range(PAIR_GROUP)]

        def masked(s_t):
            if diagonal:
                k_i = lax.broadcasted_iota(jnp.int32, (tk, tq), 0) + offset
                q_i = lax.broadcasted_iota(jnp.int32, (tk, tq), 1)
                s_t = jnp.where(k_i <= q_i, s_t, NEG_INF)
            return s_t

        n_heads = PAIR_GROUP * HEAD_PAIR
        s_next = s_ref[...]
        for n in range(n_heads):
            g, hd = divmod(n, HEAD_PAIR)
            s_t = masked(s_next)
            if n + 1 < n_heads:
                s_next = jnp.dot(keys[(n + 1) // HEAD_PAIR], rhs[n + 1], preferred_element_type=F32)
            elif has_next:
                s_ref[...] = jnp.dot(keys_of(j + 1, 0), rhs[0], preferred_element_type=F32)
            m_old = m_ref[n:n + 1, :]
            m_new = jnp.maximum(m_old, jnp.max(s_t, axis=0, keepdims=True))
            a = jnp.exp2(m_old - m_new)
            p_t = jnp.exp2(s_t - m_new).astype(BF16)
            m_ref[n:n + 1, :] = m_new
            rows = slice(head_dim * n, head_dim * (n + 1))
            lhs = jnp.concatenate([vt_ref[0, g, j, head_dim * hd:head_dim * (hd + 1), :], ones_rows], axis=0)
            pv = jnp.dot(lhs, p_t, preferred_element_type=F32)
            l_ref[n:n + 1, :] = a * l_ref[n:n + 1, :] + pv[head_dim:head_dim + 1, :]
            acc_ref[rows, :] = acc_ref[rows, :] * a + pv[:head_dim, :]

    def body(j, carry):
        chunk(j, False)
        return carry
    lax.fori_loop(0, i * per_q, body, 0)
    for sub in range(per_q):
        chunk(i * per_q + sub, True, sub * tk, has_next=sub + 1 < per_q)
    inv = jnp.concatenate([jnp.broadcast_to(1.0 / l_ref[n:n + 1, :], (head_dim, tq))
                           for n in range(PAIR_GROUP * HEAD_PAIR)], axis=0)
    o_ref[0] = (acc_ref[...] * inv).T.astype(o_ref.dtype)


def _attention(q, k, kaug, v_t, cq):
    bsz, seq, d = q.shape
    width = PAIR_GROUP * LANES
    tq = ATTN_TILE
    return pl.pallas_call(
        _attn_kernel,
        grid=(bsz, d // width, seq // tq),
        in_specs=[pl.BlockSpec((1, tq, width), lambda b, p, i: (b, i, p)),
                  pl.BlockSpec((1, seq, width), lambda b, p, i: (b, 0, p)),
                  pl.BlockSpec((1, PAIR_GROUP, seq, LANES), lambda b, p, i: (b, p, 0, 0)),
                  pl.BlockSpec((1, PAIR_GROUP) + v_t.shape[2:], lambda b, p, i: (b, p, 0, 0, 0)),
                  pl.BlockSpec((1, PAIR_GROUP, HEAD_PAIR, tq), lambda b, p, i: (b, p, 0, i))],
        out_specs=pl.BlockSpec((1, tq, width), lambda b, p, i: (b, i, p)),
        out_shape=jax.ShapeDtypeStruct((bsz, seq, d), BF16),
        scratch_shapes=[pltpu.VMEM((width, tq), F32),
                        pltpu.VMEM((8, tq), F32),
                        pltpu.VMEM((8, tq), F32),
                        pltpu.VMEM((v_t.shape[4], tq), F32)],
        compiler_params=_params("arbitrary", "arbitrary", "arbitrary"),
        name="fox_attention",
    )(q, k, kaug, v_t, cq)


def _out_proj_kernel(o_ref, x_ref, mod_ref, wo_ref, g_ref, b_ref, out_ref, *, alpha):
    y = jnp.dot(o_ref[0], wo_ref[...], preferred_element_type=F32)
    r = alpha * x_ref[0] + (1.0 + mod_ref[0, 2:3, :]) * y
    out_ref[0] = _layer_norm(r, g_ref[...], b_ref[...])


def _out_proj(o, x, mod, w_o, ln_g, ln_b, alpha):
    bsz, seq, d = x.shape
    ts = SEQ_TILE
    return pl.pallas_call(
        functools.partial(_out_proj_kernel, alpha=alpha),
        grid=(bsz, seq // ts),
        in_specs=[pl.BlockSpec((1, ts, d), lambda b, j: (b, j, 0)),
                  pl.BlockSpec((1, ts, d), lambda b, j: (b, j, 0)),
                  pl.BlockSpec((1, 3, d), lambda b, j: (b, 0, 0)),
                  pl.BlockSpec((d, d), lambda b, j: (0, 0)),
                  pl.BlockSpec((1, d), lambda b, j: (0, 0)),
                  pl.BlockSpec((1, d), lambda b, j: (0, 0))],
        out_specs=pl.BlockSpec((1, ts, d), lambda b, j: (b, j, 0)),
        out_shape=jax.ShapeDtypeStruct(x.shape, F32),
        compiler_params=_params("arbitrary", "arbitrary"),
        name="attn_out_proj",
    )(o, x, mod, w_o, ln_g, ln_b)


def kernel(x, c, conv_w_in, conv_w, conv_w_out, kv_ada_w, kv_ada_b, w_kvf, b_f, attn_w_q, attn_w_o,
           ada_w, ada_b, ln_g, ln_b, router_w, router_b, exp_w_gu, exp_b_gu, exp_w_down, exp_b_down):
    bsz, seq, d = x.shape
    depth = ada_w.shape[0]
    n_conv = conv_w_in.shape[0]
    n_exp = router_w.shape[-1]
    alpha = (2.0 * depth) ** 0.25
    head_dim = d // N_HEADS
    assert head_dim * HEAD_PAIR == LANES and seq % SEQ_TILE == 0 and seq % ATTN_TILE == 0

    c_pad = jnp.pad(c, ((0, 8 - bsz), (0, 0)))
    mods = _ada_params(c_pad, ada_w.reshape(depth * 2, d, 3 * d), ada_b.reshape(depth * 2, 1, 3 * d))
    mods = mods[:, :bsz, :].reshape(depth, 2, bsz, 3, d)
    kv_mod = _ada_params(c_pad, kv_ada_w[None], kv_ada_b[None, None])[0, :bsz].reshape(bsz, 2, d)

    w_r = jnp.pad(router_w, ((0, 0), (0, 0), (0, LANES - n_exp)))
    b_r = jnp.pad(router_b, ((0, 0), (0, LANES - n_exp)), constant_values=-1e30)[:, None, :]
    b_gu = exp_b_gu[:, :, None, :]
    b_dn = exp_b_down[:, :, None, :]
    k = kaug = v_t = cq = None
    for l in range(depth):
        g0, b0 = ln_g[l, 0][None], ln_b[l, 0][None]
        if l < n_conv:
            x = _conv_layer(x, mods[l, 0], conv_w_in[l].astype(BF16), conv_w[l],
                            conv_w_out[l].astype(BF16), g0, b0, alpha)
        else:
            j = l - n_conv
            q = _q_proj(x, mods[l, 0], attn_w_q[j].astype(BF16), head_dim ** -0.5 * LOG2E)
            o = _attention(q, k, kaug, v_t, cq)
            x = _out_proj(o, x, mods[l, 0], attn_w_o[j].astype(BF16), g0, b0, alpha)
        x = _moe_layer(l, x.reshape(bsz * seq, d), mods[l, 1], w_r[l], b_r[l], exp_w_gu, b_gu,
                       exp_w_down, b_dn,
                       ln_g[l, 1][None], ln_b[l, 1][None], alpha, seq).reshape(bsz, seq, d)
        if l == n_conv - 1:
            w_f = jnp.pad(w_kvf[:, 2 * d:], ((0, 0), (0, LANES - N_HEADS))).astype(BF16)
            bias_f = jnp.pad(b_f, (0, LANES - N_HEADS))[None]
            k, kaug, v_t, cum = _shared_kv(x, kv_mod, w_kvf[:, :d].astype(BF16),
                                           w_kvf[:, d:2 * d].astype(BF16), w_f, bias_f)
            cq = cum.reshape(bsz, seq, N_HEADS // HEAD_PAIR, HEAD_PAIR).transpose(0, 2, 3, 1)
    return x
```

```python
import functools

import numpy as np
import jax
import jax.numpy as jnp
from jax import lax
from jax.experimental import pallas as pl
from jax.experimental.pallas import tpu as pltpu

N_HEADS = 16
TOP_K = 4
SWIGLU_LIMIT = 7.0
SWIGLU_ALPHA = 1.702
LN_EPS = 1e-5
LANES = 128
HEAD_PAIR = 2
PAIR_GROUP = 2
LOG2E = 1.4426950408889634
SEQ_TILE = 512
MOE_TILE = 512
ATTN_TILE = 512
ATTN_KEY_TILE = 512
RUN_ALIGN = 8
RUN_COPY = 128
VMEM_LIMIT = 56 * 1024 * 1024

F32 = jnp.float32
BF16 = jnp.bfloat16
NEG_INF = float("-inf")


def _params(*sem):
    return pltpu.CompilerParams(dimension_semantics=sem, vmem_limit_bytes=VMEM_LIMIT)


def _layer_norm(r, g, b):
    mu = jnp.mean(r, axis=-1, keepdims=True)
    d = r - mu
    var = jnp.mean(d * d, axis=-1, keepdims=True)
    return d * lax.rsqrt(var + LN_EPS) * g + b


def _modulate(x, mod_ref):
    return x * (1.0 + mod_ref[0, 1:2, :]) + mod_ref[0, 0:1, :]


def _ada_kernel(c_ref, w_ref, b_ref, o_ref):
    c = c_ref[...]
    cond = c * jax.nn.sigmoid(c)
    o_ref[0] = jnp.dot(cond, w_ref[0], precision=lax.Precision.HIGHEST,
                       preferred_element_type=F32) + b_ref[0]


def _ada_params(c_pad, w, b):
    g, d, n = w.shape
    tn = 1024 if n % 1024 == 0 else n
    return pl.pallas_call(
        _ada_kernel,
        grid=(g, n // tn),
        in_specs=[pl.BlockSpec((8, d), lambda i, j: (0, 0)),
                  pl.BlockSpec((1, d, tn), lambda i, j: (i, 0, j)),
                  pl.BlockSpec((1, 1, tn), lambda i, j: (i, 0, j))],
        out_specs=pl.BlockSpec((1, 8, tn), lambda i, j: (i, 0, j)),
        out_shape=jax.ShapeDtypeStruct((g, 8, n), F32),
        compiler_params=_params("arbitrary", "arbitrary"),
        name="ada_params",
    )(c_pad, w, b)


def _conv_layer_kernel(x_ref, mod_ref, win_ref, wc_ref, wout_ref, g_ref, b_ref, o_ref,
                       carry_ref, a_ref, *, alpha, col_chunk):
    ts, d = x_ref.shape[1], x_ref.shape[2]

    @pl.when(pl.program_id(1) == 0)
    def _():
        carry_ref[...] = jnp.zeros_like(carry_ref)

    x = x_ref[0]
    h = _modulate(x, mod_ref).astype(BF16)
    row = lax.broadcasted_iota(jnp.int32, (ts, col_chunk), 0)
    for c in range(0, d, col_chunk):
        gate_c = jnp.dot(h, win_ref[:, c:c + col_chunk], preferred_element_type=F32)
        u = jnp.dot(h, win_ref[:, 2 * d + c:2 * d + c + col_chunk], preferred_element_type=F32)
        z = gate_c * u
        z_m1 = carry_ref[1:2, c:c + col_chunk]
        z_m2 = carry_ref[0:1, c:c + col_chunk]
        z1 = jnp.where(row == 0, z_m1, pltpu.roll(z, 1, 0))
        z2 = jnp.where(row == 0, z_m2, jnp.where(row == 1, z_m1, pltpu.roll(z, 2, 0)))
        carry_ref[0:2, c:c + col_chunk] = z[ts - 2:ts, :]
        conv = (wc_ref[0:1, c:c + col_chunk] * z2 + wc_ref[1:2, c:c + col_chunk] * z1
                + wc_ref[2:3, c:c + col_chunk] * z)
        gate_b = jnp.dot(h, win_ref[:, d + c:d + c + col_chunk], preferred_element_type=F32)
        a_ref[:, c:c + col_chunk] = (gate_b * conv).astype(BF16)
    y = jnp.dot(a_ref[...], wout_ref[...], preferred_element_type=F32)
    r = alpha * x + (1.0 + mod_ref[0, 2:3, :]) * y
    o_ref[0] = _layer_norm(r, g_ref[...], b_ref[...])


def _conv_layer(x, mod, w_in, w_conv, w_out, ln_g, ln_b, alpha):
    bsz, seq, d = x.shape
    ts = SEQ_TILE
    kern = functools.partial(_conv_layer_kernel, alpha=alpha, col_chunk=256)
    return pl.pallas_call(
        kern,
        grid=(bsz, seq // ts),
        in_specs=[pl.BlockSpec((1, ts, d), lambda b, j: (b, j, 0)),
                  pl.BlockSpec((1, 3, d), lambda b, j: (b, 0, 0)),
                  pl.BlockSpec((d, 3 * d), lambda b, j: (0, 0)),
                  pl.BlockSpec((3, d), lambda b, j: (0, 0)),
                  pl.BlockSpec((d, d), lambda b, j: (0, 0)),
                  pl.BlockSpec((1, d), lambda b, j: (0, 0)),
                  pl.BlockSpec((1, d), lambda b, j: (0, 0))],
        out_specs=pl.BlockSpec((1, ts, d), lambda b, j: (b, j, 0)),
        out_shape=jax.ShapeDtypeStruct(x.shape, F32),
        scratch_shapes=[pltpu.VMEM((8, d), F32), pltpu.VMEM((ts, d), BF16)],
        compiler_params=_params("arbitrary", "arbitrary"),
        name="conv_layer",
    )(x, mod, w_in, w_conv, w_out, ln_g, ln_b)


def _pack_bf16_pairs(h):
    half = h.shape[1] // 2
    lo = lax.bitcast_convert_type(h[:, :half].astype(BF16).astype(F32), jnp.int32)
    hi = lax.bitcast_convert_type(h[:, half:].astype(BF16).astype(F32), jnp.int32)
    return lax.shift_right_logical(lo, 16) | hi


def _unpack_bf16_pairs(w):
    lo = lax.bitcast_convert_type(w << 16, F32).astype(BF16)
    hi = lax.bitcast_convert_type(w & jnp.int32(-65536), F32).astype(BF16)
    return lo, hi


def _router_kernel(x_ref, mod_ref, wr_ref, br_ref, h_ref, route_ref, cnt_ref):
    tr = x_ref.shape[0]
    h = _modulate(x_ref[...], mod_ref)
    h_hi = h.astype(BF16)
    h_ref[...] = h_hi
    h_lo = (h - h_hi.astype(F32)).astype(BF16)
    both = jnp.dot(h_hi, wr_ref[...], preferred_element_type=F32)
    logits = (both[:, :LANES] + both[:, LANES:]
              + jnp.dot(h_lo, wr_ref[:, :LANES], preferred_element_type=F32)) + br_ref[...]
    lane = lax.broadcasted_iota(jnp.int32, (tr, LANES), 1)
    lane_f = lane.astype(F32)
    work = logits
    vals, idxs, sels = [], [], []
    for _ in range(TOP_K):
        m = jnp.max(work, axis=-1, keepdims=True)
        idx = jnp.min(jnp.where(work == m, lane_f, float(LANES)), axis=-1, keepdims=True)
        sel = lane_f == idx
        vals.append(m)
        idxs.append(idx.astype(jnp.int32))
        sels.append(sel)
        work = jnp.where(sel, NEG_INF, work)
    exps = [jnp.exp(v - vals[0]) for v in vals]
    denom = exps[0] + exps[1] + exps[2] + exps[3]
    chosen = sels[0] | sels[1] | sels[2] | sels[3]
    onehot = jnp.where(chosen, 1.0, 0.0).astype(BF16)
    r_i = lax.broadcasted_iota(jnp.int32, (tr, tr), 0)
    c_i = lax.broadcasted_iota(jnp.int32, (tr, tr), 1)
    tri = jnp.where(c_i < r_i, 1.0, 0.0).astype(BF16)
    before = jnp.dot(tri, onehot, preferred_element_type=F32)
    out = jnp.zeros((tr, LANES), jnp.int32)
    for k in range(TOP_K):
        rank = jnp.sum(jnp.where(sels[k], before, 0.0), axis=-1, keepdims=True).astype(jnp.int32)
        wgt = lax.bitcast_convert_type(exps[k] / denom, jnp.int32)
        out = jnp.where(lane == k, idxs[k], out)
        out = jnp.where(lane == TOP_K + k, rank, out)
        out = jnp.where(lane == 2 * TOP_K + k, wgt, out)
    route_ref[...] = out
    counts = jnp.sum(onehot.astype(F32), axis=0, keepdims=True).astype(jnp.int32)
    cnt_ref[0] = jnp.broadcast_to(counts, (8, LANES))


def _router(x2, mod, w_r, b_r, seq):
    t, d = x2.shape
    tr = SEQ_TILE
    per_b = seq // tr
    return pl.pallas_call(
        _router_kernel,
        grid=(t // tr,),
        in_specs=[pl.BlockSpec((tr, d), lambda i: (i, 0)),
                  pl.BlockSpec((1, 3, d), lambda i: (i // per_b, 0, 0)),
                  pl.BlockSpec((d, 2 * LANES), lambda i: (0, 0)),
                  pl.BlockSpec((1, LANES), lambda i: (0, 0))],
        out_specs=[pl.BlockSpec((tr, d), lambda i: (i, 0)),
                   pl.BlockSpec((tr, LANES), lambda i: (i, 0)),
                   pl.BlockSpec((1, 8, LANES), lambda i: (i, 0, 0))],
        out_shape=[jax.ShapeDtypeStruct((t, d), BF16),
                   jax.ShapeDtypeStruct((t, LANES), jnp.int32),
                   jax.ShapeDtypeStruct((t // tr, 8, LANES), jnp.int32)],
        compiler_params=_params("arbitrary"),
        name="moe_router",
    )(x2, mod, w_r, b_r)


def _dispatch_kernel(meta_ref, offdst_ref, route_ref, h_ref, xs_hbm, dest_ref, xs_buf, sem, *, n_exp):
    tr, d = h_ref.shape
    n_rows = xs_buf.shape[0] - RUN_COPY

    @pl.when(pl.program_id(0) == 0)
    def _():
        xs_buf[n_rows:, :] = jnp.zeros((RUN_COPY, d // 2), jnp.int32)

    route = route_ref[...]
    lane = lax.broadcasted_iota(jnp.int32, (tr, LANES), 1)
    off_row = offdst_ref[0, 0:1, :]
    dst_row = offdst_ref[0, 1:2, :]
    pos = jnp.full((tr, LANES), -1.0, F32)
    dest = jnp.zeros((tr, LANES), jnp.int32)
    for k in range(TOP_K):
        sel = lane == route[:, k:k + 1]
        rank = route[:, TOP_K + k:TOP_K + k + 1].astype(F32)
        pos_k = jnp.sum(jnp.where(sel, off_row, 0.0), axis=-1, keepdims=True) + rank
        dest_k = jnp.sum(jnp.where(sel, dst_row, 0.0), axis=-1, keepdims=True) + rank
        pos = jnp.where(lane == k, pos_k, pos)
        dest = jnp.where(lane == k, dest_k.astype(jnp.int32), dest)
    dest_ref[...] = dest
    pos_t = pos.T
    h = h_ref[...]
    chunk = n_rows // 4
    for c in range(4):
        slot = (lax.broadcasted_iota(jnp.int32, (chunk, tr), 0) + chunk * c).astype(F32)
        hit = slot == pos_t[0:1, :]
        for k in range(1, TOP_K):
            hit = hit | (slot == pos_t[k:k + 1, :])
        onehot = jnp.where(hit, 1.0, 0.0).astype(BF16)
        rows = jnp.dot(onehot, h, preferred_element_type=F32)
        xs_buf[chunk * c:chunk * (c + 1), :] = _pack_bf16_pairs(rows)

    for e in range(n_exp):
        src = pl.multiple_of(meta_ref[0, 0, e], RUN_ALIGN)
        dst = pl.multiple_of(meta_ref[0, 0, LANES // 4 + e], RUN_ALIGN)
        length = meta_ref[0, 0, LANES // 2 + e]
        for q in range(tr // RUN_COPY):
            @pl.when(length > RUN_COPY * q)
            def _():
                pltpu.make_async_copy(xs_buf.at[pl.ds(src + RUN_COPY * q, RUN_COPY)],
                                      xs_hbm.at[pl.ds(dst + RUN_COPY * q, RUN_COPY)], sem).start()

    def wait_one(n, carry):
        pltpu.make_async_copy(xs_buf.at[pl.ds(0, RUN_COPY)], xs_hbm.at[pl.ds(0, RUN_COPY)], sem).wait()
        return carry
    lax.fori_loop(0, meta_ref[0, 0, 3 * LANES // 4], wait_one, 0)


def _dispatch(meta, offdst, route, h, n_slots, n_exp):
    t, d = h.shape
    tr = SEQ_TILE
    n_rows = TOP_K * tr + n_exp * RUN_ALIGN
    return pl.pallas_call(
        functools.partial(_dispatch_kernel, n_exp=n_exp),
        grid=(t // tr,),
        in_specs=[pl.BlockSpec((1, 1, LANES), lambda i: (i, 0, 0), memory_space=pltpu.SMEM),
                  pl.BlockSpec((1, 8, LANES), lambda i: (i, 0, 0)),
                  pl.BlockSpec((tr, LANES), lambda i: (i, 0)),
                  pl.BlockSpec((tr, d), lambda i: (i, 0))],
        out_specs=[pl.BlockSpec(memory_space=pl.ANY),
                   pl.BlockSpec((tr, LANES), lambda i: (i, 0))],
        out_shape=[jax.ShapeDtypeStruct((n_slots, d // 2), jnp.int32),
                   jax.ShapeDtypeStruct((t, LANES), jnp.int32)],
        scratch_shapes=[pltpu.VMEM((n_rows + RUN_COPY, d // 2), jnp.int32), pltpu.SemaphoreType.DMA(())],
        compiler_params=_params("arbitrary"),
        name="moe_dispatch",
    )(meta, offdst, route, h)


def _row_copy(src_hbm, dst_vmem, sem, src_row, dst_row):
    return pltpu.make_async_copy(src_hbm.at[pl.ds(src_row, 1)], dst_vmem.at[pl.ds(dst_row, 1)], sem)


def _moe_gemm_kernel(te_ref, valid_ref, xs_ref, wgu_ref, bgu_ref, wd_ref, bd_ref, y_ref, wgu_bf, wd_bf):
    half = xs_ref.shape[1]
    f = wd_bf.shape[0]
    i = pl.program_id(0)

    @pl.when((i == 0) | (te_ref[i] != te_ref[jnp.maximum(i - 1, 0)]))
    def _():
        wgu_bf[...] = wgu_ref[0, 0].astype(BF16)
        wd_bf[...] = wd_ref[0, 0].astype(BF16)

    @pl.when(valid_ref[i] == 1)
    def _():
        lo, hi = _unpack_bf16_pairs(xs_ref[...])
        gu = (jnp.dot(lo, wgu_bf[:half, :], preferred_element_type=F32)
              + jnp.dot(hi, wgu_bf[half:, :], preferred_element_type=F32) + bgu_ref[0, 0])
        g = jnp.minimum(gu[:, :f], SWIGLU_LIMIT)
        u = jnp.clip(gu[:, f:], -SWIGLU_LIMIT, SWIGLU_LIMIT)
        a = g * jax.nn.sigmoid(SWIGLU_ALPHA * g) * (u + 1.0)
        y = jnp.dot(a.astype(BF16), wd_bf[...], preferred_element_type=F32) + bd_ref[0, 0]
        y_ref[...] = _pack_bf16_pairs(y)

    @pl.when(valid_ref[i] == 0)
    def _():
        y_ref[...] = jnp.zeros_like(y_ref)


def _moe_gemm(layer, tile_expert, tile_valid, xs, w_gu, b_gu, w_down, b_down):
    n_tiles = tile_expert.shape[0]
    tm = MOE_TILE
    _, _, d, f2 = w_gu.shape
    f = f2 // 2
    grid_spec = pltpu.PrefetchScalarGridSpec(
        num_scalar_prefetch=2,
        grid=(n_tiles,),
        in_specs=[pl.BlockSpec((tm, d // 2), lambda i, te, tv: (i, 0)),
                  pl.BlockSpec((1, 1, d, f2), lambda i, te, tv: (layer, te[i], 0, 0)),
                  pl.BlockSpec((1, 1, 1, f2), lambda i, te, tv: (layer, te[i], 0, 0)),
                  pl.BlockSpec((1, 1, f, d), lambda i, te, tv: (layer, te[i], 0, 0)),
                  pl.BlockSpec((1, 1, 1, d), lambda i, te, tv: (layer, te[i], 0, 0))],
        out_specs=pl.BlockSpec((tm, d // 2), lambda i, te, tv: (i, 0)),
        scratch_shapes=[pltpu.VMEM((d, f2), BF16), pltpu.VMEM((f, d), BF16)],
    )
    return pl.pallas_call(
        _moe_gemm_kernel,
        grid_spec=grid_spec,
        out_shape=jax.ShapeDtypeStruct((n_tiles * tm, d // 2), jnp.int32),
        compiler_params=_params("arbitrary"),
        name="moe_experts",
    )(tile_expert, tile_valid, xs, w_gu, b_gu, w_down, b_down)


def _combine_kernel(dest_ref, y_hbm, x_ref, mod_ref, route_ref, g_ref, b_ref, o_ref, ybuf, sem, *, alpha):
    tr = x_ref.shape[0]

    for r in range(tr):
        for k in range(TOP_K):
            _row_copy(y_hbm, ybuf.at[k], sem, dest_ref[0, 0, r * TOP_K + k], r).start(priority=k % 2)

    for k in range(TOP_K):
        pltpu.make_async_copy(y_hbm.at[pl.ds(0, tr)], ybuf.at[k], sem).wait()

    wts = lax.bitcast_convert_type(route_ref[...], F32)
    sub_lo = sub_hi = None
    for k in range(TOP_K):
        w_k = wts[:, 2 * TOP_K + k:2 * TOP_K + k + 1]
        words = ybuf[k]
        lo = w_k * lax.bitcast_convert_type(words << 16, F32)
        hi = w_k * lax.bitcast_convert_type(words & jnp.int32(-65536), F32)
        sub_lo = lo if sub_lo is None else sub_lo + lo
        sub_hi = hi if sub_hi is None else sub_hi + hi
    sub = jnp.concatenate([sub_lo, sub_hi], axis=1)
    x = x_ref[...]
    r = alpha * x + (1.0 + mod_ref[0, 2:3, :]) * sub
    o_ref[...] = _layer_norm(r, g_ref[...], b_ref[...])


def _combine(dest, y, x2, mod, route, ln_g, ln_b, alpha, seq):
    t, d = x2.shape
    tr = SEQ_TILE
    per_b = seq // tr
    kern = functools.partial(_combine_kernel, alpha=alpha)
    return pl.pallas_call(
        kern,
        grid=(t // tr,),
        in_specs=[pl.BlockSpec((1, 1, tr * TOP_K), lambda i: (i, 0, 0), memory_space=pltpu.SMEM),
                  pl.BlockSpec(memory_space=pl.ANY),
                  pl.BlockSpec((tr, d), lambda i: (i, 0)),
                  pl.BlockSpec((1, 3, d), lambda i: (i // per_b, 0, 0)),
                  pl.BlockSpec((tr, LANES), lambda i: (i, 0)),
                  pl.BlockSpec((1, d), lambda i: (0, 0)),
                  pl.BlockSpec((1, d), lambda i: (0, 0))],
        out_specs=pl.BlockSpec((tr, d), lambda i: (i, 0)),
        out_shape=jax.ShapeDtypeStruct((t, d), F32),
        scratch_shapes=[pltpu.VMEM((TOP_K, tr, d // 2), jnp.int32), pltpu.SemaphoreType.DMA(())],
        compiler_params=_params("arbitrary"),
        name="moe_combine",
    )(dest, y, x2, mod, route, ln_g, ln_b)


def _moe_layer(layer, x2, mod, w_r, b_r, w_gu, b_gu, w_down, b_down, ln_g, ln_b, alpha, seq):
    t, d = x2.shape
    n_exp = w_gu.shape[1]
    tm = MOE_TILE
    n_tok_tiles = t // SEQ_TILE
    assert n_exp <= LANES // 4 and SEQ_TILE % RUN_COPY == 0
    h, route, cnt = _router(x2, mod, w_r, b_r, seq)
    runs = (cnt[:, 0, :n_exp] + RUN_ALIGN - 1) // RUN_ALIGN * RUN_ALIGN
    run_off = jnp.cumsum(runs, axis=1) - runs
    run_base = jnp.cumsum(runs, axis=0) - runs
    rows = jnp.sum(runs, axis=0)
    region = (rows + RUN_COPY + tm - 1) // tm * tm
    region_end = jnp.cumsum(region)
    region_start = region_end - region
    run_dst = region_start[None, :] + run_base
    n_tiles = -(-(t * TOP_K + n_tok_tiles * n_exp * (RUN_ALIGN - 1) + n_exp * (RUN_COPY + tm - 1)) // tm)
    tile_start = jnp.arange(n_tiles, dtype=jnp.int32) * tm
    tile_expert = jnp.minimum(jnp.sum(region_end[None, :] <= tile_start[:, None], axis=1),
                              n_exp - 1).astype(jnp.int32)
    tile_valid = ((tile_start - region_start[tile_expert] < rows[tile_expert])
                  & (tile_start < region_end[-1])).astype(jnp.int32)
    pad = jnp.zeros((n_tok_tiles, LANES // 4 - n_exp), jnp.int32)
    n_copies = jnp.sum((runs + RUN_COPY - 1) // RUN_COPY, axis=1, keepdims=True)
    meta = jnp.concatenate([run_off, pad, run_dst, pad, runs, pad, n_copies,
                            jnp.zeros((n_tok_tiles, LANES // 4 - 1), jnp.int32)], axis=1).astype(jnp.int32)
    wide = lambda v: jnp.pad(v.astype(F32), ((0, 0), (0, LANES - n_exp)))
    offdst = jnp.concatenate([wide(run_off)[:, None], wide(run_dst)[:, None],
                              jnp.zeros((n_tok_tiles, 6, LANES), F32)], axis=1)
    xs, dest = _dispatch(meta[:, None, :], offdst, route, h, n_tiles * tm, n_exp)
    y = _moe_gemm(layer, tile_expert, tile_valid, xs, w_gu, b_gu, w_down, b_down)
    dest3 = dest[:, :TOP_K].reshape(n_tok_tiles, 1, SEQ_TILE * TOP_K)
    return _combine(dest3, y, x2, mod, route, ln_g, ln_b, alpha, seq)


def _split3(v):
    p0 = v.astype(BF16)
    r1 = v - p0.astype(F32)
    p1 = r1.astype(BF16)
    p2 = (r1 - p1.astype(F32)).astype(BF16)
    return p0, p1, p2


AUG_STRIDE = 8
AUG_PARTS = 3


def _aug_constants(d):
    n_pairs = d // LANES
    sel = np.zeros((AUG_PARTS, LANES, d), np.float32)
    ones = np.zeros((1, d), np.float32)
    for p in range(n_pairs):
        for hd in range(HEAD_PAIR):
            base = LANES * p + AUG_STRIDE * hd
            for part in range(AUG_PARTS):
                sel[part, HEAD_PAIR * p + hd, base + part] = 1.0
                ones[0, base + AUG_PARTS + part] = 1.0
    return jnp.asarray(sel, BF16), jnp.asarray(ones, F32)


def _kv_kernel(x_ref, mod_ref, wk_ref, wv_ref, wf_ref, bf_ref, sel_ref, ones_ref,
               k_ref, kaug_ref, vt_ref, cum_ref, carry_ref):
    ts = x_ref.shape[1]
    n_heads = cum_ref.shape[2]
    n_pairs = kaug_ref.shape[1]

    @pl.when(pl.program_id(1) == 0)
    def _():
        carry_ref[...] = jnp.zeros_like(carry_ref)

    h = _modulate(x_ref[0], mod_ref).astype(BF16)
    k_ref[0] = jnp.dot(h, wk_ref[...], preferred_element_type=F32).astype(BF16)
    vt = jnp.dot(h, wv_ref[...], preferred_element_type=F32).T.astype(BF16)
    tk = vt_ref.shape[4]
    for p in range(n_pairs):
        for c in range(ts // tk):
            vt_ref[0, p, c] = vt[LANES * p:LANES * (p + 1), tk * c:tk * (c + 1)]
    fz = jnp.dot(h, wf_ref[...], preferred_element_type=F32) + bf_ref[...]
    log_f = jnp.minimum(fz, 0.0) - jnp.log1p(jnp.exp(-jnp.abs(fz)))
    r_i = lax.broadcasted_iota(jnp.int32, (ts, ts), 0)
    c_i = lax.broadcasted_iota(jnp.int32, (ts, ts), 1)
    tri = jnp.where(c_i <= r_i, 1.0, 0.0).astype(BF16)
    cum = carry_ref[0:1, :]
    for part in _split3(log_f):
        cum = cum + jnp.dot(tri, part, preferred_element_type=F32)
    carry_ref[0:1, :] = cum[ts - 1:ts, :]
    cum = cum * LOG2E
    cum_ref[0] = cum[:, :n_heads]
    aug = ones_ref[...]
    for i, part in enumerate(_split3(cum)):
        aug = aug + jnp.dot(part, sel_ref[i], preferred_element_type=F32)
    aug = aug.astype(BF16)
    for p in range(n_pairs):
        kaug_ref[0, p] = aug[:, LANES * p:LANES * (p + 1)]


def _shared_kv(x, mod, w_k, w_v, w_f, b_f):
    bsz, seq, d = x.shape
    ts = ATTN_TILE
    tk = ATTN_KEY_TILE
    n_pairs = d // LANES
    sel, ones = _aug_constants(d)
    return pl.pallas_call(
        _kv_kernel,
        grid=(bsz, seq // ts),
        in_specs=[pl.BlockSpec((1, ts, d), lambda b, j: (b, j, 0)),
                  pl.BlockSpec((1, 2, d), lambda b, j: (b, 0, 0)),
                  pl.BlockSpec((d, d), lambda b, j: (0, 0)),
                  pl.BlockSpec((d, d), lambda b, j: (0, 0)),
                  pl.BlockSpec((d, LANES), lambda b, j: (0, 0)),
                  pl.BlockSpec((1, LANES), lambda b, j: (0, 0)),
                  pl.BlockSpec((AUG_PARTS, LANES, d), lambda b, j: (0, 0, 0)),
                  pl.BlockSpec((1, d), lambda b, j: (0, 0))],
        out_specs=[pl.BlockSpec((1, ts, d), lambda b, j: (b, j, 0)),
                   pl.BlockSpec((1, n_pairs, ts, LANES), lambda b, j: (b, 0, j, 0)),
                   pl.BlockSpec((1, n_pairs, ts // tk, LANES, tk), lambda b, j: (b, 0, j, 0, 0)),
                   pl.BlockSpec((1, ts, N_HEADS), lambda b, j: (b, j, 0))],
        out_shape=[jax.ShapeDtypeStruct((bsz, seq, d), BF16),
                   jax.ShapeDtypeStruct((bsz, n_pairs, seq, LANES), BF16),
                   jax.ShapeDtypeStruct((bsz, n_pairs, seq // tk, LANES, tk), BF16),
                   jax.ShapeDtypeStruct((bsz, seq, N_HEADS), F32)],
        scratch_shapes=[pltpu.VMEM((8, LANES), F32)],
        compiler_params=_params("arbitrary", "arbitrary"),
        name="shared_kv",
    )(x, mod, w_k, w_v, w_f, b_f, sel, ones)


def _q_proj_kernel(x_ref, mod_ref, wq_ref, q_ref, *, scale):
    h = _modulate(x_ref[0], mod_ref).astype(BF16)
    q_ref[0] = (jnp.dot(h, wq_ref[...], preferred_element_type=F32) * scale).astype(BF16)


def _q_proj(x, mod, w_q, scale):
    bsz, seq, d = x.shape
    ts = SEQ_TILE
    return pl.pallas_call(
        functools.partial(_q_proj_kernel, scale=scale),
        grid=(bsz, seq // ts),
        in_specs=[pl.BlockSpec((1, ts, d), lambda b, j: (b, j, 0)),
                  pl.BlockSpec((1, 3, d), lambda b, j: (b, 0, 0)),
                  pl.BlockSpec((d, d), lambda b, j: (0, 0))],
        out_specs=pl.BlockSpec((1, ts, d), lambda b, j: (b, j, 0)),
        out_shape=jax.ShapeDtypeStruct((bsz, seq, d), BF16),
        compiler_params=_params("arbitrary", "arbitrary"),
        name="q_proj",
    )(x, mod, w_q)


def _attn_kernel(q_ref, k_ref, kaug_ref, vt_ref, cq_ref, o_ref, acc_ref, m_ref, l_ref, s_ref):
    tq = q_ref.shape[1]
    tk = vt_ref.shape[4]
    per_q = tq // tk
    head_dim = LANES // HEAD_PAIR
    i = pl.program_id(2)
    q_t = q_ref[0].astype(F32).T
    row = lax.broadcasted_iota(jnp.int32, (LANES, tq), 0)
    rhs = []
    for g in range(PAIR_GROUP):
        q_pair = q_t[LANES * g:LANES * (g + 1), :]
        for hd in range(HEAD_PAIR):
            own = (row >= head_dim * hd) & (row < head_dim * (hd + 1))
            parts = _split3(cq_ref[0, g, hd:hd + 1, :])
            base = AUG_STRIDE * hd
            aug = jnp.where((row >= base) & (row < base + AUG_PARTS), -1.0, 0.0)
            for n, part in enumerate(parts):
                aug = jnp.where(row == base + AUG_PARTS + n, part.astype(F32), aug)
            rhs.append(jnp.concatenate([jnp.where(own, q_pair, 0.0).astype(BF16), aug.astype(BF16)],
                                       axis=0))
    ones_rows = jnp.ones((16, tk), BF16)
    acc_ref[...] = jnp.zeros_like(acc_ref)
    m_ref[...] = jnp.full_like(m_ref, NEG_INF)
    l_ref[...] = jnp.zeros_like(l_ref)

    def keys_of(j, g):
        start = pl.multiple_of(j * tk, tk)
        return jnp.concatenate([k_ref[0, pl.ds(start, tk), LANES * g:LANES * (g + 1)],
                                kaug_ref[0, g, pl.ds(start, tk), :]], axis=1)

    s_ref[...] = jnp.dot(keys_of(0, 0), rhs[0], preferred_element_type=F32)

    def chunk(j, diagonal, offset=0, has_next=True):
        keys = [keys_of(j, g) for g in range(PAIR_GROUP)]

        def masked(s_t):
            if diagonal:
                k_i = lax.broadcasted_iota(jnp.int32, (tk, tq), 0) + offset
                q_i = lax.broadcasted_iota(jnp.int32, (tk, tq), 1)
                s_t = jnp.where(k_i <= q_i, s_t, NEG_INF)
            return s_t

        n_heads = PAIR_GROUP * HEAD_PAIR
        s_next = s_ref[...]
        for n in range(n_heads):
            g, hd = divmod(n, HEAD_PAIR)
            s_t = masked(s_next)
            if n + 1 < n_heads:
                s_next = jnp.dot(keys[(n + 1) // HEAD_PAIR], rhs[n + 1], preferred_element_type=F32)
            elif has_next:
                s_ref[...] = jnp.dot(keys_of(j + 1, 0), rhs[0], preferred_element_type=F32)
            m_old = m_ref[n:n + 1, :]
            m_new = jnp.maximum(m_old, jnp.max(s_t, axis=0, keepdims=True))
            a = jnp.exp2(m_old - m_new)
            p_t = jnp.exp2(s_t - m_new).astype(BF16)
            m_ref[n:n + 1, :] = m_new
            rows = slice(head_dim * n, head_dim * (n + 1))
            lhs = jnp.concatenate([vt_ref[0, g, j, head_dim * hd:head_dim * (hd + 1), :], ones_rows], axis=0)
            pv = jnp.dot(lhs, p_t, preferred_element_type=F32)
            l_ref[n:n + 1, :] = a * l_ref[n:n + 1, :] + pv[head_dim:head_dim + 1, :]
            acc_ref[rows, :] = acc_ref[rows, :] * a + pv[:head_dim, :]

    def body(j, carry):
        chunk(j, False)
        return carry
    lax.fori_loop(0, i * per_q, body, 0)
    for sub in range(per_q):
        chunk(i * per_q + sub, True, sub * tk, has_next=sub + 1 < per_q)
    inv = jnp.concatenate([jnp.broadcast_to(1.0 / l_ref[n:n + 1, :], (head_dim, tq))
                           for n in range(PAIR_GROUP * HEAD_PAIR)], axis=0)
    o_ref[0] = (acc_ref[...] * inv).T.astype(o_ref.dtype)


def _attention(q, k, kaug, v_t, cq):
    bsz, seq, d = q.shape
    width = PAIR_GROUP * LANES
    tq = ATTN_TILE
    return pl.pallas_call(
        _attn_kernel,
        grid=(bsz, d // width, seq // tq),
        in_specs=[pl.BlockSpec((1, tq, width), lambda b, p, i: (b, i, p)),
                  pl.BlockSpec((1, seq, width), lambda b, p, i: (b, 0, p)),
                  pl.BlockSpec((1, PAIR_GROUP, seq, LANES), lambda b, p, i: (b, p, 0, 0)),
                  pl.BlockSpec((1, PAIR_GROUP) + v_t.shape[2:], lambda b, p, i: (b, p, 0, 0, 0)),
                  pl.BlockSpec((1, PAIR_GROUP, HEAD_PAIR, tq), lambda b, p, i: (b, p, 0, i))],
        out_specs=pl.BlockSpec((1, tq, width), lambda b, p, i: (b, i, p)),
        out_shape=jax.ShapeDtypeStruct((bsz, seq, d), BF16),
        scratch_shapes=[pltpu.VMEM((width, tq), F32),
                        pltpu.VMEM((8, tq), F32),
                        pltpu.VMEM((8, tq), F32),
                        pltpu.VMEM((v_t.shape[4], tq), F32)],
        compiler_params=_params("arbitrary", "arbitrary", "arbitrary"),
        name="fox_attention",
    )(q, k, kaug, v_t, cq)


def _out_proj_kernel(o_ref, x_ref, mod_ref, wo_ref, g_ref, b_ref, out_ref, *, alpha):
    y = jnp.dot(o_ref[0], wo_ref[...], preferred_element_type=F32)
    r = alpha * x_ref[0] + (1.0 + mod_ref[0, 2:3, :]) * y
    out_ref[0] = _layer_norm(r, g_ref[...], b_ref[...])


def _out_proj(o, x, mod, w_o, ln_g, ln_b, alpha):
    bsz, seq, d = x.shape
    ts = SEQ_TILE
    return pl.pallas_call(
        functools.partial(_out_proj_kernel, alpha=alpha),
        grid=(bsz, seq // ts),
        in_specs=[pl.BlockSpec((1, ts, d), lambda b, j: (b, j, 0)),
                  pl.BlockSpec((1, ts, d), lambda b, j: (b, j, 0)),
                  pl.BlockSpec((1, 3, d), lambda b, j: (b, 0, 0)),
                  pl.BlockSpec((d, d), lambda b, j: (0, 0)),
                  pl.BlockSpec((1, d), lambda b, j: (0, 0)),
                  pl.BlockSpec((1, d), lambda b, j: (0, 0))],
        out_specs=pl.BlockSpec((1, ts, d), lambda b, j: (b, j, 0)),
        out_shape=jax.ShapeDtypeStruct(x.shape, F32),
        compiler_params=_params("arbitrary", "arbitrary"),
        name="attn_out_proj",
    )(o, x, mod, w_o, ln_g, ln_b)


def kernel(x, c, conv_w_in, conv_w, conv_w_out, kv_ada_w, kv_ada_b, w_kvf, b_f, attn_w_q, attn_w_o,
           ada_w, ada_b, ln_g, ln_b, router_w, router_b, exp_w_gu, exp_b_gu, exp_w_down, exp_b_down):
    bsz, seq, d = x.shape
    depth = ada_w.shape[0]
    n_conv = conv_w_in.shape[0]
    n_exp = router_w.shape[-1]
    alpha = (2.0 * depth) ** 0.25
    head_dim = d // N_HEADS
    assert head_dim * HEAD_PAIR == LANES and seq % SEQ_TILE == 0 and seq % ATTN_TILE == 0

    c_pad = jnp.pad(c, ((0, 8 - bsz), (0, 0)))
    mods = _ada_params(c_pad, ada_w.reshape(depth * 2, d, 3 * d), ada_b.reshape(depth * 2, 1, 3 * d))
    mods = mods[:, :bsz, :].reshape(depth, 2, bsz, 3, d)
    kv_mod = _ada_params(c_pad, kv_ada_w[None], kv_ada_b[None, None])[0, :bsz].reshape(bsz, 2, d)

    w_r = jnp.pad(router_w, ((0, 0), (0, 0), (0, LANES - n_exp)))
    w_r_hi = w_r.astype(BF16)
    w_r = jnp.concatenate([w_r_hi, (w_r - w_r_hi.astype(F32)).astype(BF16)], axis=-1)
    b_r = jnp.pad(router_b, ((0, 0), (0, LANES - n_exp)), constant_values=-1e30)[:, None, :]
    b_gu = exp_b_gu[:, :, None, :]
    b_dn = exp_b_down[:, :, None, :]
    k = kaug = v_t = cq = None
    for l in range(depth):
        g0, b0 = ln_g[l, 0][None], ln_b[l, 0][None]
        if l < n_conv:
            x = _conv_layer(x, mods[l, 0], conv_w_in[l].astype(BF16), conv_w[l],
                            conv_w_out[l].astype(BF16), g0, b0, alpha)
        else:
            j = l - n_conv
            q = _q_proj(x, mods[l, 0], attn_w_q[j].astype(BF16), head_dim ** -0.5 * LOG2E)
            o = _attention(q, k, kaug, v_t, cq)
            x = _out_proj(o, x, mods[l, 0], attn_w_o[j].astype(BF16), g0, b0, alpha)
        x = _moe_layer(l, x.reshape(bsz * seq, d), mods[l, 1], w_r[l], b_r[l], exp_w_gu, b_gu,
                       exp_w_down, b_dn,
                       ln_g[l, 1][None], ln_b[l, 1][None], alpha, seq).reshape(bsz, seq, d)
        if l == n_conv - 1:
            w_f = jnp.pad(w_kvf[:, 2 * d:], ((0, 0), (0, LANES - N_HEADS))).astype(BF16)
            bias_f = jnp.pad(b_f, (0, LANES - N_HEADS))[None]
            k, kaug, v_t, cum = _shared_kv(x, kv_mod, w_kvf[:, :d].astype(BF16),
                                           w_kvf[:, d:2 * d].astype(BF16), w_f, bias_f)
            cq = cum.reshape(bsz, seq, N_HEADS // HEAD_PAIR, HEAD_PAIR).transpose(0, 2, 3, 1)
    return x
```

```python
import functools

import numpy as np
import jax
import jax.numpy as jnp
from jax import lax
from jax.experimental import pallas as pl
from jax.experimental.pallas import tpu as pltpu

N_HEADS = 16
TOP_K = 4
SWIGLU_LIMIT = 7.0
SWIGLU_ALPHA = 1.702
LN_EPS = 1e-5
LANES = 128
HEAD_PAIR = 2
PAIR_GROUP = 2
LOG2E = 1.4426950408889634
SEQ_TILE = 512
MOE_TILE = 512
ATTN_TILE = 512
ATTN_KEY_TILE = 512
RUN_ALIGN = 8
RUN_COPY = 128
VMEM_LIMIT = 56 * 1024 * 1024

F32 = jnp.float32
BF16 = jnp.bfloat16
NEG_INF = float("-inf")


def _params(*sem):
    return pltpu.CompilerParams(dimension_semantics=sem, vmem_limit_bytes=VMEM_LIMIT)


def _layer_norm(r, g, b):
    mu = jnp.mean(r, axis=-1, keepdims=True)
    d = r - mu
    var = jnp.mean(d * d, axis=-1, keepdims=True)
    return d * lax.rsqrt(var + LN_EPS) * g + b


def _modulate(x, mod_ref):
    return x * (1.0 + mod_ref[0, 1:2, :]) + mod_ref[0, 0:1, :]


def _ada_kernel(c_ref, w_ref, b_ref, o_ref):
    c = c_ref[...]
    cond = c * jax.nn.sigmoid(c)
    o_ref[0] = jnp.dot(cond, w_ref[0], precision=lax.Precision.HIGHEST,
                       preferred_element_type=F32) + b_ref[0]


def _ada_params(c_pad, w, b):
    g, d, n = w.shape
    tn = 1024 if n % 1024 == 0 else n
    return pl.pallas_call(
        _ada_kernel,
        grid=(g, n // tn),
        in_specs=[pl.BlockSpec((8, d), lambda i, j: (0, 0)),
                  pl.BlockSpec((1, d, tn), lambda i, j: (i, 0, j)),
                  pl.BlockSpec((1, 1, tn), lambda i, j: (i, 0, j))],
        out_specs=pl.BlockSpec((1, 8, tn), lambda i, j: (i, 0, j)),
        out_shape=jax.ShapeDtypeStruct((g, 8, n), F32),
        compiler_params=_params("arbitrary", "arbitrary"),
        name="ada_params",
    )(c_pad, w, b)


def _conv_layer_kernel(x_ref, mod_ref, win_ref, wc_ref, wout_ref, g_ref, b_ref, o_ref,
                       carry_ref, a_ref, *, alpha, col_chunk):
    ts, d = x_ref.shape[1], x_ref.shape[2]

    @pl.when(pl.program_id(1) == 0)
    def _():
        carry_ref[...] = jnp.zeros_like(carry_ref)

    x = x_ref[0]
    h = _modulate(x, mod_ref).astype(BF16)
    row = lax.broadcasted_iota(jnp.int32, (ts, col_chunk), 0)
    for c in range(0, d, col_chunk):
        gate_c = jnp.dot(h, win_ref[:, c:c + col_chunk], preferred_element_type=F32)
        u = jnp.dot(h, win_ref[:, 2 * d + c:2 * d + c + col_chunk], preferred_element_type=F32)
        z = gate_c * u
        z_m1 = carry_ref[1:2, c:c + col_chunk]
        z_m2 = carry_ref[0:1, c:c + col_chunk]
        z1 = jnp.where(row == 0, z_m1, pltpu.roll(z, 1, 0))
        z2 = jnp.where(row == 0, z_m2, jnp.where(row == 1, z_m1, pltpu.roll(z, 2, 0)))
        carry_ref[0:2, c:c + col_chunk] = z[ts - 2:ts, :]
        conv = (wc_ref[0:1, c:c + col_chunk] * z2 + wc_ref[1:2, c:c + col_chunk] * z1
                + wc_ref[2:3, c:c + col_chunk] * z)
        gate_b = jnp.dot(h, win_ref[:, d + c:d + c + col_chunk], preferred_element_type=F32)
        a_ref[:, c:c + col_chunk] = (gate_b * conv).astype(BF16)
    y = jnp.dot(a_ref[...], wout_ref[...], preferred_element_type=F32)
    r = alpha * x + (1.0 + mod_ref[0, 2:3, :]) * y
    o_ref[0] = _layer_norm(r, g_ref[...], b_ref[...])


def _conv_layer(x, mod, w_in, w_conv, w_out, ln_g, ln_b, alpha):
    bsz, seq, d = x.shape
    ts = SEQ_TILE
    kern = functools.partial(_conv_layer_kernel, alpha=alpha, col_chunk=256)
    return pl.pallas_call(
        kern,
        grid=(bsz, seq // ts),
        in_specs=[pl.BlockSpec((1, ts, d), lambda b, j: (b, j, 0)),
                  pl.BlockSpec((1, 3, d), lambda b, j: (b, 0, 0)),
                  pl.BlockSpec((d, 3 * d), lambda b, j: (0, 0)),
                  pl.BlockSpec((3, d), lambda b, j: (0, 0)),
                  pl.BlockSpec((d, d), lambda b, j: (0, 0)),
                  pl.BlockSpec((1, d), lambda b, j: (0, 0)),
                  pl.BlockSpec((1, d), lambda b, j: (0, 0))],
        out_specs=pl.BlockSpec((1, ts, d), lambda b, j: (b, j, 0)),
        out_shape=jax.ShapeDtypeStruct(x.shape, F32),
        scratch_shapes=[pltpu.VMEM((8, d), F32), pltpu.VMEM((ts, d), BF16)],
        compiler_params=_params("arbitrary", "arbitrary"),
        name="conv_layer",
    )(x, mod, w_in, w_conv, w_out, ln_g, ln_b)


def _pack_bf16_pairs(h):
    half = h.shape[1] // 2
    lo = lax.bitcast_convert_type(h[:, :half].astype(BF16).astype(F32), jnp.int32)
    hi = lax.bitcast_convert_type(h[:, half:].astype(BF16).astype(F32), jnp.int32)
    return lax.shift_right_logical(lo, 16) | hi


def _unpack_bf16_pairs(w):
    lo = lax.bitcast_convert_type(w << 16, F32).astype(BF16)
    hi = lax.bitcast_convert_type(w & jnp.int32(-65536), F32).astype(BF16)
    return lo, hi


def _router_kernel(x_ref, mod_ref, wr_ref, br_ref, h_ref, route_ref, cnt_ref):
    tr = x_ref.shape[0]
    h = _modulate(x_ref[...], mod_ref)
    h_hi = h.astype(BF16)
    h_ref[...] = h_hi
    h_lo = (h - h_hi.astype(F32)).astype(BF16)
    both = jnp.dot(h_hi, wr_ref[...], preferred_element_type=F32)
    logits = (both[:, :LANES] + both[:, LANES:]
              + jnp.dot(h_lo, wr_ref[:, :LANES], preferred_element_type=F32)) + br_ref[...]
    lane = lax.broadcasted_iota(jnp.int32, (tr, LANES), 1)
    lane_f = lane.astype(F32)
    work = logits
    vals, idxs, sels = [], [], []
    for _ in range(TOP_K):
        m = jnp.max(work, axis=-1, keepdims=True)
        idx = jnp.min(jnp.where(work == m, lane_f, float(LANES)), axis=-1, keepdims=True)
        sel = lane_f == idx
        vals.append(m)
        idxs.append(idx.astype(jnp.int32))
        sels.append(sel)
        work = jnp.where(sel, NEG_INF, work)
    exps = [jnp.exp(v - vals[0]) for v in vals]
    denom = exps[0] + exps[1] + exps[2] + exps[3]
    chosen = sels[0] | sels[1] | sels[2] | sels[3]
    onehot = jnp.where(chosen, 1.0, 0.0).astype(BF16)
    r_i = lax.broadcasted_iota(jnp.int32, (tr, tr), 0)
    c_i = lax.broadcasted_iota(jnp.int32, (tr, tr), 1)
    tri = jnp.where(c_i < r_i, 1.0, 0.0).astype(BF16)
    before = jnp.dot(tri, onehot, preferred_element_type=F32)
    out = jnp.zeros((tr, LANES), jnp.int32)
    for k in range(TOP_K):
        rank = jnp.sum(jnp.where(sels[k], before, 0.0), axis=-1, keepdims=True).astype(jnp.int32)
        wgt = lax.bitcast_convert_type(exps[k] / denom, jnp.int32)
        out = jnp.where(lane == k, idxs[k], out)
        out = jnp.where(lane == TOP_K + k, rank, out)
        out = jnp.where(lane == 2 * TOP_K + k, wgt, out)
    route_ref[...] = out
    counts = jnp.sum(onehot.astype(F32), axis=0, keepdims=True).astype(jnp.int32)
    cnt_ref[0] = jnp.broadcast_to(counts, (8, LANES))


def _router(x2, mod, w_r, b_r, seq):
    t, d = x2.shape
    tr = SEQ_TILE
    per_b = seq // tr
    return pl.pallas_call(
        _router_kernel,
        grid=(t // tr,),
        in_specs=[pl.BlockSpec((tr, d), lambda i: (i, 0)),
                  pl.BlockSpec((1, 3, d), lambda i: (i // per_b, 0, 0)),
                  pl.BlockSpec((d, 2 * LANES), lambda i: (0, 0)),
                  pl.BlockSpec((1, LANES), lambda i: (0, 0))],
        out_specs=[pl.BlockSpec((tr, d), lambda i: (i, 0)),
                   pl.BlockSpec((tr, LANES), lambda i: (i, 0)),
                   pl.BlockSpec((1, 8, LANES), lambda i: (i, 0, 0))],
        out_shape=[jax.ShapeDtypeStruct((t, d), BF16),
                   jax.ShapeDtypeStruct((t, LANES), jnp.int32),
                   jax.ShapeDtypeStruct((t // tr, 8, LANES), jnp.int32)],
        compiler_params=_params("arbitrary"),
        name="moe_router",
    )(x2, mod, w_r, b_r)


def _dispatch_kernel(meta_ref, prev_ref, offdst_ref, route_ref, h_ref, xs_hbm, dest_ref, xs_buf, sem, *, n_exp):
    tr, d = h_ref.shape
    n_rows = xs_buf.shape[1] - RUN_COPY
    i = pl.program_id(0)
    buf = i % 2

    @pl.when(i == 0)
    def _():
        xs_buf[:, n_rows:, :] = jnp.zeros((2, RUN_COPY, d // 2), jnp.int32)

    route = route_ref[...]
    lane = lax.broadcasted_iota(jnp.int32, (tr, LANES), 1)
    off_row = offdst_ref[0, 0:1, :]
    dst_row = offdst_ref[0, 1:2, :]
    pos = jnp.full((tr, LANES), -1.0, F32)
    dest = jnp.zeros((tr, LANES), jnp.int32)
    for k in range(TOP_K):
        sel = lane == route[:, k:k + 1]
        rank = route[:, TOP_K + k:TOP_K + k + 1].astype(F32)
        pos_k = jnp.sum(jnp.where(sel, off_row, 0.0), axis=-1, keepdims=True) + rank
        dest_k = jnp.sum(jnp.where(sel, dst_row, 0.0), axis=-1, keepdims=True) + rank
        pos = jnp.where(lane == k, pos_k, pos)
        dest = jnp.where(lane == k, dest_k.astype(jnp.int32), dest)
    dest_ref[...] = dest
    pos_t = pos.T
    h = h_ref[...]
    chunk = n_rows // 4
    for c in range(4):
        slot = (lax.broadcasted_iota(jnp.int32, (chunk, tr), 0) + chunk * c).astype(F32)
        hit = slot == pos_t[0:1, :]
        for k in range(1, TOP_K):
            hit = hit | (slot == pos_t[k:k + 1, :])
        onehot = jnp.where(hit, 1.0, 0.0).astype(BF16)
        rows = jnp.dot(onehot, h, preferred_element_type=F32)
        xs_buf[buf, chunk * c:chunk * (c + 1), :] = _pack_bf16_pairs(rows)

    def wait_copies(count):
        def wait_one(n, carry):
            pltpu.make_async_copy(xs_buf.at[0, pl.ds(0, RUN_COPY)], xs_hbm.at[pl.ds(0, RUN_COPY)], sem).wait()
            return carry
        lax.fori_loop(0, count, wait_one, 0)

    @pl.when(i > 0)
    def _():
        wait_copies(prev_ref[0, 0, 3 * LANES // 4])

    for e in range(n_exp):
        src = pl.multiple_of(meta_ref[0, 0, e], RUN_ALIGN)
        dst = pl.multiple_of(meta_ref[0, 0, LANES // 4 + e], RUN_ALIGN)
        length = meta_ref[0, 0, LANES // 2 + e]
        for q in range(tr // RUN_COPY):
            @pl.when(length > RUN_COPY * q)
            def _():
                pltpu.make_async_copy(xs_buf.at[buf, pl.ds(src + RUN_COPY * q, RUN_COPY)],
                                      xs_hbm.at[pl.ds(dst + RUN_COPY * q, RUN_COPY)], sem).start()

    @pl.when(i == pl.num_programs(0) - 1)
    def _():
        wait_copies(meta_ref[0, 0, 3 * LANES // 4])


def _dispatch(meta, offdst, route, h, n_slots, n_exp):
    t, d = h.shape
    tr = SEQ_TILE
    n_rows = TOP_K * tr + n_exp * RUN_ALIGN
    return pl.pallas_call(
        functools.partial(_dispatch_kernel, n_exp=n_exp),
        grid=(t // tr,),
        in_specs=[pl.BlockSpec((1, 1, LANES), lambda i: (i, 0, 0), memory_space=pltpu.SMEM),
                  pl.BlockSpec((1, 1, LANES), lambda i: (jnp.maximum(i - 1, 0), 0, 0), memory_space=pltpu.SMEM),
                  pl.BlockSpec((1, 8, LANES), lambda i: (i, 0, 0)),
                  pl.BlockSpec((tr, LANES), lambda i: (i, 0)),
                  pl.BlockSpec((tr, d), lambda i: (i, 0))],
        out_specs=[pl.BlockSpec(memory_space=pl.ANY),
                   pl.BlockSpec((tr, LANES), lambda i: (i, 0))],
        out_shape=[jax.ShapeDtypeStruct((n_slots, d // 2), jnp.int32),
                   jax.ShapeDtypeStruct((t, LANES), jnp.int32)],
        scratch_shapes=[pltpu.VMEM((2, n_rows + RUN_COPY, d // 2), jnp.int32), pltpu.SemaphoreType.DMA(())],
        compiler_params=_params("arbitrary"),
        name="moe_dispatch",
    )(meta, meta, offdst, route, h)


def _row_copy(src_hbm, dst_vmem, sem, src_row, dst_row):
    return pltpu.make_async_copy(src_hbm.at[pl.ds(src_row, 1)], dst_vmem.at[pl.ds(dst_row, 1)], sem)


def _moe_gemm_kernel(te_ref, valid_ref, xs_ref, wgu_ref, bgu_ref, wd_ref, bd_ref, y_ref, wgu_bf, wd_bf):
    half = xs_ref.shape[1]
    f = wd_bf.shape[0]
    i = pl.program_id(0)

    @pl.when((i == 0) | (te_ref[i] != te_ref[jnp.maximum(i - 1, 0)]))
    def _():
        wgu_bf[...] = wgu_ref[0, 0].astype(BF16)
        wd_bf[...] = wd_ref[0, 0].astype(BF16)

    @pl.when(valid_ref[i] == 1)
    def _():
        lo, hi = _unpack_bf16_pairs(xs_ref[...])
        gu = (jnp.dot(lo, wgu_bf[:half, :], preferred_element_type=F32)
              + jnp.dot(hi, wgu_bf[half:, :], preferred_element_type=F32) + bgu_ref[0, 0])
        g = jnp.minimum(gu[:, :f], SWIGLU_LIMIT)
        u = jnp.clip(gu[:, f:], -SWIGLU_LIMIT, SWIGLU_LIMIT)
        a = g * jax.nn.sigmoid(SWIGLU_ALPHA * g) * (u + 1.0)
        y = jnp.dot(a.astype(BF16), wd_bf[...], preferred_element_type=F32) + bd_ref[0, 0]
        y_ref[...] = _pack_bf16_pairs(y)

    @pl.when(valid_ref[i] == 0)
    def _():
        y_ref[...] = jnp.zeros_like(y_ref)


def _moe_gemm(layer, tile_expert, tile_valid, xs, w_gu, b_gu, w_down, b_down):
    n_tiles = tile_expert.shape[0]
    tm = MOE_TILE
    _, _, d, f2 = w_gu.shape
    f = f2 // 2
    grid_spec = pltpu.PrefetchScalarGridSpec(
        num_scalar_prefetch=2,
        grid=(n_tiles,),
        in_specs=[pl.BlockSpec((tm, d // 2), lambda i, te, tv: (i, 0)),
                  pl.BlockSpec((1, 1, d, f2), lambda i, te, tv: (layer, te[i], 0, 0)),
                  pl.BlockSpec((1, 1, 1, f2), lambda i, te, tv: (layer, te[i], 0, 0)),
                  pl.BlockSpec((1, 1, f, d), lambda i, te, tv: (layer, te[i], 0, 0)),
                  pl.BlockSpec((1, 1, 1, d), lambda i, te, tv: (layer, te[i], 0, 0))],
        out_specs=pl.BlockSpec((tm, d // 2), lambda i, te, tv: (i, 0)),
        scratch_shapes=[pltpu.VMEM((d, f2), BF16), pltpu.VMEM((f, d), BF16)],
    )
    return pl.pallas_call(
        _moe_gemm_kernel,
        grid_spec=grid_spec,
        out_shape=jax.ShapeDtypeStruct((n_tiles * tm, d // 2), jnp.int32),
        compiler_params=_params("arbitrary"),
        name="moe_experts",
    )(tile_expert, tile_valid, xs, w_gu, b_gu, w_down, b_down)


def _combine_kernel(dest_ref, y_hbm, x_ref, mod_ref, route_ref, g_ref, b_ref, o_ref, ybuf, sem, *, alpha):
    tr = x_ref.shape[0]

    for r in range(tr):
        for k in range(TOP_K):
            _row_copy(y_hbm, ybuf.at[k], sem, dest_ref[0, 0, r * TOP_K + k], r).start(priority=k % 2)

    for k in range(TOP_K):
        pltpu.make_async_copy(y_hbm.at[pl.ds(0, tr)], ybuf.at[k], sem).wait()

    wts = lax.bitcast_convert_type(route_ref[...], F32)
    sub_lo = sub_hi = None
    for k in range(TOP_K):
        w_k = wts[:, 2 * TOP_K + k:2 * TOP_K + k + 1]
        words = ybuf[k]
        lo = w_k * lax.bitcast_convert_type(words << 16, F32)
        hi = w_k * lax.bitcast_convert_type(words & jnp.int32(-65536), F32)
        sub_lo = lo if sub_lo is None else sub_lo + lo
        sub_hi = hi if sub_hi is None else sub_hi + hi
    sub = jnp.concatenate([sub_lo, sub_hi], axis=1)
    x = x_ref[...]
    r = alpha * x + (1.0 + mod_ref[0, 2:3, :]) * sub
    o_ref[...] = _layer_norm(r, g_ref[...], b_ref[...])


def _combine(dest, y, x2, mod, route, ln_g, ln_b, alpha, seq):
    t, d = x2.shape
    tr = SEQ_TILE
    per_b = seq // tr
    kern = functools.partial(_combine_kernel, alpha=alpha)
    return pl.pallas_call(
        kern,
        grid=(t // tr,),
        in_specs=[pl.BlockSpec((1, 1, tr * TOP_K), lambda i: (i, 0, 0), memory_space=pltpu.SMEM),
                  pl.BlockSpec(memory_space=pl.ANY),
                  pl.BlockSpec((tr, d), lambda i: (i, 0)),
                  pl.BlockSpec((1, 3, d), lambda i: (i // per_b, 0, 0)),
                  pl.BlockSpec((tr, LANES), lambda i: (i, 0)),
                  pl.BlockSpec((1, d), lambda i: (0, 0)),
                  pl.BlockSpec((1, d), lambda i: (0, 0))],
        out_specs=pl.BlockSpec((tr, d), lambda i: (i, 0)),
        out_shape=jax.ShapeDtypeStruct((t, d), F32),
        scratch_shapes=[pltpu.VMEM((TOP_K, tr, d // 2), jnp.int32), pltpu.SemaphoreType.DMA(())],
        compiler_params=_params("arbitrary"),
        name="moe_combine",
    )(dest, y, x2, mod, route, ln_g, ln_b)


def _moe_layer(layer, x2, mod, w_r, b_r, w_gu, b_gu, w_down, b_down, ln_g, ln_b, alpha, seq):
    t, d = x2.shape
    n_exp = w_gu.shape[1]
    tm = MOE_TILE
    n_tok_tiles = t // SEQ_TILE
    assert n_exp <= LANES // 4 and SEQ_TILE % RUN_COPY == 0
    h, route, cnt = _router(x2, mod, w_r, b_r, seq)
    runs = (cnt[:, 0, :n_exp] + RUN_ALIGN - 1) // RUN_ALIGN * RUN_ALIGN
    run_off = jnp.cumsum(runs, axis=1) - runs
    run_base = jnp.cumsum(runs, axis=0) - runs
    rows = jnp.sum(runs, axis=0)
    region = (rows + RUN_COPY + tm - 1) // tm * tm
    region_end = jnp.cumsum(region)
    region_start = region_end - region
    run_dst = region_start[None, :] + run_base
    n_tiles = -(-(t * TOP_K + n_tok_tiles * n_exp * (RUN_ALIGN - 1) + n_exp * (RUN_COPY + tm - 1)) // tm)
    tile_start = jnp.arange(n_tiles, dtype=jnp.int32) * tm
    tile_expert = jnp.minimum(jnp.sum(region_end[None, :] <= tile_start[:, None], axis=1),
                              n_exp - 1).astype(jnp.int32)
    tile_valid = ((tile_start - region_start[tile_expert] < rows[tile_expert])
                  & (tile_start < region_end[-1])).astype(jnp.int32)
    pad = jnp.zeros((n_tok_tiles, LANES // 4 - n_exp), jnp.int32)
    n_copies = jnp.sum((runs + RUN_COPY - 1) // RUN_COPY, axis=1, keepdims=True)
    meta = jnp.concatenate([run_off, pad, run_dst, pad, runs, pad, n_copies,
                            jnp.zeros((n_tok_tiles, LANES // 4 - 1), jnp.int32)], axis=1).astype(jnp.int32)
    wide = lambda v: jnp.pad(v.astype(F32), ((0, 0), (0, LANES - n_exp)))
    offdst = jnp.concatenate([wide(run_off)[:, None], wide(run_dst)[:, None],
                              jnp.zeros((n_tok_tiles, 6, LANES), F32)], axis=1)
    xs, dest = _dispatch(meta[:, None, :], offdst, route, h, n_tiles * tm, n_exp)
    y = _moe_gemm(layer, tile_expert, tile_valid, xs, w_gu, b_gu, w_down, b_down)
    dest3 = dest[:, :TOP_K].reshape(n_tok_tiles, 1, SEQ_TILE * TOP_K)
    return _combine(dest3, y, x2, mod, route, ln_g, ln_b, alpha, seq)


def _split3(v):
    p0 = v.astype(BF16)
    r1 = v - p0.astype(F32)
    p1 = r1.astype(BF16)
    p2 = (r1 - p1.astype(F32)).astype(BF16)
    return p0, p1, p2


AUG_STRIDE = 8
AUG_PARTS = 3


def _aug_constants(d):
    n_pairs = d // LANES
    sel = np.zeros((AUG_PARTS, LANES, d), np.float32)
    ones = np.zeros((1, d), np.float32)
    for p in range(n_pairs):
        for hd in range(HEAD_PAIR):
            base = LANES * p + AUG_STRIDE * hd
            for part in range(AUG_PARTS):
                sel[part, HEAD_PAIR * p + hd, base + part] = 1.0
                ones[0, base + AUG_PARTS + part] = 1.0
    return jnp.asarray(sel, BF16), jnp.asarray(ones, F32)


def _kv_kernel(x_ref, mod_ref, wk_ref, wv_ref, wf_ref, bf_ref, sel_ref, ones_ref,
               k_ref, kaug_ref, vt_ref, cum_ref, carry_ref):
    ts = x_ref.shape[1]
    n_heads = cum_ref.shape[2]
    n_pairs = kaug_ref.shape[1]

    @pl.when(pl.program_id(1) == 0)
    def _():
        carry_ref[...] = jnp.zeros_like(carry_ref)

    h = _modulate(x_ref[0], mod_ref).astype(BF16)
    k_ref[0] = jnp.dot(h, wk_ref[...], preferred_element_type=F32).astype(BF16)
    vt = jnp.dot(h, wv_ref[...], preferred_element_type=F32).T.astype(BF16)
    tk = vt_ref.shape[4]
    for p in range(n_pairs):
        for c in range(ts // tk):
            vt_ref[0, p, c] = vt[LANES * p:LANES * (p + 1), tk * c:tk * (c + 1)]
    fz = jnp.dot(h, wf_ref[...], preferred_element_type=F32) + bf_ref[...]
    log_f = jnp.minimum(fz, 0.0) - jnp.log1p(jnp.exp(-jnp.abs(fz)))
    r_i = lax.broadcasted_iota(jnp.int32, (ts, ts), 0)
    c_i = lax.broadcasted_iota(jnp.int32, (ts, ts), 1)
    tri = jnp.where(c_i <= r_i, 1.0, 0.0).astype(BF16)
    cum = carry_ref[0:1, :]
    for part in _split3(log_f):
        cum = cum + jnp.dot(tri, part, preferred_element_type=F32)
    carry_ref[0:1, :] = cum[ts - 1:ts, :]
    cum = cum * LOG2E
    cum_ref[0] = cum[:, :n_heads]
    aug = ones_ref[...]
    for i, part in enumerate(_split3(cum)):
        aug = aug + jnp.dot(part, sel_ref[i], preferred_element_type=F32)
    aug = aug.astype(BF16)
    for p in range(n_pairs):
        kaug_ref[0, p] = aug[:, LANES * p:LANES * (p + 1)]


def _shared_kv(x, mod, w_k, w_v, w_f, b_f):
    bsz, seq, d = x.shape
    ts = ATTN_TILE
    tk = ATTN_KEY_TILE
    n_pairs = d // LANES
    sel, ones = _aug_constants(d)
    return pl.pallas_call(
        _kv_kernel,
        grid=(bsz, seq // ts),
        in_specs=[pl.BlockSpec((1, ts, d), lambda b, j: (b, j, 0)),
                  pl.BlockSpec((1, 2, d), lambda b, j: (b, 0, 0)),
                  pl.BlockSpec((d, d), lambda b, j: (0, 0)),
                  pl.BlockSpec((d, d), lambda b, j: (0, 0)),
                  pl.BlockSpec((d, LANES), lambda b, j: (0, 0)),
                  pl.BlockSpec((1, LANES), lambda b, j: (0, 0)),
                  pl.BlockSpec((AUG_PARTS, LANES, d), lambda b, j: (0, 0, 0)),
                  pl.BlockSpec((1, d), lambda b, j: (0, 0))],
        out_specs=[pl.BlockSpec((1, ts, d), lambda b, j: (b, j, 0)),
                   pl.BlockSpec((1, n_pairs, ts, LANES), lambda b, j: (b, 0, j, 0)),
                   pl.BlockSpec((1, n_pairs, ts // tk, LANES, tk), lambda b, j: (b, 0, j, 0, 0)),
                   pl.BlockSpec((1, ts, N_HEADS), lambda b, j: (b, j, 0))],
        out_shape=[jax.ShapeDtypeStruct((bsz, seq, d), BF16),
                   jax.ShapeDtypeStruct((bsz, n_pairs, seq, LANES), BF16),
                   jax.ShapeDtypeStruct((bsz, n_pairs, seq // tk, LANES, tk), BF16),
                   jax.ShapeDtypeStruct((bsz, seq, N_HEADS), F32)],
        scratch_shapes=[pltpu.VMEM((8, LANES), F32)],
        compiler_params=_params("arbitrary", "arbitrary"),
        name="shared_kv",
    )(x, mod, w_k, w_v, w_f, b_f, sel, ones)


def _q_proj_kernel(x_ref, mod_ref, wq_ref, q_ref, *, scale):
    h = _modulate(x_ref[0], mod_ref).astype(BF16)
    q_ref[0] = (jnp.dot(h, wq_ref[...], preferred_element_type=F32) * scale).astype(BF16)


def _q_proj(x, mod, w_q, scale):
    bsz, seq, d = x.shape
    ts = SEQ_TILE
    return pl.pallas_call(
        functools.partial(_q_proj_kernel, scale=scale),
        grid=(bsz, seq // ts),
        in_specs=[pl.BlockSpec((1, ts, d), lambda b, j: (b, j, 0)),
                  pl.BlockSpec((1, 3, d), lambda b, j: (b, 0, 0)),
                  pl.BlockSpec((d, d), lambda b, j: (0, 0))],
        out_specs=pl.BlockSpec((1, ts, d), lambda b, j: (b, j, 0)),
        out_shape=jax.ShapeDtypeStruct((bsz, seq, d), BF16),
        compiler_params=_params("arbitrary", "arbitrary"),
        name="q_proj",
    )(x, mod, w_q)


def _attn_kernel(q_ref, k_ref, kaug_ref, vt_ref, cq_ref, o_ref, acc_ref, m_ref, l_ref, s_ref):
    tq = q_ref.shape[1]
    tk = vt_ref.shape[4]
    per_q = tq // tk
    head_dim = LANES // HEAD_PAIR
    i = pl.program_id(2)
    q_t = q_ref[0].astype(F32).T
    row = lax.broadcasted_iota(jnp.int32, (LANES, tq), 0)
    rhs = []
    for g in range(PAIR_GROUP):
        q_pair = q_t[LANES * g:LANES * (g + 1), :]
        for hd in range(HEAD_PAIR):
            own = (row >= head_dim * hd) & (row < head_dim * (hd + 1))
            parts = _split3(cq_ref[0, g, hd:hd + 1, :])
            base = AUG_STRIDE * hd
            aug = jnp.where((row >= base) & (row < base + AUG_PARTS), -1.0, 0.0)
            for n, part in enumerate(parts):
                aug = jnp.where(row == base + AUG_PARTS + n, part.astype(F32), aug)
            rhs.append(jnp.concatenate([jnp.where(own, q_pair, 0.0).astype(BF16), aug.astype(BF16)],
                                       axis=0))
    ones_rows = jnp.ones((16, tk), BF16)
    acc_ref[...] = jnp.zeros_like(acc_ref)
    m_ref[...] = jnp.full_like(m_ref, NEG_INF)
    l_ref[...] = jnp.zeros_like(l_ref)

    def keys_of(j, g):
        start = pl.multiple_of(j * tk, tk)
        return jnp.concatenate([k_ref[0, pl.ds(start, tk), LANES * g:LANES * (g + 1)],
                                kaug_ref[0, g, pl.ds(start, tk), :]], axis=1)

    s_ref[...] = jnp.dot(keys_of(0, 0), rhs[0], preferred_element_type=F32)

    def chunk(j, diagonal, offset=0, has_next=True):
        keys = [keys_of(j, g) for g in range(PAIR_GROUP)]

        def masked(s_t):
            if diagonal:
                k_i = lax.broadcasted_iota(jnp.int32, (tk, tq), 0) + offset
                q_i = lax.broadcasted_iota(jnp.int32, (tk, tq), 1)
                s_t = jnp.where(k_i <= q_i, s_t, NEG_INF)
            return s_t

        n_heads = PAIR_GROUP * HEAD_PAIR
        s_next = s_ref[...]
        for n in range(n_heads):
            g, hd = divmod(n, HEAD_PAIR)
            s_t = masked(s_next)
            if n + 1 < n_heads:
                s_next = jnp.dot(keys[(n + 1) // HEAD_PAIR], rhs[n + 1], preferred_element_type=F32)
            elif has_next:
                s_ref[...] = jnp.dot(keys_of(j + 1, 0), rhs[0], preferred_element_type=F32)
            m_old = m_ref[n:n + 1, :]
            m_new = jnp.maximum(m_old, jnp.max(s_t, axis=0, keepdims=True))
            a = jnp.exp2(m_old - m_new)
            p_t = jnp.exp2(s_t - m_new).astype(BF16)
            m_ref[n:n + 1, :] = m_new
            rows = slice(head_dim * n, head_dim * (n + 1))
            lhs = jnp.concatenate([vt_ref[0, g, j, head_dim * hd:head_dim * (hd + 1), :], ones_rows], axis=0)
            pv = jnp.dot(lhs, p_t, preferred_element_type=F32)
            l_ref[n:n + 1, :] = a * l_ref[n:n + 1, :] + pv[head_dim:head_dim + 1, :]
            acc_ref[rows, :] = acc_ref[rows, :] * a + pv[:head_dim, :]

    def body(j, carry):
        chunk(j, False)
        return carry
    lax.fori_loop(0, i * per_q, body, 0)
    for sub in range(per_q):
        chunk(i * per_q + sub, True, sub * tk, has_next=sub + 1 < per_q)
    inv = jnp.concatenate([jnp.broadcast_to(1.0 / l_ref[n:n + 1, :], (head_dim, tq))
                           for n in range(PAIR_GROUP * HEAD_PAIR)], axis=0)
    o_ref[0] = (acc_ref[...] * inv).T.astype(o_ref.dtype)


def _attention(q, k, kaug, v_t, cq):
    bsz, seq, d = q.shape
    width = PAIR_GROUP * LANES
    tq = ATTN_TILE
    return pl.pallas_call(
        _attn_kernel,
        grid=(bsz, d // width, seq // tq),
        in_specs=[pl.BlockSpec((1, tq, width), lambda b, p, i: (b, i, p)),
                  pl.BlockSpec((1, seq, width), lambda b, p, i: (b, 0, p)),
                  pl.BlockSpec((1, PAIR_GROUP, seq, LANES), lambda b, p, i: (b, p, 0, 0)),
                  pl.BlockSpec((1, PAIR_GROUP) + v_t.shape[2:], lambda b, p, i: (b, p, 0, 0, 0)),
                  pl.BlockSpec((1, PAIR_GROUP, HEAD_PAIR, tq), lambda b, p, i: (b, p, 0, i))],
        out_specs=pl.BlockSpec((1, tq, width), lambda b, p, i: (b, i, p)),
        out_shape=jax.ShapeDtypeStruct((bsz, seq, d), BF16),
        scratch_shapes=[pltpu.VMEM((width, tq), F32),
                        pltpu.VMEM((8, tq), F32),
                        pltpu.VMEM((8, tq), F32),
                        pltpu.VMEM((v_t.shape[4], tq), F32)],
        compiler_params=_params("arbitrary", "arbitrary", "arbitrary"),
        name="fox_attention",
    )(q, k, kaug, v_t, cq)


def _out_proj_kernel(o_ref, x_ref, mod_ref, wo_ref, g_ref, b_ref, out_ref, *, alpha):
    y = jnp.dot(o_ref[0], wo_ref[...], preferred_element_type=F32)
    r = alpha * x_ref[0] + (1.0 + mod_ref[0, 2:3, :]) * y
    out_ref[0] = _layer_norm(r, g_ref[...], b_ref[...])


def _out_proj(o, x, mod, w_o, ln_g, ln_b, alpha):
    bsz, seq, d = x.shape
    ts = SEQ_TILE
    return pl.pallas_call(
        functools.partial(_out_proj_kernel, alpha=alpha),
        grid=(bsz, seq // ts),
        in_specs=[pl.BlockSpec((1, ts, d), lambda b, j: (b, j, 0)),
                  pl.BlockSpec((1, ts, d), lambda b, j: (b, j, 0)),
                  pl.BlockSpec((1, 3, d), lambda b, j: (b, 0, 0)),
                  pl.BlockSpec((d, d), lambda b, j: (0, 0)),
                  pl.BlockSpec((1, d), lambda b, j: (0, 0)),
                  pl.BlockSpec((1, d), lambda b, j: (0, 0))],
        out_specs=pl.BlockSpec((1, ts, d), lambda b, j: (b, j, 0)),
        out_shape=jax.ShapeDtypeStruct(x.shape, F32),
        compiler_params=_params("arbitrary", "arbitrary"),
        name="attn_out_proj",
    )(o, x, mod, w_o, ln_g, ln_b)


def kernel(x, c, conv_w_in, conv_w, conv_w_out, kv_ada_w, kv_ada_b, w_kvf, b_f, attn_w_q, attn_w_o,
           ada_w, ada_b, ln_g, ln_b, router_w, router_b, exp_w_gu, exp_b_gu, exp_w_down, exp_b_down):
    bsz, seq, d = x.shape
    depth = ada_w.shape[0]
    n_conv = conv_w_in.shape[0]
    n_exp = router_w.shape[-1]
    alpha = (2.0 * depth) ** 0.25
    head_dim = d // N_HEADS
    assert head_dim * HEAD_PAIR == LANES and seq % SEQ_TILE == 0 and seq % ATTN_TILE == 0

    c_pad = jnp.pad(c, ((0, 8 - bsz), (0, 0)))
    mods = _ada_params(c_pad, ada_w.reshape(depth * 2, d, 3 * d), ada_b.reshape(depth * 2, 1, 3 * d))
    mods = mods[:, :bsz, :].reshape(depth, 2, bsz, 3, d)
    kv_mod = _ada_params(c_pad, kv_ada_w[None], kv_ada_b[None, None])[0, :bsz].reshape(bsz, 2, d)

    w_r = jnp.pad(router_w, ((0, 0), (0, 0), (0, LANES - n_exp)))
    w_r_hi = w_r.astype(BF16)
    w_r = jnp.concatenate([w_r_hi, (w_r - w_r_hi.astype(F32)).astype(BF16)], axis=-1)
    b_r = jnp.pad(router_b, ((0, 0), (0, LANES - n_exp)), constant_values=-1e30)[:, None, :]
    b_gu = exp_b_gu[:, :, None, :]
    b_dn = exp_b_down[:, :, None, :]
    k = kaug = v_t = cq = None
    for l in range(depth):
        g0, b0 = ln_g[l, 0][None], ln_b[l, 0][None]
        if l < n_conv:
            x = _conv_layer(x, mods[l, 0], conv_w_in[l].astype(BF16), conv_w[l],
                            conv_w_out[l].astype(BF16), g0, b0, alpha)
        else:
            j = l - n_conv
            q = _q_proj(x, mods[l, 0], attn_w_q[j].astype(BF16), head_dim ** -0.5 * LOG2E)
            o = _attention(q, k, kaug, v_t, cq)
            x = _out_proj(o, x, mods[l, 0], attn_w_o[j].astype(BF16), g0, b0, alpha)
        x = _moe_layer(l, x.reshape(bsz * seq, d), mods[l, 1], w_r[l], b_r[l], exp_w_gu, b_gu,
                       exp_w_down, b_dn,
                       ln_g[l, 1][None], ln_b[l, 1][None], alpha, seq).reshape(bsz, seq, d)
        if l == n_conv - 1:
            w_f = jnp.pad(w_kvf[:, 2 * d:], ((0, 0), (0, LANES - N_HEADS))).astype(BF16)
            bias_f = jnp.pad(b_f, (0, LANES - N_HEADS))[None]
            k, kaug, v_t, cum = _shared_kv(x, kv_mod, w_kvf[:, :d].astype(BF16),
                                           w_kvf[:, d:2 * d].astype(BF16), w_f, bias_f)
            cq = cum.reshape(bsz, seq, N_HEADS // HEAD_PAIR, HEAD_PAIR).transpose(0, 2, 3, 1)
    return x
```

```python
import functools

import numpy as np
import jax
import jax.numpy as jnp
from jax import lax
from jax.experimental import pallas as pl
from jax.experimental.pallas import tpu as pltpu

N_HEADS = 16
TOP_K = 4
SWIGLU_LIMIT = 7.0
SWIGLU_ALPHA = 1.702
LN_EPS = 1e-5
LANES = 128
HEAD_PAIR = 2
PAIR_GROUP = 2
LOG2E = 1.4426950408889634
SEQ_TILE = 512
MOE_TILE = 512
ATTN_TILE = 512
ATTN_KEY_TILE = 512
RUN_ALIGN = 8
RUN_COPY = 128
RUN_BITS = 7
VMEM_LIMIT = 56 * 1024 * 1024

F32 = jnp.float32
BF16 = jnp.bfloat16
NEG_INF = float("-inf")


def _params(*sem):
    return pltpu.CompilerParams(dimension_semantics=sem, vmem_limit_bytes=VMEM_LIMIT)


def _layer_norm(r, g, b):
    mu = jnp.mean(r, axis=-1, keepdims=True)
    d = r - mu
    var = jnp.mean(d * d, axis=-1, keepdims=True)
    return d * lax.rsqrt(var + LN_EPS) * g + b


def _modulate(x, mod_ref):
    return x * (1.0 + mod_ref[0, 1:2, :]) + mod_ref[0, 0:1, :]


def _ada_kernel(c_ref, w_ref, b_ref, o_ref):
    c = c_ref[...]
    cond = c * jax.nn.sigmoid(c)
    o_ref[0] = jnp.dot(cond, w_ref[0], precision=lax.Precision.HIGHEST,
                       preferred_element_type=F32) + b_ref[0]


def _ada_params(c_pad, w, b):
    g, d, n = w.shape
    tn = 1024 if n % 1024 == 0 else n
    return pl.pallas_call(
        _ada_kernel,
        grid=(g, n // tn),
        in_specs=[pl.BlockSpec((8, d), lambda i, j: (0, 0)),
                  pl.BlockSpec((1, d, tn), lambda i, j: (i, 0, j)),
                  pl.BlockSpec((1, 1, tn), lambda i, j: (i, 0, j))],
        out_specs=pl.BlockSpec((1, 8, tn), lambda i, j: (i, 0, j)),
        out_shape=jax.ShapeDtypeStruct((g, 8, n), F32),
        compiler_params=_params("arbitrary", "arbitrary"),
        name="ada_params",
    )(c_pad, w, b)


def _conv_layer_kernel(x_ref, mod_ref, win_ref, wc_ref, wout_ref, g_ref, b_ref, o_ref,
                       carry_ref, a_ref, *, alpha, col_chunk):
    ts, d = x_ref.shape[1], x_ref.shape[2]

    @pl.when(pl.program_id(1) == 0)
    def _():
        carry_ref[...] = jnp.zeros_like(carry_ref)

    x = x_ref[0]
    h = _modulate(x, mod_ref).astype(BF16)
    row = lax.broadcasted_iota(jnp.int32, (ts, col_chunk), 0)
    for c in range(0, d, col_chunk):
        gate_c = jnp.dot(h, win_ref[:, c:c + col_chunk], preferred_element_type=F32)
        u = jnp.dot(h, win_ref[:, 2 * d + c:2 * d + c + col_chunk], preferred_element_type=F32)
        z = gate_c * u
        z_m1 = carry_ref[1:2, c:c + col_chunk]
        z_m2 = carry_ref[0:1, c:c + col_chunk]
        z1 = jnp.where(row == 0, z_m1, pltpu.roll(z, 1, 0))
        z2 = jnp.where(row == 0, z_m2, jnp.where(row == 1, z_m1, pltpu.roll(z, 2, 0)))
        carry_ref[0:2, c:c + col_chunk] = z[ts - 2:ts, :]
        conv = (wc_ref[0:1, c:c + col_chunk] * z2 + wc_ref[1:2, c:c + col_chunk] * z1
                + wc_ref[2:3, c:c + col_chunk] * z)
        gate_b = jnp.dot(h, win_ref[:, d + c:d + c + col_chunk], preferred_element_type=F32)
        a_ref[:, c:c + col_chunk] = (gate_b * conv).astype(BF16)
    y = jnp.dot(a_ref[...], wout_ref[...], preferred_element_type=F32)
    r = alpha * x + (1.0 + mod_ref[0, 2:3, :]) * y
    o_ref[0] = _layer_norm(r, g_ref[...], b_ref[...])


def _conv_layer(x, mod, w_in, w_conv, w_out, ln_g, ln_b, alpha):
    bsz, seq, d = x.shape
    ts = SEQ_TILE
    kern = functools.partial(_conv_layer_kernel, alpha=alpha, col_chunk=256)
    return pl.pallas_call(
        kern,
        grid=(bsz, seq // ts),
        in_specs=[pl.BlockSpec((1, ts, d), lambda b, j: (b, j, 0)),
                  pl.BlockSpec((1, 3, d), lambda b, j: (b, 0, 0)),
                  pl.BlockSpec((d, 3 * d), lambda b, j: (0, 0)),
                  pl.BlockSpec((3, d), lambda b, j: (0, 0)),
                  pl.BlockSpec((d, d), lambda b, j: (0, 0)),
                  pl.BlockSpec((1, d), lambda b, j: (0, 0)),
                  pl.BlockSpec((1, d), lambda b, j: (0, 0))],
        out_specs=pl.BlockSpec((1, ts, d), lambda b, j: (b, j, 0)),
        out_shape=jax.ShapeDtypeStruct(x.shape, F32),
        scratch_shapes=[pltpu.VMEM((8, d), F32), pltpu.VMEM((ts, d), BF16)],
        compiler_params=_params("arbitrary", "arbitrary"),
        name="conv_layer",
    )(x, mod, w_in, w_conv, w_out, ln_g, ln_b)


def _pack_bf16_pairs(h):
    half = h.shape[1] // 2
    lo = lax.bitcast_convert_type(h[:, :half].astype(BF16).astype(F32), jnp.int32)
    hi = lax.bitcast_convert_type(h[:, half:].astype(BF16).astype(F32), jnp.int32)
    return lax.shift_right_logical(lo, 16) | hi


def _unpack_bf16_pairs(w):
    lo = lax.bitcast_convert_type(w << 16, F32).astype(BF16)
    hi = lax.bitcast_convert_type(w & jnp.int32(-65536), F32).astype(BF16)
    return lo, hi


def _router_kernel(x_ref, mod_ref, wr_ref, br_ref, h_ref, route_ref, cnt_ref):
    tr = x_ref.shape[0]
    h = _modulate(x_ref[...], mod_ref)
    h_hi = h.astype(BF16)
    h_ref[...] = h_hi
    h_lo = (h - h_hi.astype(F32)).astype(BF16)
    both = jnp.dot(h_hi, wr_ref[...], preferred_element_type=F32)
    logits = (both[:, :LANES] + both[:, LANES:]
              + jnp.dot(h_lo, wr_ref[:, :LANES], preferred_element_type=F32)) + br_ref[...]
    lane = lax.broadcasted_iota(jnp.int32, (tr, LANES), 1)
    lane_f = lane.astype(F32)
    work = logits
    vals, idxs, sels = [], [], []
    for _ in range(TOP_K):
        m = jnp.max(work, axis=-1, keepdims=True)
        idx = jnp.min(jnp.where(work == m, lane_f, float(LANES)), axis=-1, keepdims=True)
        sel = lane_f == idx
        vals.append(m)
        idxs.append(idx.astype(jnp.int32))
        sels.append(sel)
        work = jnp.where(sel, NEG_INF, work)
    exps = [jnp.exp(v - vals[0]) for v in vals]
    denom = exps[0] + exps[1] + exps[2] + exps[3]
    chosen = sels[0] | sels[1] | sels[2] | sels[3]
    onehot = jnp.where(chosen, 1.0, 0.0).astype(BF16)
    r_i = lax.broadcasted_iota(jnp.int32, (tr, tr), 0)
    c_i = lax.broadcasted_iota(jnp.int32, (tr, tr), 1)
    tri = jnp.where(c_i < r_i, 1.0, 0.0).astype(BF16)
    before = jnp.dot(tri, onehot, preferred_element_type=F32)
    out = jnp.zeros((tr, LANES), jnp.int32)
    for k in range(TOP_K):
        rank = jnp.sum(jnp.where(sels[k], before, 0.0), axis=-1, keepdims=True).astype(jnp.int32)
        wgt = lax.bitcast_convert_type(exps[k] / denom, jnp.int32)
        out = jnp.where(lane == k, idxs[k], out)
        out = jnp.where(lane == TOP_K + k, rank, out)
        out = jnp.where(lane == 2 * TOP_K + k, wgt, out)
    route_ref[...] = out
    counts = jnp.sum(onehot.astype(F32), axis=0, keepdims=True).astype(jnp.int32)
    cnt_ref[0] = jnp.broadcast_to(counts, (8, LANES))


def _router(x2, mod, w_r, b_r, seq):
    t, d = x2.shape
    tr = SEQ_TILE
    per_b = seq // tr
    return pl.pallas_call(
        _router_kernel,
        grid=(t // tr,),
        in_specs=[pl.BlockSpec((tr, d), lambda i: (i, 0)),
                  pl.BlockSpec((1, 3, d), lambda i: (i // per_b, 0, 0)),
                  pl.BlockSpec((d, 2 * LANES), lambda i: (0, 0)),
                  pl.BlockSpec((1, LANES), lambda i: (0, 0))],
        out_specs=[pl.BlockSpec((tr, d), lambda i: (i, 0)),
                   pl.BlockSpec((tr, LANES), lambda i: (i, 0)),
                   pl.BlockSpec((1, 8, LANES), lambda i: (i, 0, 0))],
        out_shape=[jax.ShapeDtypeStruct((t, d), BF16),
                   jax.ShapeDtypeStruct((t, LANES), jnp.int32),
                   jax.ShapeDtypeStruct((t // tr, 8, LANES), jnp.int32)],
        compiler_params=_params("arbitrary"),
        name="moe_router",
    )(x2, mod, w_r, b_r)


def _dispatch_kernel(meta_ref, prev_ref, offdst_ref, route_ref, h_ref, xs_hbm, dest_ref, xs_buf, sem, *, n_exp):
    tr, d = h_ref.shape
    n_rows = xs_buf.shape[1] - RUN_COPY
    i = pl.program_id(0)
    buf = i % 2

    @pl.when(i == 0)
    def _():
        xs_buf[:, n_rows:, :] = jnp.zeros((2, RUN_COPY, d // 2), jnp.int32)

    route = route_ref[...]
    lane = lax.broadcasted_iota(jnp.int32, (tr, LANES), 1)
    off_row = offdst_ref[0, 0:1, :]
    pos = jnp.full((tr, LANES), -1.0, F32)
    dest = jnp.zeros((tr, LANES), jnp.int32)
    for k in range(TOP_K):
        sel = lane == route[:, k:k + 1]
        rank = route[:, TOP_K + k:TOP_K + k + 1].astype(F32)
        pos_k = jnp.sum(jnp.where(sel, off_row, 0.0), axis=-1, keepdims=True) + rank
        pos = jnp.where(lane == k, pos_k, pos)
        dest = jnp.where(lane == k, pos_k.astype(jnp.int32), dest)
    dest_ref[...] = dest
    pos_t = pos.T
    h = h_ref[...]
    chunk = n_rows // 4
    for c in range(4):
        slot = (lax.broadcasted_iota(jnp.int32, (chunk, tr), 0) + chunk * c).astype(F32)
        hit = slot == pos_t[0:1, :]
        for k in range(1, TOP_K):
            hit = hit | (slot == pos_t[k:k + 1, :])
        onehot = jnp.where(hit, 1.0, 0.0).astype(BF16)
        rows = jnp.dot(onehot, h, preferred_element_type=F32)
        xs_buf[buf, chunk * c:chunk * (c + 1), :] = _pack_bf16_pairs(rows)

    def wait_copies(count):
        def wait_one(n, carry):
            pltpu.make_async_copy(xs_buf.at[0, pl.ds(0, RUN_COPY)], xs_hbm.at[pl.ds(0, RUN_COPY)], sem).wait()
            return carry
        lax.fori_loop(0, count, wait_one, 0)

    @pl.when(i > 0)
    def _():
        wait_copies(prev_ref[0, 0, 3 * LANES // 4])

    for e in range(n_exp):
        src = pl.multiple_of(meta_ref[0, 0, e], RUN_ALIGN)
        dst = pl.multiple_of(meta_ref[0, 0, LANES // 4 + e], RUN_ALIGN)
        length = meta_ref[0, 0, LANES // 2 + e]
        for q in range(tr // RUN_COPY):
            @pl.when(length > RUN_COPY * q)
            def _():
                pltpu.make_async_copy(xs_buf.at[buf, pl.ds(src + RUN_COPY * q, RUN_COPY)],
                                      xs_hbm.at[pl.ds(dst + RUN_COPY * q, RUN_COPY)], sem).start()

    @pl.when(i == pl.num_programs(0) - 1)
    def _():
        wait_copies(meta_ref[0, 0, 3 * LANES // 4])


def _dispatch(meta, offdst, route, h, n_slots, n_exp):
    t, d = h.shape
    tr = SEQ_TILE
    n_rows = TOP_K * tr + n_exp * RUN_ALIGN
    return pl.pallas_call(
        functools.partial(_dispatch_kernel, n_exp=n_exp),
        grid=(t // tr,),
        in_specs=[pl.BlockSpec((1, 1, LANES), lambda i: (i, 0, 0), memory_space=pltpu.SMEM),
                  pl.BlockSpec((1, 1, LANES), lambda i: (jnp.maximum(i - 1, 0), 0, 0), memory_space=pltpu.SMEM),
                  pl.BlockSpec((1, 8, LANES), lambda i: (i, 0, 0)),
                  pl.BlockSpec((tr, LANES), lambda i: (i, 0)),
                  pl.BlockSpec((tr, d), lambda i: (i, 0))],
        out_specs=[pl.BlockSpec(memory_space=pl.ANY),
                   pl.BlockSpec((tr, LANES), lambda i: (i, 0))],
        out_shape=[jax.ShapeDtypeStruct((n_slots, d // 2), jnp.int32),
                   jax.ShapeDtypeStruct((t, LANES), jnp.int32)],
        scratch_shapes=[pltpu.VMEM((2, n_rows + RUN_COPY, d // 2), jnp.int32), pltpu.SemaphoreType.DMA(())],
        compiler_params=_params("arbitrary"),
        name="moe_dispatch",
    )(meta, meta, offdst, route, h)


def _moe_gemm_kernel(te_ref, valid_ref, xs_ref, wgu_ref, bgu_ref, wd_ref, bd_ref, y_ref, wgu_bf, wd_bf):
    half = xs_ref.shape[1]
    f = wd_bf.shape[0]
    i = pl.program_id(0)

    @pl.when((i == 0) | (te_ref[i] != te_ref[jnp.maximum(i - 1, 0)]))
    def _():
        wgu_bf[...] = wgu_ref[0, 0].astype(BF16)
        wd_bf[...] = wd_ref[0, 0].astype(BF16)

    @pl.when(valid_ref[i] == 1)
    def _():
        lo, hi = _unpack_bf16_pairs(xs_ref[...])
        gu = (jnp.dot(lo, wgu_bf[:half, :], preferred_element_type=F32)
              + jnp.dot(hi, wgu_bf[half:, :], preferred_element_type=F32) + bgu_ref[0, 0])
        g = jnp.minimum(gu[:, :f], SWIGLU_LIMIT)
        u = jnp.clip(gu[:, f:], -SWIGLU_LIMIT, SWIGLU_LIMIT)
        a = g * jax.nn.sigmoid(SWIGLU_ALPHA * g) * (u + 1.0)
        y = jnp.dot(a.astype(BF16), wd_bf[...], preferred_element_type=F32) + bd_ref[0, 0]
        y_ref[...] = _pack_bf16_pairs(y)

    @pl.when(valid_ref[i] == 0)
    def _():
        y_ref[...] = jnp.zeros_like(y_ref)


def _moe_gemm(layer, tile_expert, tile_valid, xs, w_gu, b_gu, w_down, b_down):
    n_tiles = tile_expert.shape[0]
    tm = MOE_TILE
    _, _, d, f2 = w_gu.shape
    f = f2 // 2
    grid_spec = pltpu.PrefetchScalarGridSpec(
        num_scalar_prefetch=2,
        grid=(n_tiles,),
        in_specs=[pl.BlockSpec((tm, d // 2), lambda i, te, tv: (i, 0)),
                  pl.BlockSpec((1, 1, d, f2), lambda i, te, tv: (layer, te[i], 0, 0)),
                  pl.BlockSpec((1, 1, 1, f2), lambda i, te, tv: (layer, te[i], 0, 0)),
                  pl.BlockSpec((1, 1, f, d), lambda i, te, tv: (layer, te[i], 0, 0)),
                  pl.BlockSpec((1, 1, 1, d), lambda i, te, tv: (layer, te[i], 0, 0))],
        out_specs=pl.BlockSpec((tm, d // 2), lambda i, te, tv: (i, 0)),
        scratch_shapes=[pltpu.VMEM((d, f2), BF16), pltpu.VMEM((f, d), BF16)],
    )
    return pl.pallas_call(
        _moe_gemm_kernel,
        grid_spec=grid_spec,
        out_shape=jax.ShapeDtypeStruct((n_tiles * tm, d // 2), jnp.int32),
        compiler_params=_params("arbitrary"),
        name="moe_experts",
    )(tile_expert, tile_valid, xs, w_gu, b_gu, w_down, b_down)


def _combine_kernel(meta_ref, y_hbm, pos_ref, x_ref, mod_ref, route_ref, g_ref, b_ref, o_ref, ybuf, sem,
                    *, alpha, n_exp):
    tr = x_ref.shape[0]
    n_rows = ybuf.shape[0]
    sizes = [RUN_ALIGN << b for b in range(RUN_BITS)]

    @pl.when(pl.program_id(0) == 0)
    def _():
        ybuf[...] = jnp.zeros_like(ybuf)

    for e in range(n_exp):
        off = pl.multiple_of(meta_ref[0, 0, e], RUN_ALIGN)
        src = pl.multiple_of(meta_ref[0, 0, LANES // 4 + e], RUN_ALIGN)
        length = meta_ref[0, 0, LANES // 2 + e]
        for size in reversed(sizes):
            done = pl.multiple_of((length // (2 * size)) * (2 * size), RUN_ALIGN)

            @pl.when((length & size) != 0)
            def _():
                pltpu.make_async_copy(y_hbm.at[pl.ds(src + done, size)],
                                      ybuf.at[pl.ds(off + done, size)], sem).start()

    pos = pos_ref[...]
    wts = lax.bitcast_convert_type(route_ref[...], F32)
    n_chunks = 3
    width = n_rows // n_chunks
    spread = []
    for c in range(n_chunks):
        slot = lax.broadcasted_iota(jnp.int32, (tr, width), 1) + width * c
        w_c = jnp.zeros((tr, width), F32)
        for k in range(TOP_K):
            w_c = jnp.where(slot == pos[:, k:k + 1], wts[:, 2 * TOP_K + k:2 * TOP_K + k + 1], w_c)
        spread.append(w_c.astype(BF16))

    for b, size in enumerate(sizes):
        def wait_one(n, carry, size=size):
            pltpu.make_async_copy(y_hbm.at[pl.ds(0, size)], ybuf.at[pl.ds(0, size)], sem).wait()
            return carry
        lax.fori_loop(0, meta_ref[0, 0, 3 * LANES // 4 + 1 + b], wait_one, 0)

    sub_lo = sub_hi = None
    for c in range(n_chunks):
        lo, hi = _unpack_bf16_pairs(ybuf[width * c:width * (c + 1), :])
        part_lo = jnp.dot(spread[c], lo, preferred_element_type=F32)
        part_hi = jnp.dot(spread[c], hi, preferred_element_type=F32)
        sub_lo = part_lo if sub_lo is None else sub_lo + part_lo
        sub_hi = part_hi if sub_hi is None else sub_hi + part_hi
    sub = jnp.concatenate([sub_lo, sub_hi], axis=1)
    x = x_ref[...]
    r = alpha * x + (1.0 + mod_ref[0, 2:3, :]) * sub
    o_ref[...] = _layer_norm(r, g_ref[...], b_ref[...])


def _combine(meta, y, pos, x2, mod, route, ln_g, ln_b, alpha, seq, n_exp):
    t, d = x2.shape
    tr = SEQ_TILE
    per_b = seq // tr
    n_rows = TOP_K * tr + n_exp * RUN_ALIGN
    kern = functools.partial(_combine_kernel, alpha=alpha, n_exp=n_exp)
    return pl.pallas_call(
        kern,
        grid=(t // tr,),
        in_specs=[pl.BlockSpec((1, 1, LANES), lambda i: (i, 0, 0), memory_space=pltpu.SMEM),
                  pl.BlockSpec(memory_space=pl.ANY),
                  pl.BlockSpec((tr, LANES), lambda i: (i, 0)),
                  pl.BlockSpec((tr, d), lambda i: (i, 0)),
                  pl.BlockSpec((1, 3, d), lambda i: (i // per_b, 0, 0)),
                  pl.BlockSpec((tr, LANES), lambda i: (i, 0)),
                  pl.BlockSpec((1, d), lambda i: (0, 0)),
                  pl.BlockSpec((1, d), lambda i: (0, 0))],
        out_specs=pl.BlockSpec((tr, d), lambda i: (i, 0)),
        out_shape=jax.ShapeDtypeStruct((t, d), F32),
        scratch_shapes=[pltpu.VMEM((n_rows, d // 2), jnp.int32), pltpu.SemaphoreType.DMA(())],
        compiler_params=_params("arbitrary"),
        name="moe_combine",
    )(meta, y, pos, x2, mod, route, ln_g, ln_b)


def _moe_layer(layer, x2, mod, w_r, b_r, w_gu, b_gu, w_down, b_down, ln_g, ln_b, alpha, seq):
    t, d = x2.shape
    n_exp = w_gu.shape[1]
    tm = MOE_TILE
    n_tok_tiles = t // SEQ_TILE
    assert n_exp <= LANES // 4 and SEQ_TILE % RUN_COPY == 0
    h, route, cnt = _router(x2, mod, w_r, b_r, seq)
    runs = (cnt[:, 0, :n_exp] + RUN_ALIGN - 1) // RUN_ALIGN * RUN_ALIGN
    run_off = jnp.cumsum(runs, axis=1) - runs
    run_base = jnp.cumsum(runs, axis=0) - runs
    rows = jnp.sum(runs, axis=0)
    region = (rows + RUN_COPY + tm - 1) // tm * tm
    region_end = jnp.cumsum(region)
    region_start = region_end - region
    run_dst = region_start[None, :] + run_base
    n_tiles = -(-(t * TOP_K + n_tok_tiles * n_exp * (RUN_ALIGN - 1) + n_exp * (RUN_COPY + tm - 1)) // tm)
    tile_start = jnp.arange(n_tiles, dtype=jnp.int32) * tm
    tile_expert = jnp.minimum(jnp.sum(region_end[None, :] <= tile_start[:, None], axis=1),
                              n_exp - 1).astype(jnp.int32)
    tile_valid = ((tile_start - region_start[tile_expert] < rows[tile_expert])
                  & (tile_start < region_end[-1])).astype(jnp.int32)
    pad = jnp.zeros((n_tok_tiles, LANES // 4 - n_exp), jnp.int32)
    n_copies = jnp.sum((runs + RUN_COPY - 1) // RUN_COPY, axis=1, keepdims=True)
    pieces = jnp.stack([jnp.sum((runs // (RUN_ALIGN << b)) % 2, axis=1) for b in range(RUN_BITS)], axis=1)
    meta = jnp.concatenate([run_off, pad, run_dst, pad, runs, pad, n_copies, pieces,
                            jnp.zeros((n_tok_tiles, LANES // 4 - 1 - RUN_BITS), jnp.int32)],
                           axis=1).astype(jnp.int32)[:, None, :]
    wide = lambda v: jnp.pad(v.astype(F32), ((0, 0), (0, LANES - n_exp)))
    offdst = jnp.concatenate([wide(run_off)[:, None], wide(run_dst)[:, None],
                              jnp.zeros((n_tok_tiles, 6, LANES), F32)], axis=1)
    xs, pos = _dispatch(meta, offdst, route, h, n_tiles * tm, n_exp)
    y = _moe_gemm(layer, tile_expert, tile_valid, xs, w_gu, b_gu, w_down, b_down)
    return _combine(meta, y, pos, x2, mod, route, ln_g, ln_b, alpha, seq, n_exp)


def _split3(v):
    p0 = v.astype(BF16)
    r1 = v - p0.astype(F32)
    p1 = r1.astype(BF16)
    p2 = (r1 - p1.astype(F32)).astype(BF16)
    return p0, p1, p2


AUG_STRIDE = 8
AUG_PARTS = 3


def _aug_constants(d):
    n_pairs = d // LANES
    sel = np.zeros((AUG_PARTS, LANES, d), np.float32)
    ones = np.zeros((1, d), np.float32)
    for p in range(n_pairs):
        for hd in range(HEAD_PAIR):
            base = LANES * p + AUG_STRIDE * hd
            for part in range(AUG_PARTS):
                sel[part, HEAD_PAIR * p + hd, base + part] = 1.0
                ones[0, base + AUG_PARTS + part] = 1.0
    return jnp.asarray(sel, BF16), jnp.asarray(ones, F32)


def _kv_kernel(x_ref, mod_ref, wk_ref, wv_ref, wf_ref, bf_ref, sel_ref, ones_ref,
               k_ref, kaug_ref, vt_ref, cum_ref, carry_ref):
    ts = x_ref.shape[1]
    n_heads = cum_ref.shape[2]
    n_pairs = kaug_ref.shape[1]

    @pl.when(pl.program_id(1) == 0)
    def _():
        carry_ref[...] = jnp.zeros_like(carry_ref)

    h = _modulate(x_ref[0], mod_ref).astype(BF16)
    k_ref[0] = jnp.dot(h, wk_ref[...], preferred_element_type=F32).astype(BF16)
    vt = jnp.dot(h, wv_ref[...], preferred_element_type=F32).T.astype(BF16)
    tk = vt_ref.shape[4]
    for p in range(n_pairs):
        for c in range(ts // tk):
            vt_ref[0, p, c] = vt[LANES * p:LANES * (p + 1), tk * c:tk * (c + 1)]
    fz = jnp.dot(h, wf_ref[...], preferred_element_type=F32) + bf_ref[...]
    log_f = jnp.minimum(fz, 0.0) - jnp.log1p(jnp.exp(-jnp.abs(fz)))
    r_i = lax.broadcasted_iota(jnp.int32, (ts, ts), 0)
    c_i = lax.broadcasted_iota(jnp.int32, (ts, ts), 1)
    tri = jnp.where(c_i <= r_i, 1.0, 0.0).astype(BF16)
    cum = carry_ref[0:1, :]
    for part in _split3(log_f):
        cum = cum + jnp.dot(tri, part, preferred_element_type=F32)
    carry_ref[0:1, :] = cum[ts - 1:ts, :]
    cum = cum * LOG2E
    cum_ref[0] = cum[:, :n_heads]
    aug = ones_ref[...]
    for i, part in enumerate(_split3(cum)):
        aug = aug + jnp.dot(part, sel_ref[i], preferred_element_type=F32)
    aug = aug.astype(BF16)
    for p in range(n_pairs):
        kaug_ref[0, p] = aug[:, LANES * p:LANES * (p + 1)]


def _shared_kv(x, mod, w_k, w_v, w_f, b_f):
    bsz, seq, d = x.shape
    ts = ATTN_TILE
    tk = ATTN_KEY_TILE
    n_pairs = d // LANES
    sel, ones = _aug_constants(d)
    return pl.pallas_call(
        _kv_kernel,
        grid=(bsz, seq // ts),
        in_specs=[pl.BlockSpec((1, ts, d), lambda b, j: (b, j, 0)),
                  pl.BlockSpec((1, 2, d), lambda b, j: (b, 0, 0)),
                  pl.BlockSpec((d, d), lambda b, j: (0, 0)),
                  pl.BlockSpec((d, d), lambda b, j: (0, 0)),
                  pl.BlockSpec((d, LANES), lambda b, j: (0, 0)),
                  pl.BlockSpec((1, LANES), lambda b, j: (0, 0)),
                  pl.BlockSpec((AUG_PARTS, LANES, d), lambda b, j: (0, 0, 0)),
                  pl.BlockSpec((1, d), lambda b, j: (0, 0))],
        out_specs=[pl.BlockSpec((1, ts, d), lambda b, j: (b, j, 0)),
                   pl.BlockSpec((1, n_pairs, ts, LANES), lambda b, j: (b, 0, j, 0)),
                   pl.BlockSpec((1, n_pairs, ts // tk, LANES, tk), lambda b, j: (b, 0, j, 0, 0)),
                   pl.BlockSpec((1, ts, N_HEADS), lambda b, j: (b, j, 0))],
        out_shape=[jax.ShapeDtypeStruct((bsz, seq, d), BF16),
                   jax.ShapeDtypeStruct((bsz, n_pairs, seq, LANES), BF16),
                   jax.ShapeDtypeStruct((bsz, n_pairs, seq // tk, LANES, tk), BF16),
                   jax.ShapeDtypeStruct((bsz, seq, N_HEADS), F32)],
        scratch_shapes=[pltpu.VMEM((8, LANES), F32)],
        compiler_params=_params("arbitrary", "arbitrary"),
        name="shared_kv",
    )(x, mod, w_k, w_v, w_f, b_f, sel, ones)


def _q_proj_kernel(x_ref, mod_ref, wq_ref, q_ref, *, scale):
    h = _modulate(x_ref[0], mod_ref).astype(BF16)
    q_ref[0] = (jnp.dot(h, wq_ref[...], preferred_element_type=F32) * scale).astype(BF16)


def _q_proj(x, mod, w_q, scale):
    bsz, seq, d = x.shape
    ts = SEQ_TILE
    return pl.pallas_call(
        functools.partial(_q_proj_kernel, scale=scale),
        grid=(bsz, seq // ts),
        in_specs=[pl.BlockSpec((1, ts, d), lambda b, j: (b, j, 0)),
                  pl.BlockSpec((1, 3, d), lambda b, j: (b, 0, 0)),
                  pl.BlockSpec((d, d), lambda b, j: (0, 0))],
        out_specs=pl.BlockSpec((1, ts, d), lambda b, j: (b, j, 0)),
        out_shape=jax.ShapeDtypeStruct((bsz, seq, d), BF16),
        compiler_params=_params("arbitrary", "arbitrary"),
        name="q_proj",
    )(x, mod, w_q)


def _attn_kernel(q_ref, k_ref, kaug_ref, vt_ref, cq_ref, o_ref, acc_ref, m_ref, l_ref, s_ref):
    tq = q_ref.shape[1]
    tk = vt_ref.shape[4]
    per_q = tq // tk
    head_dim = LANES // HEAD_PAIR
    i = pl.program_id(2)
    q_t = q_ref[0].astype(F32).T
    row = lax.broadcasted_iota(jnp.int32, (LANES, tq), 0)
    rhs = []
    for g in range(PAIR_GROUP):
        q_pair = q_t[LANES * g:LANES * (g + 1), :]
        for hd in range(HEAD_PAIR):
            own = (row >= head_dim * hd) & (row < head_dim * (hd + 1))
            parts = _split3(cq_ref[0, g, hd:hd + 1, :])
            base = AUG_STRIDE * hd
            aug = jnp.where((row >= base) & (row < base + AUG_PARTS), -1.0, 0.0)
            for n, part in enumerate(parts):
                aug = jnp.where(row == base + AUG_PARTS + n, part.astype(F32), aug)
            rhs.append(jnp.concatenate([jnp.where(own, q_pair, 0.0).astype(BF16), aug.astype(BF16)],
                                       axis=0))
    ones_rows = jnp.ones((16, tk), BF16)
    acc_ref[...] = jnp.zeros_like(acc_ref)
    m_ref[...] = jnp.full_like(m_ref, NEG_INF)
    l_ref[...] = jnp.zeros_like(l_ref)

    def keys_of(j, g):
        start = pl.multiple_of(j * tk, tk)
        return jnp.concatenate([k_ref[0, pl.ds(start, tk), LANES * g:LANES * (g + 1)],
                                kaug_ref[0, g, pl.ds(start, tk), :]], axis=1)

    s_ref[...] = jnp.dot(keys_of(0, 0), rhs[0], preferred_element_type=F32)

    def chunk(j, diagonal, offset=0, has_next=True):
        keys = [keys_of(j, g) for g in range(PAIR_GROUP)]

        def masked(s_t):
            if diagonal:
                k_i = lax.broadcasted_iota(jnp.int32, (tk, tq), 0) + offset
                q_i = lax.broadcasted_iota(jnp.int32, (tk, tq), 1)
                s_t = jnp.where(k_i <= q_i, s_t, NEG_INF)
            return s_t

        n_heads = PAIR_GROUP * HEAD_PAIR
        s_next = s_ref[...]
        for n in range(n_heads):
            g, hd = divmod(n, HEAD_PAIR)
            s_t = masked(s_next)
            if n + 1 < n_heads:
                s_next = jnp.dot(keys[(n + 1) // HEAD_PAIR], rhs[n + 1], preferred_element_type=F32)
            elif has_next:
                s_ref[...] = jnp.dot(keys_of(j + 1, 0), rhs[0], preferred_element_type=F32)
            m_old = m_ref[n:n + 1, :]
            m_new = jnp.maximum(m_old, jnp.max(s_t, axis=0, keepdims=True))
            a = jnp.exp2(m_old - m_new)
            p_t = jnp.exp2(s_t - m_new).astype(BF16)
            m_ref[n:n + 1, :] = m_new
            rows = slice(head_dim * n, head_dim * (n + 1))
            lhs = jnp.concatenate([vt_ref[0, g, j, head_dim * hd:head_dim * (hd + 1), :], ones_rows], axis=0)
            pv = jnp.dot(lhs, p_t, preferred_element_type=F32)
            l_ref[n:n + 1, :] = a * l_ref[n:n + 1, :] + pv[head_dim:head_dim + 1, :]
            acc_ref[rows, :] = acc_ref[rows, :] * a + pv[:head_dim, :]

    def body(j, carry):
        chunk(j, False)
        return carry
    lax.fori_loop(0, i * per_q, body, 0)
    for sub in range(per_q):
        chunk(i * per_q + sub, True, sub * tk, has_next=sub + 1 < per_q)
    inv = jnp.concatenate([jnp.broadcast_to(1.0 / l_ref[n:n + 1, :], (head_dim, tq))
                           for n in range(PAIR_GROUP * HEAD_PAIR)], axis=0)
    o_ref[0] = (acc_ref[...] * inv).T.astype(o_ref.dtype)


def _attention(q, k, kaug, v_t, cq):
    bsz, seq, d = q.shape
    width = PAIR_GROUP * LANES
    tq = ATTN_TILE
    return pl.pallas_call(
        _attn_kernel,
        grid=(bsz, d // width, seq // tq),
        in_specs=[pl.BlockSpec((1, tq, width), lambda b, p, i: (b, i, p)),
                  pl.BlockSpec((1, seq, width), lambda b, p, i: (b, 0, p)),
                  pl.BlockSpec((1, PAIR_GROUP, seq, LANES), lambda b, p, i: (b, p, 0, 0)),
                  pl.BlockSpec((1, PAIR_GROUP) + v_t.shape[2:], lambda b, p, i: (b, p, 0, 0, 0)),
                  pl.BlockSpec((1, PAIR_GROUP, HEAD_PAIR, tq), lambda b, p, i: (b, p, 0, i))],
        out_specs=pl.BlockSpec((1, tq, width), lambda b, p, i: (b, i, p)),
        out_shape=jax.ShapeDtypeStruct((bsz, seq, d), BF16),
        scratch_shapes=[pltpu.VMEM((width, tq), F32),
                        pltpu.VMEM((8, tq), F32),
                        pltpu.VMEM((8, tq), F32),
                        pltpu.VMEM((v_t.shape[4], tq), F32)],
        compiler_params=_params("arbitrary", "arbitrary", "arbitrary"),
        name="fox_attention",
    )(q, k, kaug, v_t, cq)


def _out_proj_kernel(o_ref, x_ref, mod_ref, wo_ref, g_ref, b_ref, out_ref, *, alpha):
    y = jnp.dot(o_ref[0], wo_ref[...], preferred_element_type=F32)
    r = alpha * x_ref[0] + (1.0 + mod_ref[0, 2:3, :]) * y
    out_ref[0] = _layer_norm(r, g_ref[...], b_ref[...])


def _out_proj(o, x, mod, w_o, ln_g, ln_b, alpha):
    bsz, seq, d = x.shape
    ts = SEQ_TILE
    return pl.pallas_call(
        functools.partial(_out_proj_kernel, alpha=alpha),
        grid=(bsz, seq // ts),
        in_specs=[pl.BlockSpec((1, ts, d), lambda b, j: (b, j, 0)),
                  pl.BlockSpec((1, ts, d), lambda b, j: (b, j, 0)),
                  pl.BlockSpec((1, 3, d), lambda b, j: (b, 0, 0)),
                  pl.BlockSpec((d, d), lambda b, j: (0, 0)),
                  pl.BlockSpec((1, d), lambda b, j: (0, 0)),
                  pl.BlockSpec((1, d), lambda b, j: (0, 0))],
        out_specs=pl.BlockSpec((1, ts, d), lambda b, j: (b, j, 0)),
        out_shape=jax.ShapeDtypeStruct(x.shape, F32),
        compiler_params=_params("arbitrary", "arbitrary"),
        name="attn_out_proj",
    )(o, x, mod, w_o, ln_g, ln_b)


def kernel(x, c, conv_w_in, conv_w, conv_w_out, kv_ada_w, kv_ada_b, w_kvf, b_f, attn_w_q, attn_w_o,
           ada_w, ada_b, ln_g, ln_b, router_w, router_b, exp_w_gu, exp_b_gu, exp_w_down, exp_b_down):
    bsz, seq, d = x.shape
    depth = ada_w.shape[0]
    n_conv = conv_w_in.shape[0]
    n_exp = router_w.shape[-1]
    alpha = (2.0 * depth) ** 0.25
    head_dim = d // N_HEADS
    assert head_dim * HEAD_PAIR == LANES and seq % SEQ_TILE == 0 and seq % ATTN_TILE == 0

    c_pad = jnp.pad(c, ((0, 8 - bsz), (0, 0)))
    mods = _ada_params(c_pad, ada_w.reshape(depth * 2, d, 3 * d), ada_b.reshape(depth * 2, 1, 3 * d))
    mods = mods[:, :bsz, :].reshape(depth, 2, bsz, 3, d)
    kv_mod = _ada_params(c_pad, kv_ada_w[None], kv_ada_b[None, None])[0, :bsz].reshape(bsz, 2, d)

    w_r = jnp.pad(router_w, ((0, 0), (0, 0), (0, LANES - n_exp)))
    w_r_hi = w_r.astype(BF16)
    w_r = jnp.concatenate([w_r_hi, (w_r - w_r_hi.astype(F32)).astype(BF16)], axis=-1)
    b_r = jnp.pad(router_b, ((0, 0), (0, LANES - n_exp)), constant_values=-1e30)[:, None, :]
    b_gu = exp_b_gu[:, :, None, :]
    b_dn = exp_b_down[:, :, None, :]
    k = kaug = v_t = cq = None
    for l in range(depth):
        g0, b0 = ln_g[l, 0][None], ln_b[l, 0][None]
        if l < n_conv:
            x = _conv_layer(x, mods[l, 0], conv_w_in[l].astype(BF16), conv_w[l],
                            conv_w_out[l].astype(BF16), g0, b0, alpha)
        else:
            j = l - n_conv
            q = _q_proj(x, mods[l, 0], attn_w_q[j].astype(BF16), head_dim ** -0.5 * LOG2E)
            o = _attention(q, k, kaug, v_t, cq)
            x = _out_proj(o, x, mods[l, 0], attn_w_o[j].astype(BF16), g0, b0, alpha)
        x = _moe_layer(l, x.reshape(bsz * seq, d), mods[l, 1], w_r[l], b_r[l], exp_w_gu, b_gu,
                       exp_w_down, b_dn,
                       ln_g[l, 1][None], ln_b[l, 1][None], alpha, seq).reshape(bsz, seq, d)
        if l == n_conv - 1:
            w_f = jnp.pad(w_kvf[:, 2 * d:], ((0, 0), (0, LANES - N_HEADS))).astype(BF16)
            bias_f = jnp.pad(b_f, (0, LANES - N_HEADS))[None]
            k, kaug, v_t, cum = _shared_kv(x, kv_mod, w_kvf[:, :d].astype(BF16),
                                           w_kvf[:, d:2 * d].astype(BF16), w_f, bias_f)
            cq = cum.reshape(bsz, seq, N_HEADS // HEAD_PAIR, HEAD_PAIR).transpose(0, 2, 3, 1)
    return x
```

```python
import functools

import numpy as np
import jax
import jax.numpy as jnp
from jax import lax
from jax.experimental import pallas as pl
from jax.experimental.pallas import tpu as pltpu

N_HEADS = 16
TOP_K = 4
SWIGLU_LIMIT = 7.0
SWIGLU_ALPHA = 1.702
LN_EPS = 1e-5
LANES = 128
HEAD_PAIR = 2
PAIR_GROUP = 2
LOG2E = 1.4426950408889634
SEQ_TILE = 512
MOE_TILE = 512
ATTN_TILE = 512
ATTN_KEY_TILE = 512
RUN_ALIGN = 8
RUN_COPY = 128
RUN_BITS = 7
VMEM_LIMIT = 56 * 1024 * 1024

F32 = jnp.float32
BF16 = jnp.bfloat16
NEG_INF = float("-inf")


def _params(*sem):
    return pltpu.CompilerParams(dimension_semantics=sem, vmem_limit_bytes=VMEM_LIMIT)


def _layer_norm(r, g, b):
    mu = jnp.mean(r, axis=-1, keepdims=True)
    d = r - mu
    var = jnp.mean(d * d, axis=-1, keepdims=True)
    return d * lax.rsqrt(var + LN_EPS) * g + b


def _modulate(x, mod_ref):
    return x * (1.0 + mod_ref[0, 1:2, :]) + mod_ref[0, 0:1, :]


def _ada_kernel(c_ref, w_ref, b_ref, o_ref):
    c = c_ref[...]
    cond = c * jax.nn.sigmoid(c)
    o_ref[0] = jnp.dot(cond, w_ref[0], precision=lax.Precision.HIGHEST,
                       preferred_element_type=F32) + b_ref[0]


def _ada_params(c_pad, w, b):
    g, d, n = w.shape
    tn = 1024 if n % 1024 == 0 else n
    return pl.pallas_call(
        _ada_kernel,
        grid=(g, n // tn),
        in_specs=[pl.BlockSpec((8, d), lambda i, j: (0, 0)),
                  pl.BlockSpec((1, d, tn), lambda i, j: (i, 0, j)),
                  pl.BlockSpec((1, 1, tn), lambda i, j: (i, 0, j))],
        out_specs=pl.BlockSpec((1, 8, tn), lambda i, j: (i, 0, j)),
        out_shape=jax.ShapeDtypeStruct((g, 8, n), F32),
        compiler_params=_params("arbitrary", "arbitrary"),
        name="ada_params",
    )(c_pad, w, b)


def _conv_layer_kernel(x_ref, mod_ref, win_ref, wc_ref, wout_ref, g_ref, b_ref, o_ref,
                       carry_ref, a_ref, *, alpha, col_chunk):
    ts, d = x_ref.shape[1], x_ref.shape[2]

    @pl.when(pl.program_id(1) == 0)
    def _():
        carry_ref[...] = jnp.zeros_like(carry_ref)

    x = x_ref[0]
    h = _modulate(x, mod_ref).astype(BF16)
    row = lax.broadcasted_iota(jnp.int32, (ts, col_chunk), 0)
    for c in range(0, d, col_chunk):
        gate_c = jnp.dot(h, win_ref[:, c:c + col_chunk], preferred_element_type=F32)
        u = jnp.dot(h, win_ref[:, 2 * d + c:2 * d + c + col_chunk], preferred_element_type=F32)
        z = gate_c * u
        z_m1 = carry_ref[1:2, c:c + col_chunk]
        z_m2 = carry_ref[0:1, c:c + col_chunk]
        z1 = jnp.where(row == 0, z_m1, pltpu.roll(z, 1, 0))
        z2 = jnp.where(row == 0, z_m2, jnp.where(row == 1, z_m1, pltpu.roll(z, 2, 0)))
        carry_ref[0:2, c:c + col_chunk] = z[ts - 2:ts, :]
        conv = (wc_ref[0:1, c:c + col_chunk] * z2 + wc_ref[1:2, c:c + col_chunk] * z1
                + wc_ref[2:3, c:c + col_chunk] * z)
        gate_b = jnp.dot(h, win_ref[:, d + c:d + c + col_chunk], preferred_element_type=F32)
        a_ref[:, c:c + col_chunk] = (gate_b * conv).astype(BF16)
    y = jnp.dot(a_ref[...], wout_ref[...], preferred_element_type=F32)
    r = alpha * x + (1.0 + mod_ref[0, 2:3, :]) * y
    o_ref[0] = _layer_norm(r, g_ref[...], b_ref[...])


def _conv_layer(x, mod, w_in, w_conv, w_out, ln_g, ln_b, alpha):
    bsz, seq, d = x.shape
    ts = SEQ_TILE
    kern = functools.partial(_conv_layer_kernel, alpha=alpha, col_chunk=256)
    return pl.pallas_call(
        kern,
        grid=(bsz, seq // ts),
        in_specs=[pl.BlockSpec((1, ts, d), lambda b, j: (b, j, 0)),
                  pl.BlockSpec((1, 3, d), lambda b, j: (b, 0, 0)),
                  pl.BlockSpec((d, 3 * d), lambda b, j: (0, 0)),
                  pl.BlockSpec((3, d), lambda b, j: (0, 0)),
                  pl.BlockSpec((d, d), lambda b, j: (0, 0)),
                  pl.BlockSpec((1, d), lambda b, j: (0, 0)),
                  pl.BlockSpec((1, d), lambda b, j: (0, 0))],
        out_specs=pl.BlockSpec((1, ts, d), lambda b, j: (b, j, 0)),
        out_shape=jax.ShapeDtypeStruct(x.shape, F32),
        scratch_shapes=[pltpu.VMEM((8, d), F32), pltpu.VMEM((ts, d), BF16)],
        compiler_params=_params("arbitrary", "arbitrary"),
        name="conv_layer",
    )(x, mod, w_in, w_conv, w_out, ln_g, ln_b)


def _pack_bf16_pairs(h):
    half = h.shape[1] // 2
    lo = lax.bitcast_convert_type(h[:, :half].astype(BF16).astype(F32), jnp.int32)
    hi = lax.bitcast_convert_type(h[:, half:].astype(BF16).astype(F32), jnp.int32)
    return lax.shift_right_logical(lo, 16) | hi


def _unpack_bf16_pairs(w):
    lo = lax.bitcast_convert_type(w << 16, F32).astype(BF16)
    hi = lax.bitcast_convert_type(w & jnp.int32(-65536), F32).astype(BF16)
    return lo, hi


def _router_kernel(x_ref, mod_ref, wr_ref, br_ref, h_ref, route_ref, cnt_ref):
    tr = x_ref.shape[0]
    h = _modulate(x_ref[...], mod_ref)
    h_hi = h.astype(BF16)
    h_ref[...] = h_hi
    h_lo = (h - h_hi.astype(F32)).astype(BF16)
    both = jnp.dot(h_hi, wr_ref[...], preferred_element_type=F32)
    logits = (both[:, :LANES] + both[:, LANES:]
              + jnp.dot(h_lo, wr_ref[:, :LANES], preferred_element_type=F32)) + br_ref[...]
    lane = lax.broadcasted_iota(jnp.int32, (tr, LANES), 1)
    lane_f = lane.astype(F32)
    work = logits
    vals, idxs, sels = [], [], []
    for _ in range(TOP_K):
        m = jnp.max(work, axis=-1, keepdims=True)
        idx = jnp.min(jnp.where(work == m, lane_f, float(LANES)), axis=-1, keepdims=True)
        sel = lane_f == idx
        vals.append(m)
        idxs.append(idx.astype(jnp.int32))
        sels.append(sel)
        work = jnp.where(sel, NEG_INF, work)
    exps = [jnp.exp(v - vals[0]) for v in vals]
    denom = exps[0] + exps[1] + exps[2] + exps[3]
    chosen = sels[0] | sels[1] | sels[2] | sels[3]
    onehot = jnp.where(chosen, 1.0, 0.0).astype(BF16)
    r_i = lax.broadcasted_iota(jnp.int32, (tr, tr), 0)
    c_i = lax.broadcasted_iota(jnp.int32, (tr, tr), 1)
    tri = jnp.where(c_i < r_i, 1.0, 0.0).astype(BF16)
    before = jnp.dot(tri, onehot, preferred_element_type=F32)
    out = jnp.zeros((tr, LANES), jnp.int32)
    for k in range(TOP_K):
        rank = jnp.sum(jnp.where(sels[k], before, 0.0), axis=-1, keepdims=True).astype(jnp.int32)
        wgt = lax.bitcast_convert_type(exps[k] / denom, jnp.int32)
        out = jnp.where(lane == k, idxs[k], out)
        out = jnp.where(lane == TOP_K + k, rank, out)
        out = jnp.where(lane == 2 * TOP_K + k, wgt, out)
    route_ref[...] = out
    counts = jnp.sum(onehot.astype(F32), axis=0, keepdims=True).astype(jnp.int32)
    cnt_ref[0] = jnp.broadcast_to(counts, (8, LANES))


def _router(x2, mod, w_r, b_r, seq):
    t, d = x2.shape
    tr = SEQ_TILE
    per_b = seq // tr
    return pl.pallas_call(
        _router_kernel,
        grid=(t // tr,),
        in_specs=[pl.BlockSpec((tr, d), lambda i: (i, 0)),
                  pl.BlockSpec((1, 3, d), lambda i: (i // per_b, 0, 0)),
                  pl.BlockSpec((d, 2 * LANES), lambda i: (0, 0)),
                  pl.BlockSpec((1, LANES), lambda i: (0, 0))],
        out_specs=[pl.BlockSpec((tr, d), lambda i: (i, 0)),
                   pl.BlockSpec((tr, LANES), lambda i: (i, 0)),
                   pl.BlockSpec((1, 8, LANES), lambda i: (i, 0, 0))],
        out_shape=[jax.ShapeDtypeStruct((t, d), BF16),
                   jax.ShapeDtypeStruct((t, LANES), jnp.int32),
                   jax.ShapeDtypeStruct((t // tr, 8, LANES), jnp.int32)],
        compiler_params=_params("arbitrary"),
        name="moe_router",
    )(x2, mod, w_r, b_r)


def _dispatch_kernel(meta_ref, prev_ref, fill_ref, offdst_ref, route_ref, h_ref, xs_hbm, dest_ref, xs_buf, sem,
                     *, n_exp):
    tr, d = h_ref.shape
    n_rows = xs_buf.shape[1] - RUN_COPY
    i = pl.program_id(0)
    buf = i % 2
    sizes = [RUN_ALIGN << b for b in range(RUN_BITS)]

    @pl.when(i == 0)
    def _():
        xs_buf[0, n_rows:, :] = jnp.zeros((RUN_COPY, d // 2), jnp.int32)
        xs_buf[1] = jnp.zeros(xs_buf.shape[1:], jnp.int32)
        for e in range(n_exp):
            start = pl.multiple_of(fill_ref[0, 0, e], RUN_ALIGN)
            length = fill_ref[0, 0, LANES // 4 + e]
            for size in reversed(sizes):
                done = pl.multiple_of((length // (2 * size)) * (2 * size), RUN_ALIGN)

                @pl.when((length & size) != 0)
                def _():
                    pltpu.make_async_copy(xs_buf.at[1, pl.ds(0, size)],
                                          xs_hbm.at[pl.ds(start + done, size)], sem).start()
        spare = pl.multiple_of(fill_ref[0, 0, LANES // 2], tr)

        def fill_spare(n, carry):
            pltpu.make_async_copy(xs_buf.at[1, pl.ds(0, tr)],
                                  xs_hbm.at[pl.ds(pl.multiple_of(spare + n * tr, tr), tr)], sem).start()
            return carry
        lax.fori_loop(0, fill_ref[0, 0, LANES // 2 + 1], fill_spare, 0)
        for b, size in enumerate(sizes):
            def wait_piece(n, carry, size=size):
                pltpu.make_async_copy(xs_buf.at[1, pl.ds(0, size)], xs_hbm.at[pl.ds(0, size)], sem).wait()
                return carry
            lax.fori_loop(0, fill_ref[0, 0, LANES // 2 + 2 + b], wait_piece, 0)

        def wait_spare(n, carry):
            pltpu.make_async_copy(xs_buf.at[1, pl.ds(0, tr)], xs_hbm.at[pl.ds(0, tr)], sem).wait()
            return carry
        lax.fori_loop(0, fill_ref[0, 0, LANES // 2 + 1], wait_spare, 0)

    route = route_ref[...]
    lane = lax.broadcasted_iota(jnp.int32, (tr, LANES), 1)
    off_row = offdst_ref[0, 0:1, :]
    pos = jnp.full((tr, LANES), -1.0, F32)
    dest = jnp.zeros((tr, LANES), jnp.int32)
    for k in range(TOP_K):
        sel = lane == route[:, k:k + 1]
        rank = route[:, TOP_K + k:TOP_K + k + 1].astype(F32)
        pos_k = jnp.sum(jnp.where(sel, off_row, 0.0), axis=-1, keepdims=True) + rank
        pos = jnp.where(lane == k, pos_k, pos)
        dest = jnp.where(lane == k, pos_k.astype(jnp.int32), dest)
    dest_ref[...] = dest
    pos_t = pos.T
    h = h_ref[...]
    chunk = n_rows // 4
    for c in range(4):
        slot = (lax.broadcasted_iota(jnp.int32, (chunk, tr), 0) + chunk * c).astype(F32)
        hit = slot == pos_t[0:1, :]
        for k in range(1, TOP_K):
            hit = hit | (slot == pos_t[k:k + 1, :])
        onehot = jnp.where(hit, 1.0, 0.0).astype(BF16)
        rows = jnp.dot(onehot, h, preferred_element_type=F32)
        xs_buf[buf, chunk * c:chunk * (c + 1), :] = _pack_bf16_pairs(rows)

    def wait_copies(count):
        def wait_one(n, carry):
            pltpu.make_async_copy(xs_buf.at[0, pl.ds(0, RUN_COPY)], xs_hbm.at[pl.ds(0, RUN_COPY)], sem).wait()
            return carry
        lax.fori_loop(0, count, wait_one, 0)

    @pl.when(i > 0)
    def _():
        wait_copies(prev_ref[0, 0, 3 * LANES // 4])

    for e in range(n_exp):
        src = pl.multiple_of(meta_ref[0, 0, e], RUN_ALIGN)
        dst = pl.multiple_of(meta_ref[0, 0, LANES // 4 + e], RUN_ALIGN)
        length = meta_ref[0, 0, LANES // 2 + e]
        for q in range(tr // RUN_COPY):
            @pl.when(length > RUN_COPY * q)
            def _():
                pltpu.make_async_copy(xs_buf.at[buf, pl.ds(src + RUN_COPY * q, RUN_COPY)],
                                      xs_hbm.at[pl.ds(dst + RUN_COPY * q, RUN_COPY)], sem).start()

    @pl.when(i == pl.num_programs(0) - 1)
    def _():
        wait_copies(meta_ref[0, 0, 3 * LANES // 4])


def _dispatch(meta, fill, offdst, route, h, n_slots, n_exp):
    t, d = h.shape
    tr = SEQ_TILE
    n_rows = TOP_K * tr + n_exp * RUN_ALIGN
    return pl.pallas_call(
        functools.partial(_dispatch_kernel, n_exp=n_exp),
        grid=(t // tr,),
        in_specs=[pl.BlockSpec((1, 1, LANES), lambda i: (i, 0, 0), memory_space=pltpu.SMEM),
                  pl.BlockSpec((1, 1, LANES), lambda i: (jnp.maximum(i - 1, 0), 0, 0), memory_space=pltpu.SMEM),
                  pl.BlockSpec((1, 1, LANES), lambda i: (0, 0, 0), memory_space=pltpu.SMEM),
                  pl.BlockSpec((1, 8, LANES), lambda i: (i, 0, 0)),
                  pl.BlockSpec((tr, LANES), lambda i: (i, 0)),
                  pl.BlockSpec((tr, d), lambda i: (i, 0))],
        out_specs=[pl.BlockSpec(memory_space=pl.ANY),
                   pl.BlockSpec((tr, LANES), lambda i: (i, 0))],
        out_shape=[jax.ShapeDtypeStruct((n_slots, d // 2), jnp.int32),
                   jax.ShapeDtypeStruct((t, LANES), jnp.int32)],
        scratch_shapes=[pltpu.VMEM((2, n_rows + RUN_COPY, d // 2), jnp.int32), pltpu.SemaphoreType.DMA(())],
        compiler_params=_params("arbitrary"),
        name="moe_dispatch",
    )(meta, meta, fill, offdst, route, h)


def _moe_gemm_kernel(te_ref, valid_ref, xs_ref, wgu_ref, bgu_ref, wd_ref, bd_ref, y_ref, wgu_bf, wd_bf):
    half = xs_ref.shape[1]
    f = wd_bf.shape[0]
    i = pl.program_id(0)

    @pl.when((i == 0) | (te_ref[i] != te_ref[jnp.maximum(i - 1, 0)]))
    def _():
        wgu_bf[...] = wgu_ref[0, 0].astype(BF16)
        wd_bf[...] = wd_ref[0, 0].astype(BF16)

    @pl.when(valid_ref[i] == 1)
    def _():
        lo, hi = _unpack_bf16_pairs(xs_ref[...])
        gu = (jnp.dot(lo, wgu_bf[:half, :], preferred_element_type=F32)
              + jnp.dot(hi, wgu_bf[half:, :], preferred_element_type=F32) + bgu_ref[0, 0])
        g = jnp.minimum(gu[:, :f], SWIGLU_LIMIT)
        u = jnp.clip(gu[:, f:], -SWIGLU_LIMIT, SWIGLU_LIMIT)
        a = g * jax.nn.sigmoid(SWIGLU_ALPHA * g) * (u + 1.0)
        y = jnp.dot(a.astype(BF16), wd_bf[...], preferred_element_type=F32) + bd_ref[0, 0]
        y_ref[...] = _pack_bf16_pairs(y)

    @pl.when(valid_ref[i] == 0)
    def _():
        y_ref[...] = jnp.zeros_like(y_ref)


def _moe_gemm(layer, tile_expert, tile_valid, xs, w_gu, b_gu, w_down, b_down):
    n_tiles = tile_expert.shape[0]
    tm = MOE_TILE
    _, _, d, f2 = w_gu.shape
    f = f2 // 2
    grid_spec = pltpu.PrefetchScalarGridSpec(
        num_scalar_prefetch=2,
        grid=(n_tiles,),
        in_specs=[pl.BlockSpec((tm, d // 2), lambda i, te, tv: (i, 0)),
                  pl.BlockSpec((1, 1, d, f2), lambda i, te, tv: (layer, te[i], 0, 0)),
                  pl.BlockSpec((1, 1, 1, f2), lambda i, te, tv: (layer, te[i], 0, 0)),
                  pl.BlockSpec((1, 1, f, d), lambda i, te, tv: (layer, te[i], 0, 0)),
                  pl.BlockSpec((1, 1, 1, d), lambda i, te, tv: (layer, te[i], 0, 0))],
        out_specs=pl.BlockSpec((tm, d // 2), lambda i, te, tv: (i, 0)),
        scratch_shapes=[pltpu.VMEM((d, f2), BF16), pltpu.VMEM((f, d), BF16)],
    )
    return pl.pallas_call(
        _moe_gemm_kernel,
        grid_spec=grid_spec,
        out_shape=jax.ShapeDtypeStruct((n_tiles * tm, d // 2), jnp.int32),
        compiler_params=_params("arbitrary"),
        name="moe_experts",
    )(tile_expert, tile_valid, xs, w_gu, b_gu, w_down, b_down)


def _combine_kernel(meta_ref, y_hbm, pos_ref, x_ref, mod_ref, route_ref, g_ref, b_ref, o_ref, ybuf, sem,
                    *, alpha, n_exp):
    tr = x_ref.shape[0]
    n_rows = ybuf.shape[0]
    sizes = [RUN_ALIGN << b for b in range(RUN_BITS)]

    @pl.when(pl.program_id(0) == 0)
    def _():
        ybuf[...] = jnp.zeros_like(ybuf)

    for e in range(n_exp):
        off = pl.multiple_of(meta_ref[0, 0, e], RUN_ALIGN)
        src = pl.multiple_of(meta_ref[0, 0, LANES // 4 + e], RUN_ALIGN)
        length = meta_ref[0, 0, LANES // 2 + e]
        for size in reversed(sizes):
            done = pl.multiple_of((length // (2 * size)) * (2 * size), RUN_ALIGN)

            @pl.when((length & size) != 0)
            def _():
                pltpu.make_async_copy(y_hbm.at[pl.ds(src + done, size)],
                                      ybuf.at[pl.ds(off + done, size)], sem).start()

    pos = pos_ref[...]
    wts = lax.bitcast_convert_type(route_ref[...], F32)
    n_chunks = 3
    width = n_rows // n_chunks
    spread = []
    for c in range(n_chunks):
        slot = lax.broadcasted_iota(jnp.int32, (tr, width), 1) + width * c
        w_c = jnp.zeros((tr, width), F32)
        for k in range(TOP_K):
            w_c = jnp.where(slot == pos[:, k:k + 1], wts[:, 2 * TOP_K + k:2 * TOP_K + k + 1], w_c)
        spread.append(w_c.astype(BF16))

    for b, size in enumerate(sizes):
        def wait_one(n, carry, size=size):
            pltpu.make_async_copy(y_hbm.at[pl.ds(0, size)], ybuf.at[pl.ds(0, size)], sem).wait()
            return carry
        lax.fori_loop(0, meta_ref[0, 0, 3 * LANES // 4 + 1 + b], wait_one, 0)

    sub_lo = sub_hi = None
    for c in range(n_chunks):
        lo, hi = _unpack_bf16_pairs(ybuf[width * c:width * (c + 1), :])
        part_lo = jnp.dot(spread[c], lo, preferred_element_type=F32)
        part_hi = jnp.dot(spread[c], hi, preferred_element_type=F32)
        sub_lo = part_lo if sub_lo is None else sub_lo + part_lo
        sub_hi = part_hi if sub_hi is None else sub_hi + part_hi
    sub = jnp.concatenate([sub_lo, sub_hi], axis=1)
    x = x_ref[...]
    r = alpha * x + (1.0 + mod_ref[0, 2:3, :]) * sub
    o_ref[...] = _layer_norm(r, g_ref[...], b_ref[...])


def _combine(meta, y, pos, x2, mod, route, ln_g, ln_b, alpha, seq, n_exp):
    t, d = x2.shape
    tr = SEQ_TILE
    per_b = seq // tr
    n_rows = TOP_K * tr + n_exp * RUN_ALIGN
    kern = functools.partial(_combine_kernel, alpha=alpha, n_exp=n_exp)
    return pl.pallas_call(
        kern,
        grid=(t // tr,),
        in_specs=[pl.BlockSpec((1, 1, LANES), lambda i: (i, 0, 0), memory_space=pltpu.SMEM),
                  pl.BlockSpec(memory_space=pl.ANY),
                  pl.BlockSpec((tr, LANES), lambda i: (i, 0)),
                  pl.BlockSpec((tr, d), lambda i: (i, 0)),
                  pl.BlockSpec((1, 3, d), lambda i: (i // per_b, 0, 0)),
                  pl.BlockSpec((tr, LANES), lambda i: (i, 0)),
                  pl.BlockSpec((1, d), lambda i: (0, 0)),
                  pl.BlockSpec((1, d), lambda i: (0, 0))],
        out_specs=pl.BlockSpec((tr, d), lambda i: (i, 0)),
        out_shape=jax.ShapeDtypeStruct((t, d), F32),
        scratch_shapes=[pltpu.VMEM((n_rows, d // 2), jnp.int32), pltpu.SemaphoreType.DMA(())],
        compiler_params=_params("arbitrary"),
        name="moe_combine",
    )(meta, y, pos, x2, mod, route, ln_g, ln_b)


def _moe_layer(layer, x2, mod, w_r, b_r, w_gu, b_gu, w_down, b_down, ln_g, ln_b, alpha, seq):
    t, d = x2.shape
    n_exp = w_gu.shape[1]
    tm = MOE_TILE
    n_tok_tiles = t // SEQ_TILE
    assert n_exp <= LANES // 4 and SEQ_TILE % RUN_COPY == 0 and SEQ_TILE == tm
    assert RUN_ALIGN << (RUN_BITS - 1) == SEQ_TILE and RUN_COPY + tm - 1 < 2 * SEQ_TILE
    h, route, cnt = _router(x2, mod, w_r, b_r, seq)
    runs = (cnt[:, 0, :n_exp] + RUN_ALIGN - 1) // RUN_ALIGN * RUN_ALIGN
    run_off = jnp.cumsum(runs, axis=1) - runs
    run_base = jnp.cumsum(runs, axis=0) - runs
    rows = jnp.sum(runs, axis=0)
    region = (rows + RUN_COPY + tm - 1) // tm * tm
    region_end = jnp.cumsum(region)
    region_start = region_end - region
    run_dst = region_start[None, :] + run_base
    n_tiles = -(-(t * TOP_K + n_tok_tiles * n_exp * (RUN_ALIGN - 1) + n_exp * (RUN_COPY + tm - 1)) // tm)
    tile_start = jnp.arange(n_tiles, dtype=jnp.int32) * tm
    tile_expert = jnp.minimum(jnp.sum(region_end[None, :] <= tile_start[:, None], axis=1),
                              n_exp - 1).astype(jnp.int32)
    tile_valid = ((tile_start - region_start[tile_expert] < rows[tile_expert])
                  & (tile_start < region_end[-1])).astype(jnp.int32)
    pad = jnp.zeros((n_tok_tiles, LANES // 4 - n_exp), jnp.int32)
    n_copies = jnp.sum((runs + RUN_COPY - 1) // RUN_COPY, axis=1, keepdims=True)
    pieces = jnp.stack([jnp.sum((runs // (RUN_ALIGN << b)) % 2, axis=1) for b in range(RUN_BITS)], axis=1)
    meta = jnp.concatenate([run_off, pad, run_dst, pad, runs, pad, n_copies, pieces,
                            jnp.zeros((n_tok_tiles, LANES // 4 - 1 - RUN_BITS), jnp.int32)],
                           axis=1).astype(jnp.int32)[:, None, :]
    wide = lambda v: jnp.pad(v.astype(F32), ((0, 0), (0, LANES - n_exp)))
    offdst = jnp.concatenate([wide(run_off)[:, None], wide(run_dst)[:, None],
                              jnp.zeros((n_tok_tiles, 6, LANES), F32)], axis=1)
    tail_len = region - rows
    tail_pieces = jnp.stack([jnp.sum((tail_len // (RUN_ALIGN << b)) % 2) for b in range(RUN_BITS)])
    spare = jnp.stack([region_end[-1], (n_tiles * tm - region_end[-1]) // SEQ_TILE])
    fill = jnp.concatenate([region_start + rows, pad[0], tail_len, pad[0], spare, tail_pieces,
                            jnp.zeros((LANES // 2 - 2 - RUN_BITS,), jnp.int32)]).astype(jnp.int32)
    xs, pos = _dispatch(meta, fill[None, None, :], offdst, route, h, n_tiles * tm, n_exp)
    y = _moe_gemm(layer, tile_expert, tile_valid, xs, w_gu, b_gu, w_down, b_down)
    return _combine(meta, y, pos, x2, mod, route, ln_g, ln_b, alpha, seq, n_exp)


def _split3(v):
    p0 = v.astype(BF16)
    r1 = v - p0.astype(F32)
    p1 = r1.astype(BF16)
    p2 = (r1 - p1.astype(F32)).astype(BF16)
    return p0, p1, p2


AUG_STRIDE = 8
AUG_PARTS = 3


def _aug_constants(d):
    n_pairs = d // LANES
    sel = np.zeros((AUG_PARTS, LANES, d), np.float32)
    ones = np.zeros((1, d), np.float32)
    for p in range(n_pairs):
        for hd in range(HEAD_PAIR):
            base = LANES * p + AUG_STRIDE * hd
            for part in range(AUG_PARTS):
                sel[part, HEAD_PAIR * p + hd, base + part] = 1.0
                ones[0, base + AUG_PARTS + part] = 1.0
    return jnp.asarray(sel, BF16), jnp.asarray(ones, F32)


def _kv_kernel(x_ref, mod_ref, wk_ref, wv_ref, wf_ref, bf_ref, sel_ref, ones_ref,
               k_ref, kaug_ref, vt_ref, cum_ref, carry_ref):
    ts = x_ref.shape[1]
    n_heads = cum_ref.shape[2]
    n_pairs = kaug_ref.shape[1]

    @pl.when(pl.program_id(1) == 0)
    def _():
        carry_ref[...] = jnp.zeros_like(carry_ref)

    h = _modulate(x_ref[0], mod_ref).astype(BF16)
    k_ref[0] = jnp.dot(h, wk_ref[...], preferred_element_type=F32).astype(BF16)
    vt = jnp.dot(h, wv_ref[...], preferred_element_type=F32).T.astype(BF16)
    tk = vt_ref.shape[4]
    for p in range(n_pairs):
        for c in range(ts // tk):
            vt_ref[0, p, c] = vt[LANES * p:LANES * (p + 1), tk * c:tk * (c + 1)]
    fz = jnp.dot(h, wf_ref[...], preferred_element_type=F32) + bf_ref[...]
    log_f = jnp.minimum(fz, 0.0) - jnp.log1p(jnp.exp(-jnp.abs(fz)))
    r_i = lax.broadcasted_iota(jnp.int32, (ts, ts), 0)
    c_i = lax.broadcasted_iota(jnp.int32, (ts, ts), 1)
    tri = jnp.where(c_i <= r_i, 1.0, 0.0).astype(BF16)
    cum = carry_ref[0:1, :]
    for part in _split3(log_f):
        cum = cum + jnp.dot(tri, part, preferred_element_type=F32)
    carry_ref[0:1, :] = cum[ts - 1:ts, :]
    cum = cum * LOG2E
    cum_ref[0] = cum[:, :n_heads]
    aug = ones_ref[...]
    for i, part in enumerate(_split3(cum)):
        aug = aug + jnp.dot(part, sel_ref[i], preferred_element_type=F32)
    aug = aug.astype(BF16)
    for p in range(n_pairs):
        kaug_ref[0, p] = aug[:, LANES * p:LANES * (p + 1)]


def _shared_kv(x, mod, w_k, w_v, w_f, b_f):
    bsz, seq, d = x.shape
    ts = ATTN_TILE
    tk = ATTN_KEY_TILE
    n_pairs = d // LANES
    sel, ones = _aug_constants(d)
    return pl.pallas_call(
        _kv_kernel,
        grid=(bsz, seq // ts),
        in_specs=[pl.BlockSpec((1, ts, d), lambda b, j: (b, j, 0)),
                  pl.BlockSpec((1, 2, d), lambda b, j: (b, 0, 0)),
                  pl.BlockSpec((d, d), lambda b, j: (0, 0)),
                  pl.BlockSpec((d, d), lambda b, j: (0, 0)),
                  pl.BlockSpec((d, LANES), lambda b, j: (0, 0)),
                  pl.BlockSpec((1, LANES), lambda b, j: (0, 0)),
                  pl.BlockSpec((AUG_PARTS, LANES, d), lambda b, j: (0, 0, 0)),
                  pl.BlockSpec((1, d), lambda b, j: (0, 0))],
        out_specs=[pl.BlockSpec((1, ts, d), lambda b, j: (b, j, 0)),
                   pl.BlockSpec((1, n_pairs, ts, LANES), lambda b, j: (b, 0, j, 0)),
                   pl.BlockSpec((1, n_pairs, ts // tk, LANES, tk), lambda b, j: (b, 0, j, 0, 0)),
                   pl.BlockSpec((1, ts, N_HEADS), lambda b, j: (b, j, 0))],
        out_shape=[jax.ShapeDtypeStruct((bsz, seq, d), BF16),
                   jax.ShapeDtypeStruct((bsz, n_pairs, seq, LANES), BF16),
                   jax.ShapeDtypeStruct((bsz, n_pairs, seq // tk, LANES, tk), BF16),
                   jax.ShapeDtypeStruct((bsz, seq, N_HEADS), F32)],
        scratch_shapes=[pltpu.VMEM((8, LANES), F32)],
        compiler_params=_params("arbitrary", "arbitrary"),
        name="shared_kv",
    )(x, mod, w_k, w_v, w_f, b_f, sel, ones)


def _q_proj_kernel(x_ref, mod_ref, wq_ref, q_ref, *, scale):
    h = _modulate(x_ref[0], mod_ref).astype(BF16)
    q_ref[0] = (jnp.dot(h, wq_ref[...], preferred_element_type=F32) * scale).astype(BF16)


def _q_proj(x, mod, w_q, scale):
    bsz, seq, d = x.shape
    ts = SEQ_TILE
    return pl.pallas_call(
        functools.partial(_q_proj_kernel, scale=scale),
        grid=(bsz, seq // ts),
        in_specs=[pl.BlockSpec((1, ts, d), lambda b, j: (b, j, 0)),
                  pl.BlockSpec((1, 3, d), lambda b, j: (b, 0, 0)),
                  pl.BlockSpec((d, d), lambda b, j: (0, 0))],
        out_specs=pl.BlockSpec((1, ts, d), lambda b, j: (b, j, 0)),
        out_shape=jax.ShapeDtypeStruct((bsz, seq, d), BF16),
        compiler_params=_params("arbitrary", "arbitrary"),
        name="q_proj",
    )(x, mod, w_q)


def _attn_kernel(q_ref, k_ref, kaug_ref, vt_ref, cq_ref, o_ref, acc_ref, m_ref, l_ref, s_ref):
    tq = q_ref.shape[1]
    tk = vt_ref.shape[4]
    per_q = tq // tk
    head_dim = LANES // HEAD_PAIR
    i = pl.program_id(2)
    q_t = q_ref[0].astype(F32).T
    row = lax.broadcasted_iota(jnp.int32, (LANES, tq), 0)
    rhs = []
    for g in range(PAIR_GROUP):
        q_pair = q_t[LANES * g:LANES * (g + 1), :]
        for hd in range(HEAD_PAIR):
            own = (row >= head_dim * hd) & (row < head_dim * (hd + 1))
            parts = _split3(cq_ref[0, g, hd:hd + 1, :])
            base = AUG_STRIDE * hd
            aug = jnp.where((row >= base) & (row < base + AUG_PARTS), -1.0, 0.0)
            for n, part in enumerate(parts):
                aug = jnp.where(row == base + AUG_PARTS + n, part.astype(F32), aug)
            rhs.append(jnp.concatenate([jnp.where(own, q_pair, 0.0).astype(BF16), aug.astype(BF16)],
                                       axis=0))
    ones_rows = jnp.ones((16, tk), BF16)
    acc_ref[...] = jnp.zeros_like(acc_ref)
    m_ref[...] = jnp.full_like(m_ref, NEG_INF)
    l_ref[...] = jnp.zeros_like(l_ref)

    def keys_of(j, g):
        start = pl.multiple_of(j * tk, tk)
        return jnp.concatenate([k_ref[0, pl.ds(start, tk), LANES * g:LANES * (g + 1)],
                                kaug_ref[0, g, pl.ds(start, tk), :]], axis=1)

    s_ref[...] = jnp.dot(keys_of(0, 0), rhs[0], preferred_element_type=F32)

    def chunk(j, diagonal, offset=0, has_next=True):
        keys = [keys_of(j, g) for g in range(PAIR_GROUP)]

        def masked(s_t):
            if diagonal:
                k_i = lax.broadcasted_iota(jnp.int32, (tk, tq), 0) + offset
                q_i = lax.broadcasted_iota(jnp.int32, (tk, tq), 1)
                s_t = jnp.where(k_i <= q_i, s_t, NEG_INF)
            return s_t

        n_heads = PAIR_GROUP * HEAD_PAIR
        s_next = s_ref[...]
        for n in range(n_heads):
            g, hd = divmod(n, HEAD_PAIR)
            s_t = masked(s_next)
            if n + 1 < n_heads:
                s_next = jnp.dot(keys[(n + 1) // HEAD_PAIR], rhs[n + 1], preferred_element_type=F32)
            elif has_next:
                s_ref[...] = jnp.dot(keys_of(j + 1, 0), rhs[0], preferred_element_type=F32)
            m_old = m_ref[n:n + 1, :]
            m_new = jnp.maximum(m_old, jnp.max(s_t, axis=0, keepdims=True))
            a = jnp.exp2(m_old - m_new)
            p_t = jnp.exp2(s_t - m_new).astype(BF16)
            m_ref[n:n + 1, :] = m_new
            rows = slice(head_dim * n, head_dim * (n + 1))
            lhs = jnp.concatenate([vt_ref[0, g, j, head_dim * hd:head_dim * (hd + 1), :], ones_rows], axis=0)
            pv = jnp.dot(lhs, p_t, preferred_element_type=F32)
            l_ref[n:n + 1, :] = a * l_ref[n:n + 1, :] + pv[head_dim:head_dim + 1, :]
            acc_ref[rows, :] = acc_ref[rows, :] * a + pv[:head_dim, :]

    def body(j, carry):
        chunk(j, False)
        return carry
    lax.fori_loop(0, i * per_q, body, 0)
    for sub in range(per_q):
        chunk(i * per_q + sub, True, sub * tk, has_next=sub + 1 < per_q)
    inv = jnp.concatenate([jnp.broadcast_to(1.0 / l_ref[n:n + 1, :], (head_dim, tq))
                           for n in range(PAIR_GROUP * HEAD_PAIR)], axis=0)
    o_ref[0] = (acc_ref[...] * inv).T.astype(o_ref.dtype)


def _attention(q, k, kaug, v_t, cq):
    bsz, seq, d = q.shape
    width = PAIR_GROUP * LANES
    tq = ATTN_TILE
    return pl.pallas_call(
        _attn_kernel,
        grid=(bsz, d // width, seq // tq),
        in_specs=[pl.BlockSpec((1, tq, width), lambda b, p, i: (b, i, p)),
                  pl.BlockSpec((1, seq, width), lambda b, p, i: (b, 0, p)),
                  pl.BlockSpec((1, PAIR_GROUP, seq, LANES), lambda b, p, i: (b, p, 0, 0)),
                  pl.BlockSpec((1, PAIR_GROUP) + v_t.shape[2:], lambda b, p, i: (b, p, 0, 0, 0)),
                  pl.BlockSpec((1, PAIR_GROUP, HEAD_PAIR, tq), lambda b, p, i: (b, p, 0, i))],
        out_specs=pl.BlockSpec((1, tq, width), lambda b, p, i: (b, i, p)),
        out_shape=jax.ShapeDtypeStruct((bsz, seq, d), BF16),
        scratch_shapes=[pltpu.VMEM((width, tq), F32),
                        pltpu.VMEM((8, tq), F32),
                        pltpu.VMEM((8, tq), F32),
                        pltpu.VMEM((v_t.shape[4], tq), F32)],
        compiler_params=_params("arbitrary", "arbitrary", "arbitrary"),
        name="fox_attention",
    )(q, k, kaug, v_t, cq)


def _out_proj_kernel(o_ref, x_ref, mod_ref, wo_ref, g_ref, b_ref, out_ref, *, alpha):
    y = jnp.dot(o_ref[0], wo_ref[...], preferred_element_type=F32)
    r = alpha * x_ref[0] + (1.0 + mod_ref[0, 2:3, :]) * y
    out_ref[0] = _layer_norm(r, g_ref[...], b_ref[...])


def _out_proj(o, x, mod, w_o, ln_g, ln_b, alpha):
    bsz, seq, d = x.shape
    ts = SEQ_TILE
    return pl.pallas_call(
        functools.partial(_out_proj_kernel, alpha=alpha),
        grid=(bsz, seq // ts),
        in_specs=[pl.BlockSpec((1, ts, d), lambda b, j: (b, j, 0)),
                  pl.BlockSpec((1, ts, d), lambda b, j: (b, j, 0)),
                  pl.BlockSpec((1, 3, d), lambda b, j: (b, 0, 0)),
                  pl.BlockSpec((d, d), lambda b, j: (0, 0)),
                  pl.BlockSpec((1, d), lambda b, j: (0, 0)),
                  pl.BlockSpec((1, d), lambda b, j: (0, 0))],
        out_specs=pl.BlockSpec((1, ts, d), lambda b, j: (b, j, 0)),
        out_shape=jax.ShapeDtypeStruct(x.shape, F32),
        compiler_params=_params("arbitrary", "arbitrary"),
        name="attn_out_proj",
    )(o, x, mod, w_o, ln_g, ln_b)


def kernel(x, c, conv_w_in, conv_w, conv_w_out, kv_ada_w, kv_ada_b, w_kvf, b_f, attn_w_q, attn_w_o,
           ada_w, ada_b, ln_g, ln_b, router_w, router_b, exp_w_gu, exp_b_gu, exp_w_down, exp_b_down):
    bsz, seq, d = x.shape
    depth = ada_w.shape[0]
    n_conv = conv_w_in.shape[0]
    n_exp = router_w.shape[-1]
    alpha = (2.0 * depth) ** 0.25
    head_dim = d // N_HEADS
    assert head_dim * HEAD_PAIR == LANES and seq % SEQ_TILE == 0 and seq % ATTN_TILE == 0

    c_pad = jnp.pad(c, ((0, 8 - bsz), (0, 0)))
    mods = _ada_params(c_pad, ada_w.reshape(depth * 2, d, 3 * d), ada_b.reshape(depth * 2, 1, 3 * d))
    mods = mods[:, :bsz, :].reshape(depth, 2, bsz, 3, d)
    kv_mod = _ada_params(c_pad, kv_ada_w[None], kv_ada_b[None, None])[0, :bsz].reshape(bsz, 2, d)

    w_r = jnp.pad(router_w, ((0, 0), (0, 0), (0, LANES - n_exp)))
    w_r_hi = w_r.astype(BF16)
    w_r = jnp.concatenate([w_r_hi, (w_r - w_r_hi.astype(F32)).astype(BF16)], axis=-1)
    b_r = jnp.pad(router_b, ((0, 0), (0, LANES - n_exp)), constant_values=-1e30)[:, None, :]
    b_gu = exp_b_gu[:, :, None, :]
    b_dn = exp_b_down[:, :, None, :]
    k = kaug = v_t = cq = None
    for l in range(depth):
        g0, b0 = ln_g[l, 0][None], ln_b[l, 0][None]
        if l < n_conv:
            x = _conv_layer(x, mods[l, 0], conv_w_in[l].astype(BF16), conv_w[l],
                            conv_w_out[l].astype(BF16), g0, b0, alpha)
        else:
            j = l - n_conv
            q = _q_proj(x, mods[l, 0], attn_w_q[j].astype(BF16), head_dim ** -0.5 * LOG2E)
            o = _attention(q, k, kaug, v_t, cq)
            x = _out_proj(o, x, mods[l, 0], attn_w_o[j].astype(BF16), g0, b0, alpha)
        x = _moe_layer(l, x.reshape(bsz * seq, d), mods[l, 1], w_r[l], b_r[l], exp_w_gu, b_gu,
                       exp_w_down, b_dn,
                       ln_g[l, 1][None], ln_b[l, 1][None], alpha, seq).reshape(bsz, seq, d)
        if l == n_conv - 1:
            w_f = jnp.pad(w_kvf[:, 2 * d:], ((0, 0), (0, LANES - N_HEADS))).astype(BF16)
            bias_f = jnp.pad(b_f, (0, LANES - N_HEADS))[None]
            k, kaug, v_t, cum = _shared_kv(x, kv_mod, w_kvf[:, :d].astype(BF16),
                                           w_kvf[:, d:2 * d].astype(BF16), w_f, bias_f)
            cq = cum.reshape(bsz, seq, N_HEADS // HEAD_PAIR, HEAD_PAIR).transpose(0, 2, 3, 1)
    return x
```

```python
import functools

import numpy as np
import jax
import jax.numpy as jnp
from jax import lax
from jax.experimental import pallas as pl
from jax.experimental.pallas import tpu as pltpu

N_HEADS = 16
TOP_K = 4
SWIGLU_LIMIT = 7.0
SWIGLU_ALPHA = 1.702
LN_EPS = 1e-5
LANES = 128
BF16_ROWS = 16
HEAD_PAIR = 2
PAIR_GROUP = 2
LOG2E = 1.4426950408889634
SEQ_TILE = 512
MOE_TILE = 512
ATTN_TILE = 512
ATTN_KEY_TILE = 512
RUN_ALIGN = 8
RUN_COPY = 128
RUN_BITS = 7
VMEM_LIMIT = 56 * 1024 * 1024

F32 = jnp.float32
BF16 = jnp.bfloat16
NEG_INF = float("-inf")


def _params(*sem):
    return pltpu.CompilerParams(dimension_semantics=sem, vmem_limit_bytes=VMEM_LIMIT)


def _layer_norm(r, g, b):
    mu = jnp.mean(r, axis=-1, keepdims=True)
    d = r - mu
    var = jnp.mean(d * d, axis=-1, keepdims=True)
    return d * lax.rsqrt(var + LN_EPS) * g + b


def _modulate(x, mod_ref):
    return x * (1.0 + mod_ref[0, 1:2, :]) + mod_ref[0, 0:1, :]


def _ada_kernel(c_ref, w_ref, b_ref, o_ref):
    c = c_ref[...]
    cond = c * jax.nn.sigmoid(c)
    o_ref[0] = jnp.dot(cond, w_ref[0], precision=lax.Precision.HIGHEST,
                       preferred_element_type=F32) + b_ref[0]


def _ada_params(c_pad, w, b):
    g, d, n = w.shape
    tn = 1024 if n % 1024 == 0 else n
    return pl.pallas_call(
        _ada_kernel,
        grid=(g, n // tn),
        in_specs=[pl.BlockSpec((8, d), lambda i, j: (0, 0)),
                  pl.BlockSpec((1, d, tn), lambda i, j: (i, 0, j)),
                  pl.BlockSpec((1, 1, tn), lambda i, j: (i, 0, j))],
        out_specs=pl.BlockSpec((1, 8, tn), lambda i, j: (i, 0, j)),
        out_shape=jax.ShapeDtypeStruct((g, 8, n), F32),
        compiler_params=_params("arbitrary", "arbitrary"),
        name="ada_params",
    )(c_pad, w, b)


def _conv_layer_kernel(x_ref, mod_ref, win_ref, wc_ref, wout_ref, g_ref, b_ref, o_ref,
                       carry_ref, a_ref, *, alpha, col_chunk):
    ts, d = x_ref.shape[1], x_ref.shape[2]

    @pl.when(pl.program_id(1) == 0)
    def _():
        carry_ref[...] = jnp.zeros_like(carry_ref)

    x = x_ref[0]
    h = _modulate(x, mod_ref).astype(BF16)
    row = lax.broadcasted_iota(jnp.int32, (ts, col_chunk), 0)
    for c in range(0, d, col_chunk):
        gate_c = jnp.dot(h, win_ref[:, c:c + col_chunk], preferred_element_type=F32)
        u = jnp.dot(h, win_ref[:, 2 * d + c:2 * d + c + col_chunk], preferred_element_type=F32)
        z = gate_c * u
        z_m1 = carry_ref[1:2, c:c + col_chunk]
        z_m2 = carry_ref[0:1, c:c + col_chunk]
        z1 = jnp.where(row == 0, z_m1, pltpu.roll(z, 1, 0))
        z2 = jnp.where(row == 0, z_m2, jnp.where(row == 1, z_m1, pltpu.roll(z, 2, 0)))
        carry_ref[0:2, c:c + col_chunk] = z[ts - 2:ts, :]
        conv = (wc_ref[0:1, c:c + col_chunk] * z2 + wc_ref[1:2, c:c + col_chunk] * z1
                + wc_ref[2:3, c:c + col_chunk] * z)
        gate_b = jnp.dot(h, win_ref[:, d + c:d + c + col_chunk], preferred_element_type=F32)
        a_ref[:, c:c + col_chunk] = (gate_b * conv).astype(BF16)
    y = jnp.dot(a_ref[...], wout_ref[...], preferred_element_type=F32)
    r = alpha * x + (1.0 + mod_ref[0, 2:3, :]) * y
    o_ref[0] = _layer_norm(r, g_ref[...], b_ref[...])


def _conv_layer(x, mod, w_in, w_conv, w_out, ln_g, ln_b, alpha):
    bsz, seq, d = x.shape
    ts = SEQ_TILE
    kern = functools.partial(_conv_layer_kernel, alpha=alpha, col_chunk=256)
    return pl.pallas_call(
        kern,
        grid=(bsz, seq // ts),
        in_specs=[pl.BlockSpec((1, ts, d), lambda b, j: (b, j, 0)),
                  pl.BlockSpec((1, 3, d), lambda b, j: (b, 0, 0)),
                  pl.BlockSpec((d, 3 * d), lambda b, j: (0, 0)),
                  pl.BlockSpec((3, d), lambda b, j: (0, 0)),
                  pl.BlockSpec((d, d), lambda b, j: (0, 0)),
                  pl.BlockSpec((1, d), lambda b, j: (0, 0)),
                  pl.BlockSpec((1, d), lambda b, j: (0, 0))],
        out_specs=pl.BlockSpec((1, ts, d), lambda b, j: (b, j, 0)),
        out_shape=jax.ShapeDtypeStruct(x.shape, F32),
        scratch_shapes=[pltpu.VMEM((8, d), F32), pltpu.VMEM((ts, d), BF16)],
        compiler_params=_params("arbitrary", "arbitrary"),
        name="conv_layer",
    )(x, mod, w_in, w_conv, w_out, ln_g, ln_b)


def _pack_bf16_pairs(h):
    half = h.shape[1] // 2
    lo = lax.bitcast_convert_type(h[:, :half].astype(BF16).astype(F32), jnp.int32)
    hi = lax.bitcast_convert_type(h[:, half:].astype(BF16).astype(F32), jnp.int32)
    return lax.shift_right_logical(lo, 16) | hi


def _unpack_bf16_pairs(w):
    lo = lax.bitcast_convert_type(w << 16, F32).astype(BF16)
    hi = lax.bitcast_convert_type(w & jnp.int32(-65536), F32).astype(BF16)
    return lo, hi


def _router_kernel(x_ref, mod_ref, wr_ref, br_ref, h_ref, route_ref, cnt_ref):
    tr = x_ref.shape[0]
    h = _modulate(x_ref[...], mod_ref)
    h_hi = h.astype(BF16)
    h_ref[...] = h_hi
    h_lo = (h - h_hi.astype(F32)).astype(BF16)
    both = jnp.dot(h_hi, wr_ref[...], preferred_element_type=F32)
    logits = (both[:, :LANES] + both[:, LANES:]
              + jnp.dot(h_lo, wr_ref[:, :LANES], preferred_element_type=F32)) + br_ref[...]
    lane = lax.broadcasted_iota(jnp.int32, (tr, LANES), 1)
    lane_f = lane.astype(F32)
    work = logits
    vals, idxs, sels = [], [], []
    for _ in range(TOP_K):
        m = jnp.max(work, axis=-1, keepdims=True)
        idx = jnp.min(jnp.where(work == m, lane_f, float(LANES)), axis=-1, keepdims=True)
        sel = lane_f == idx
        vals.append(m)
        idxs.append(idx.astype(jnp.int32))
        sels.append(sel)
        work = jnp.where(sel, NEG_INF, work)
    exps = [jnp.exp(v - vals[0]) for v in vals]
    denom = exps[0] + exps[1] + exps[2] + exps[3]
    chosen = sels[0] | sels[1] | sels[2] | sels[3]
    onehot = jnp.where(chosen, 1.0, 0.0).astype(BF16)
    r_i = lax.broadcasted_iota(jnp.int32, (tr, tr), 0)
    c_i = lax.broadcasted_iota(jnp.int32, (tr, tr), 1)
    tri = jnp.where(c_i < r_i, 1.0, 0.0).astype(BF16)
    before = jnp.dot(tri, onehot, preferred_element_type=F32)
    out = jnp.zeros((tr, LANES), jnp.int32)
    for k in range(TOP_K):
        rank = jnp.sum(jnp.where(sels[k], before, 0.0), axis=-1, keepdims=True).astype(jnp.int32)
        wgt = lax.bitcast_convert_type(exps[k] / denom, jnp.int32)
        out = jnp.where(lane == k, idxs[k], out)
        out = jnp.where(lane == TOP_K + k, rank, out)
        out = jnp.where(lane == 2 * TOP_K + k, wgt, out)
    route_ref[...] = out
    counts = jnp.sum(onehot.astype(F32), axis=0, keepdims=True).astype(jnp.int32)
    cnt_ref[0] = jnp.broadcast_to(counts, (8, LANES))


def _router(x2, mod, w_r, b_r, seq):
    t, d = x2.shape
    tr = SEQ_TILE
    per_b = seq // tr
    return pl.pallas_call(
        _router_kernel,
        grid=(t // tr,),
        in_specs=[pl.BlockSpec((tr, d), lambda i: (i, 0)),
                  pl.BlockSpec((1, 3, d), lambda i: (i // per_b, 0, 0)),
                  pl.BlockSpec((d, 2 * LANES), lambda i: (0, 0)),
                  pl.BlockSpec((1, LANES), lambda i: (0, 0))],
        out_specs=[pl.BlockSpec((tr, d), lambda i: (i, 0)),
                   pl.BlockSpec((tr, LANES), lambda i: (i, 0)),
                   pl.BlockSpec((1, 8, LANES), lambda i: (i, 0, 0))],
        out_shape=[jax.ShapeDtypeStruct((t, d), BF16),
                   jax.ShapeDtypeStruct((t, LANES), jnp.int32),
                   jax.ShapeDtypeStruct((t // tr, 8, LANES), jnp.int32)],
        compiler_params=_params("arbitrary"),
        name="moe_router",
    )(x2, mod, w_r, b_r)


def _dispatch_kernel(meta_ref, prev_ref, fill_ref, offdst_ref, route_ref, h_ref, xs_hbm, pos_ref, xs_buf, sem,
                     *, n_exp):
    tr, d = h_ref.shape
    n_rows = xs_buf.shape[1] - RUN_COPY
    i = pl.program_id(0)
    buf = i % 2
    sizes = [RUN_ALIGN << b for b in range(RUN_BITS)]

    @pl.when(i == 0)
    def _():
        xs_buf[0, n_rows:, :] = jnp.zeros((RUN_COPY, d // 2), jnp.int32)
        xs_buf[1] = jnp.zeros(xs_buf.shape[1:], jnp.int32)
        for e in range(n_exp):
            start = pl.multiple_of(fill_ref[0, 0, e], RUN_ALIGN)
            length = fill_ref[0, 0, LANES // 4 + e]
            for size in reversed(sizes):
                done = pl.multiple_of((length // (2 * size)) * (2 * size), RUN_ALIGN)

                @pl.when((length & size) != 0)
                def _():
                    pltpu.make_async_copy(xs_buf.at[1, pl.ds(0, size)],
                                          xs_hbm.at[pl.ds(start + done, size)], sem).start()
        spare = pl.multiple_of(fill_ref[0, 0, LANES // 2], tr)

        def fill_spare(n, carry):
            pltpu.make_async_copy(xs_buf.at[1, pl.ds(0, tr)],
                                  xs_hbm.at[pl.ds(pl.multiple_of(spare + n * tr, tr), tr)], sem).start()
            return carry
        lax.fori_loop(0, fill_ref[0, 0, LANES // 2 + 1], fill_spare, 0)
        for b, size in enumerate(sizes):
            def wait_piece(n, carry, size=size):
                pltpu.make_async_copy(xs_buf.at[1, pl.ds(0, size)], xs_hbm.at[pl.ds(0, size)], sem).wait()
                return carry
            lax.fori_loop(0, fill_ref[0, 0, LANES // 2 + 2 + b], wait_piece, 0)

        def wait_spare(n, carry):
            pltpu.make_async_copy(xs_buf.at[1, pl.ds(0, tr)], xs_hbm.at[pl.ds(0, tr)], sem).wait()
            return carry
        lax.fori_loop(0, fill_ref[0, 0, LANES // 2 + 1], wait_spare, 0)

    route = route_ref[...]
    lane = lax.broadcasted_iota(jnp.int32, (tr, LANES), 1)
    off_row = offdst_ref[0, 0:1, :]
    pos = jnp.full((tr, LANES), -1.0, F32)
    pos_out = jnp.zeros((tr, LANES), jnp.int32)
    for k in range(TOP_K):
        sel = lane == route[:, k:k + 1]
        rank = route[:, TOP_K + k:TOP_K + k + 1].astype(F32)
        pos_k = jnp.sum(jnp.where(sel, off_row, 0.0), axis=-1, keepdims=True) + rank
        pos = jnp.where(lane == k, pos_k, pos)
        pos_out = jnp.where(lane == k, pos_k.astype(jnp.int32), pos_out)
    pos_ref[...] = pos_out
    pos_t = pos.T
    h = h_ref[...]
    chunk = n_rows // 4
    for c in range(4):
        slot = (lax.broadcasted_iota(jnp.int32, (chunk, tr), 0) + chunk * c).astype(F32)
        hit = slot == pos_t[0:1, :]
        for k in range(1, TOP_K):
            hit = hit | (slot == pos_t[k:k + 1, :])
        onehot = jnp.where(hit, 1.0, 0.0).astype(BF16)
        rows = jnp.dot(onehot, h, preferred_element_type=F32)
        xs_buf[buf, chunk * c:chunk * (c + 1), :] = _pack_bf16_pairs(rows)

    def wait_copies(count):
        def wait_one(n, carry):
            pltpu.make_async_copy(xs_buf.at[0, pl.ds(0, RUN_COPY)], xs_hbm.at[pl.ds(0, RUN_COPY)], sem).wait()
            return carry
        lax.fori_loop(0, count, wait_one, 0)

    @pl.when(i > 0)
    def _():
        wait_copies(prev_ref[0, 0, 3 * LANES // 4])

    for e in range(n_exp):
        src = pl.multiple_of(meta_ref[0, 0, e], RUN_ALIGN)
        dst = pl.multiple_of(meta_ref[0, 0, LANES // 4 + e], RUN_ALIGN)
        length = meta_ref[0, 0, LANES // 2 + e]
        for q in range(tr // RUN_COPY):
            @pl.when(length > RUN_COPY * q)
            def _():
                pltpu.make_async_copy(xs_buf.at[buf, pl.ds(src + RUN_COPY * q, RUN_COPY)],
                                      xs_hbm.at[pl.ds(dst + RUN_COPY * q, RUN_COPY)], sem).start()

    @pl.when(i == pl.num_programs(0) - 1)
    def _():
        wait_copies(meta_ref[0, 0, 3 * LANES // 4])


def _dispatch(meta, fill, offdst, route, h, n_slots, n_exp):
    t, d = h.shape
    tr = SEQ_TILE
    n_rows = TOP_K * tr + n_exp * RUN_ALIGN
    return pl.pallas_call(
        functools.partial(_dispatch_kernel, n_exp=n_exp),
        grid=(t // tr,),
        in_specs=[pl.BlockSpec((1, 1, LANES), lambda i: (i, 0, 0), memory_space=pltpu.SMEM),
                  pl.BlockSpec((1, 1, LANES), lambda i: (jnp.maximum(i - 1, 0), 0, 0), memory_space=pltpu.SMEM),
                  pl.BlockSpec((1, 1, LANES), lambda i: (0, 0, 0), memory_space=pltpu.SMEM),
                  pl.BlockSpec((1, 8, LANES), lambda i: (i, 0, 0)),
                  pl.BlockSpec((tr, LANES), lambda i: (i, 0)),
                  pl.BlockSpec((tr, d), lambda i: (i, 0))],
        out_specs=[pl.BlockSpec(memory_space=pl.ANY),
                   pl.BlockSpec((tr, LANES), lambda i: (i, 0))],
        out_shape=[jax.ShapeDtypeStruct((n_slots, d // 2), jnp.int32),
                   jax.ShapeDtypeStruct((t, LANES), jnp.int32)],
        scratch_shapes=[pltpu.VMEM((2, n_rows + RUN_COPY, d // 2), jnp.int32), pltpu.SemaphoreType.DMA(())],
        compiler_params=_params("arbitrary"),
        name="moe_dispatch",
    )(meta, meta, fill, offdst, route, h)


def _moe_gemm_kernel(te_ref, valid_ref, xs_ref, wgu_ref, bgu_ref, wd_ref, bd_ref, y_ref, wgu_bf, wd_bf):
    half = xs_ref.shape[1]
    f = wd_bf.shape[0]
    i = pl.program_id(0)

    @pl.when((i == 0) | (te_ref[i] != te_ref[jnp.maximum(i - 1, 0)]))
    def _():
        wgu_bf[...] = wgu_ref[0, 0].astype(BF16)
        wd_bf[...] = wd_ref[0, 0].astype(BF16)

    @pl.when(valid_ref[i] == 1)
    def _():
        lo, hi = _unpack_bf16_pairs(xs_ref[...])
        gu = (jnp.dot(lo, wgu_bf[:half, :], preferred_element_type=F32)
              + jnp.dot(hi, wgu_bf[half:, :], preferred_element_type=F32) + bgu_ref[0, 0])
        g = jnp.minimum(gu[:, :f], SWIGLU_LIMIT)
        u = jnp.clip(gu[:, f:], -SWIGLU_LIMIT, SWIGLU_LIMIT)
        a = g * jax.nn.sigmoid(SWIGLU_ALPHA * g) * (u + 1.0)
        y = jnp.dot(a.astype(BF16), wd_bf[...], preferred_element_type=F32) + bd_ref[0, 0]
        y_ref[...] = _pack_bf16_pairs(y)

    @pl.when(valid_ref[i] == 0)
    def _():
        y_ref[...] = jnp.zeros_like(y_ref)


def _moe_gemm(layer, tile_expert, tile_valid, xs, w_gu, b_gu, w_down, b_down):
    n_tiles = tile_expert.shape[0]
    tm = MOE_TILE
    _, _, d, f2 = w_gu.shape
    f = f2 // 2
    grid_spec = pltpu.PrefetchScalarGridSpec(
        num_scalar_prefetch=2,
        grid=(n_tiles,),
        in_specs=[pl.BlockSpec((tm, d // 2), lambda i, te, tv: (i, 0)),
                  pl.BlockSpec((1, 1, d, f2), lambda i, te, tv: (layer, te[i], 0, 0)),
                  pl.BlockSpec((1, 1, 1, f2), lambda i, te, tv: (layer, te[i], 0, 0)),
                  pl.BlockSpec((1, 1, f, d), lambda i, te, tv: (layer, te[i], 0, 0)),
                  pl.BlockSpec((1, 1, 1, d), lambda i, te, tv: (layer, te[i], 0, 0))],
        out_specs=pl.BlockSpec((tm, d // 2), lambda i, te, tv: (i, 0)),
        scratch_shapes=[pltpu.VMEM((d, f2), BF16), pltpu.VMEM((f, d), BF16)],
    )
    return pl.pallas_call(
        _moe_gemm_kernel,
        grid_spec=grid_spec,
        out_shape=jax.ShapeDtypeStruct((n_tiles * tm, d // 2), jnp.int32),
        compiler_params=_params("arbitrary"),
        name="moe_experts",
    )(tile_expert, tile_valid, xs, w_gu, b_gu, w_down, b_down)


def _combine_kernel(meta_ref, y_hbm, pos_ref, x_ref, mod_ref, route_ref, g_ref, b_ref, o_ref, ybuf, sem,
                    *, alpha, n_exp):
    tr = x_ref.shape[0]
    n_rows = ybuf.shape[0]
    sizes = [RUN_ALIGN << b for b in range(RUN_BITS)]

    @pl.when(pl.program_id(0) == 0)
    def _():
        ybuf[...] = jnp.zeros_like(ybuf)

    for e in range(n_exp):
        off = pl.multiple_of(meta_ref[0, 0, e], RUN_ALIGN)
        src = pl.multiple_of(meta_ref[0, 0, LANES // 4 + e], RUN_ALIGN)
        length = meta_ref[0, 0, LANES // 2 + e]
        for size in reversed(sizes):
            done = pl.multiple_of((length // (2 * size)) * (2 * size), RUN_ALIGN)

            @pl.when((length & size) != 0)
            def _():
                pltpu.make_async_copy(y_hbm.at[pl.ds(src + done, size)],
                                      ybuf.at[pl.ds(off + done, size)], sem).start()

    pos = pos_ref[...]
    wts = lax.bitcast_convert_type(route_ref[...], F32)
    n_chunks = 3
    width = n_rows // n_chunks
    spread = []
    for c in range(n_chunks):
        slot = lax.broadcasted_iota(jnp.int32, (tr, width), 1) + width * c
        w_c = jnp.zeros((tr, width), F32)
        for k in range(TOP_K):
            w_c = jnp.where(slot == pos[:, k:k + 1], wts[:, 2 * TOP_K + k:2 * TOP_K + k + 1], w_c)
        spread.append(w_c.astype(BF16))

    for b, size in enumerate(sizes):
        def wait_one(n, carry, size=size):
            pltpu.make_async_copy(y_hbm.at[pl.ds(0, size)], ybuf.at[pl.ds(0, size)], sem).wait()
            return carry
        lax.fori_loop(0, meta_ref[0, 0, 3 * LANES // 4 + 1 + b], wait_one, 0)

    sub_lo = sub_hi = None
    for c in range(n_chunks):
        lo, hi = _unpack_bf16_pairs(ybuf[width * c:width * (c + 1), :])
        part_lo = jnp.dot(spread[c], lo, preferred_element_type=F32)
        part_hi = jnp.dot(spread[c], hi, preferred_element_type=F32)
        sub_lo = part_lo if sub_lo is None else sub_lo + part_lo
        sub_hi = part_hi if sub_hi is None else sub_hi + part_hi
    sub = jnp.concatenate([sub_lo, sub_hi], axis=1)
    x = x_ref[...]
    r = alpha * x + (1.0 + mod_ref[0, 2:3, :]) * sub
    o_ref[...] = _layer_norm(r, g_ref[...], b_ref[...])


def _combine(meta, y, pos, x2, mod, route, ln_g, ln_b, alpha, seq, n_exp):
    t, d = x2.shape
    tr = SEQ_TILE
    per_b = seq // tr
    n_rows = TOP_K * tr + n_exp * RUN_ALIGN
    kern = functools.partial(_combine_kernel, alpha=alpha, n_exp=n_exp)
    return pl.pallas_call(
        kern,
        grid=(t // tr,),
        in_specs=[pl.BlockSpec((1, 1, LANES), lambda i: (i, 0, 0), memory_space=pltpu.SMEM),
                  pl.BlockSpec(memory_space=pl.ANY),
                  pl.BlockSpec((tr, LANES), lambda i: (i, 0)),
                  pl.BlockSpec((tr, d), lambda i: (i, 0)),
                  pl.BlockSpec((1, 3, d), lambda i: (i // per_b, 0, 0)),
                  pl.BlockSpec((tr, LANES), lambda i: (i, 0)),
                  pl.BlockSpec((1, d), lambda i: (0, 0)),
                  pl.BlockSpec((1, d), lambda i: (0, 0))],
        out_specs=pl.BlockSpec((tr, d), lambda i: (i, 0)),
        out_shape=jax.ShapeDtypeStruct((t, d), F32),
        scratch_shapes=[pltpu.VMEM((n_rows, d // 2), jnp.int32), pltpu.SemaphoreType.DMA(())],
        compiler_params=_params("arbitrary"),
        name="moe_combine",
    )(meta, y, pos, x2, mod, route, ln_g, ln_b)


def _moe_layer(layer, x2, mod, w_r, b_r, w_gu, b_gu, w_down, b_down, ln_g, ln_b, alpha, seq):
    t, d = x2.shape
    n_exp = w_gu.shape[1]
    tm = MOE_TILE
    n_tok_tiles = t // SEQ_TILE
    assert n_exp <= LANES // 4 and SEQ_TILE % RUN_COPY == 0 and SEQ_TILE == tm
    assert RUN_ALIGN << (RUN_BITS - 1) == SEQ_TILE and RUN_COPY + tm - 1 < 2 * SEQ_TILE
    h, route, cnt = _router(x2, mod, w_r, b_r, seq)
    runs = (cnt[:, 0, :n_exp] + RUN_ALIGN - 1) // RUN_ALIGN * RUN_ALIGN
    run_off = jnp.cumsum(runs, axis=1) - runs
    run_base = jnp.cumsum(runs, axis=0) - runs
    rows = jnp.sum(runs, axis=0)
    region = (rows + RUN_COPY + tm - 1) // tm * tm
    region_end = jnp.cumsum(region)
    region_start = region_end - region
    run_dst = region_start[None, :] + run_base
    n_tiles = -(-(t * TOP_K + n_tok_tiles * n_exp * (RUN_ALIGN - 1) + n_exp * (RUN_COPY + tm - 1)) // tm)
    tile_start = jnp.arange(n_tiles, dtype=jnp.int32) * tm
    tile_expert = jnp.minimum(jnp.sum(region_end[None, :] <= tile_start[:, None], axis=1),
                              n_exp - 1).astype(jnp.int32)
    tile_valid = ((tile_start - region_start[tile_expert] < rows[tile_expert])
                  & (tile_start < region_end[-1])).astype(jnp.int32)
    pad = jnp.zeros((n_tok_tiles, LANES // 4 - n_exp), jnp.int32)
    n_copies = jnp.sum((runs + RUN_COPY - 1) // RUN_COPY, axis=1, keepdims=True)
    pieces = jnp.stack([jnp.sum((runs // (RUN_ALIGN << b)) % 2, axis=1) for b in range(RUN_BITS)], axis=1)
    meta = jnp.concatenate([run_off, pad, run_dst, pad, runs, pad, n_copies, pieces,
                            jnp.zeros((n_tok_tiles, LANES // 4 - 1 - RUN_BITS), jnp.int32)],
                           axis=1).astype(jnp.int32)[:, None, :]
    wide = lambda v: jnp.pad(v.astype(F32), ((0, 0), (0, LANES - n_exp)))
    offdst = jnp.concatenate([wide(run_off)[:, None], wide(run_dst)[:, None],
                              jnp.zeros((n_tok_tiles, 6, LANES), F32)], axis=1)
    tail_len = region - rows
    tail_pieces = jnp.stack([jnp.sum((tail_len // (RUN_ALIGN << b)) % 2) for b in range(RUN_BITS)])
    spare = jnp.stack([region_end[-1], (n_tiles * tm - region_end[-1]) // SEQ_TILE])
    fill = jnp.concatenate([region_start + rows, pad[0], tail_len, pad[0], spare, tail_pieces,
                            jnp.zeros((LANES // 2 - 2 - RUN_BITS,), jnp.int32)]).astype(jnp.int32)
    xs, pos = _dispatch(meta, fill[None, None, :], offdst, route, h, n_tiles * tm, n_exp)
    y = _moe_gemm(layer, tile_expert, tile_valid, xs, w_gu, b_gu, w_down, b_down)
    return _combine(meta, y, pos, x2, mod, route, ln_g, ln_b, alpha, seq, n_exp)


def _split3(v):
    p0 = v.astype(BF16)
    r1 = v - p0.astype(F32)
    p1 = r1.astype(BF16)
    p2 = (r1 - p1.astype(F32)).astype(BF16)
    return p0, p1, p2


AUG_STRIDE = 8
AUG_PARTS = 3


def _aug_constants(d):
    n_pairs = d // LANES
    sel = np.zeros((AUG_PARTS, LANES, d), np.float32)
    ones = np.zeros((1, d), np.float32)
    for p in range(n_pairs):
        for hd in range(HEAD_PAIR):
            base = LANES * p + AUG_STRIDE * hd
            for part in range(AUG_PARTS):
                sel[part, HEAD_PAIR * p + hd, base + part] = 1.0
                ones[0, base + AUG_PARTS + part] = 1.0
    return jnp.asarray(sel, BF16), jnp.asarray(ones, F32)


def _kv_kernel(x_ref, mod_ref, wk_ref, wv_ref, wf_ref, bf_ref, sel_ref, ones_ref,
               k_ref, kaug_ref, vt_ref, cum_ref, carry_ref):
    ts = x_ref.shape[1]
    n_heads = cum_ref.shape[2]
    n_pairs = kaug_ref.shape[1]

    @pl.when(pl.program_id(1) == 0)
    def _():
        carry_ref[...] = jnp.zeros_like(carry_ref)

    h = _modulate(x_ref[0], mod_ref).astype(BF16)
    k_ref[0] = jnp.dot(h, wk_ref[...], preferred_element_type=F32).astype(BF16)
    vt = jnp.dot(h, wv_ref[...], preferred_element_type=F32).T.astype(BF16)
    tk = vt_ref.shape[4]
    for p in range(n_pairs):
        for c in range(ts // tk):
            vt_ref[0, p, c] = vt[LANES * p:LANES * (p + 1), tk * c:tk * (c + 1)]
    fz = jnp.dot(h, wf_ref[...], preferred_element_type=F32) + bf_ref[...]
    log_f = jnp.minimum(fz, 0.0) - jnp.log1p(jnp.exp(-jnp.abs(fz)))
    r_i = lax.broadcasted_iota(jnp.int32, (ts, ts), 0)
    c_i = lax.broadcasted_iota(jnp.int32, (ts, ts), 1)
    tri = jnp.where(c_i <= r_i, 1.0, 0.0).astype(BF16)
    cum = carry_ref[0:1, :]
    for part in _split3(log_f):
        cum = cum + jnp.dot(tri, part, preferred_element_type=F32)
    carry_ref[0:1, :] = cum[ts - 1:ts, :]
    cum = cum * LOG2E
    cum_ref[0] = cum[:, :n_heads]
    aug = ones_ref[...]
    for i, part in enumerate(_split3(cum)):
        aug = aug + jnp.dot(part, sel_ref[i], preferred_element_type=F32)
    aug = aug.astype(BF16)
    for p in range(n_pairs):
        kaug_ref[0, p] = aug[:, LANES * p:LANES * (p + 1)]


def _shared_kv(x, mod, w_k, w_v, w_f, b_f):
    bsz, seq, d = x.shape
    ts = ATTN_TILE
    tk = ATTN_KEY_TILE
    n_pairs = d // LANES
    sel, ones = _aug_constants(d)
    return pl.pallas_call(
        _kv_kernel,
        grid=(bsz, seq // ts),
        in_specs=[pl.BlockSpec((1, ts, d), lambda b, j: (b, j, 0)),
                  pl.BlockSpec((1, 2, d), lambda b, j: (b, 0, 0)),
                  pl.BlockSpec((d, d), lambda b, j: (0, 0)),
                  pl.BlockSpec((d, d), lambda b, j: (0, 0)),
                  pl.BlockSpec((d, LANES), lambda b, j: (0, 0)),
                  pl.BlockSpec((1, LANES), lambda b, j: (0, 0)),
                  pl.BlockSpec((AUG_PARTS, LANES, d), lambda b, j: (0, 0, 0)),
                  pl.BlockSpec((1, d), lambda b, j: (0, 0))],
        out_specs=[pl.BlockSpec((1, ts, d), lambda b, j: (b, j, 0)),
                   pl.BlockSpec((1, n_pairs, ts, LANES), lambda b, j: (b, 0, j, 0)),
                   pl.BlockSpec((1, n_pairs, ts // tk, LANES, tk), lambda b, j: (b, 0, j, 0, 0)),
                   pl.BlockSpec((1, ts, N_HEADS), lambda b, j: (b, j, 0))],
        out_shape=[jax.ShapeDtypeStruct((bsz, seq, d), BF16),
                   jax.ShapeDtypeStruct((bsz, n_pairs, seq, LANES), BF16),
                   jax.ShapeDtypeStruct((bsz, n_pairs, seq // tk, LANES, tk), BF16),
                   jax.ShapeDtypeStruct((bsz, seq, N_HEADS), F32)],
        scratch_shapes=[pltpu.VMEM((8, LANES), F32)],
        compiler_params=_params("arbitrary", "arbitrary"),
        name="shared_kv",
    )(x, mod, w_k, w_v, w_f, b_f, sel, ones)


def _q_proj_kernel(x_ref, mod_ref, wq_ref, q_ref, *, scale):
    h = _modulate(x_ref[0], mod_ref).astype(BF16)
    q_ref[0] = (jnp.dot(h, wq_ref[...], preferred_element_type=F32) * scale).astype(BF16)


def _q_proj(x, mod, w_q, scale):
    bsz, seq, d = x.shape
    ts = SEQ_TILE
    return pl.pallas_call(
        functools.partial(_q_proj_kernel, scale=scale),
        grid=(bsz, seq // ts),
        in_specs=[pl.BlockSpec((1, ts, d), lambda b, j: (b, j, 0)),
                  pl.BlockSpec((1, 3, d), lambda b, j: (b, 0, 0)),
                  pl.BlockSpec((d, d), lambda b, j: (0, 0))],
        out_specs=pl.BlockSpec((1, ts, d), lambda b, j: (b, j, 0)),
        out_shape=jax.ShapeDtypeStruct((bsz, seq, d), BF16),
        compiler_params=_params("arbitrary", "arbitrary"),
        name="q_proj",
    )(x, mod, w_q)


def _attn_kernel(q_ref, k_ref, kaug_ref, vt_ref, cq_ref, o_ref, acc_ref, m_ref, l_ref, s_ref):
    tq = q_ref.shape[1]
    tk = vt_ref.shape[4]
    per_q = tq // tk
    head_dim = LANES // HEAD_PAIR
    i = pl.program_id(2)
    q_t = q_ref[0].astype(F32).T
    row = lax.broadcasted_iota(jnp.int32, (LANES, tq), 0)
    rhs = []
    for g in range(PAIR_GROUP):
        q_pair = q_t[LANES * g:LANES * (g + 1), :]
        for hd in range(HEAD_PAIR):
            own = (row >= head_dim * hd) & (row < head_dim * (hd + 1))
            parts = _split3(cq_ref[0, g, hd:hd + 1, :])
            base = AUG_STRIDE * hd
            aug = jnp.where((row >= base) & (row < base + AUG_PARTS), -1.0, 0.0)
            for n, part in enumerate(parts):
                aug = jnp.where(row == base + AUG_PARTS + n, part.astype(F32), aug)
            rhs.append(jnp.concatenate([jnp.where(own, q_pair, 0.0).astype(BF16), aug.astype(BF16)],
                                       axis=0))
    ones_rows = jnp.ones((BF16_ROWS, tk), BF16)
    acc_ref[...] = jnp.zeros_like(acc_ref)
    m_ref[...] = jnp.full_like(m_ref, NEG_INF)
    l_ref[...] = jnp.zeros_like(l_ref)

    def keys_of(j, g):
        start = pl.multiple_of(j * tk, tk)
        return jnp.concatenate([k_ref[0, pl.ds(start, tk), LANES * g:LANES * (g + 1)],
                                kaug_ref[0, g, pl.ds(start, tk), :]], axis=1)

    s_ref[...] = jnp.dot(keys_of(0, 0), rhs[0], preferred_element_type=F32)

    def chunk(j, diagonal, offset=0, has_next=True):
        keys = [keys_of(j, g) for g in range(PAIR_GROUP)]

        def masked(s_t):
            if diagonal:
                k_i = lax.broadcasted_iota(jnp.int32, (tk, tq), 0) + offset
                q_i = lax.broadcasted_iota(jnp.int32, (tk, tq), 1)
                s_t = jnp.where(k_i <= q_i, s_t, NEG_INF)
            return s_t

        n_heads = PAIR_GROUP * HEAD_PAIR
        s_next = s_ref[...]
        for n in range(n_heads):
            g, hd = divmod(n, HEAD_PAIR)
            s_t = masked(s_next)
            if n + 1 < n_heads:
                s_next = jnp.dot(keys[(n + 1) // HEAD_PAIR], rhs[n + 1], preferred_element_type=F32)
            elif has_next:
                s_ref[...] = jnp.dot(keys_of(j + 1, 0), rhs[0], preferred_element_type=F32)
            m_old = m_ref[n:n + 1, :]
            m_new = jnp.maximum(m_old, jnp.max(s_t, axis=0, keepdims=True))
            a = jnp.exp2(m_old - m_new)
            p_t = jnp.exp2(s_t - m_new).astype(BF16)
            m_ref[n:n + 1, :] = m_new
            rows = slice(head_dim * n, head_dim * (n + 1))
            lhs = jnp.concatenate([vt_ref[0, g, j, head_dim * hd:head_dim * (hd + 1), :], ones_rows], axis=0)
            pv = jnp.dot(lhs, p_t, preferred_element_type=F32)
            l_ref[n:n + 1, :] = a * l_ref[n:n + 1, :] + pv[head_dim:head_dim + 1, :]
            acc_ref[rows, :] = acc_ref[rows, :] * a + pv[:head_dim, :]

    def body(j, carry):
        chunk(j, False)
        return carry
    lax.fori_loop(0, i * per_q, body, 0)
    for sub in range(per_q):
        chunk(i * per_q + sub, True, sub * tk, has_next=sub + 1 < per_q)
    inv = jnp.concatenate([jnp.broadcast_to(1.0 / l_ref[n:n + 1, :], (head_dim, tq))
                           for n in range(PAIR_GROUP * HEAD_PAIR)], axis=0)
    o_ref[0] = (acc_ref[...] * inv).T.astype(o_ref.dtype)


def _attention(q, k, kaug, v_t, cq):
    bsz, seq, d = q.shape
    width = PAIR_GROUP * LANES
    tq = ATTN_TILE
    return pl.pallas_call(
        _attn_kernel,
        grid=(bsz, d // width, seq // tq),
        in_specs=[pl.BlockSpec((1, tq, width), lambda b, p, i: (b, i, p)),
                  pl.BlockSpec((1, seq, width), lambda b, p, i: (b, 0, p)),
                  pl.BlockSpec((1, PAIR_GROUP, seq, LANES), lambda b, p, i: (b, p, 0, 0)),
                  pl.BlockSpec((1, PAIR_GROUP) + v_t.shape[2:], lambda b, p, i: (b, p, 0, 0, 0)),
                  pl.BlockSpec((1, PAIR_GROUP, HEAD_PAIR, tq), lambda b, p, i: (b, p, 0, i))],
        out_specs=pl.BlockSpec((1, tq, width), lambda b, p, i: (b, i, p)),
        out_shape=jax.ShapeDtypeStruct((bsz, seq, d), BF16),
        scratch_shapes=[pltpu.VMEM((width, tq), F32),
                        pltpu.VMEM((8, tq), F32),
                        pltpu.VMEM((8, tq), F32),
                        pltpu.VMEM((v_t.shape[4], tq), F32)],
        compiler_params=_params("arbitrary", "arbitrary", "arbitrary"),
        name="fox_attention",
    )(q, k, kaug, v_t, cq)


def _out_proj_kernel(o_ref, x_ref, mod_ref, wo_ref, g_ref, b_ref, out_ref, *, alpha):
    y = jnp.dot(o_ref[0], wo_ref[...], preferred_element_type=F32)
    r = alpha * x_ref[0] + (1.0 + mod_ref[0, 2:3, :]) * y
    out_ref[0] = _layer_norm(r, g_ref[...], b_ref[...])


def _out_proj(o, x, mod, w_o, ln_g, ln_b, alpha):
    bsz, seq, d = x.shape
    ts = SEQ_TILE
    return pl.pallas_call(
        functools.partial(_out_proj_kernel, alpha=alpha),
        grid=(bsz, seq // ts),
        in_specs=[pl.BlockSpec((1, ts, d), lambda b, j: (b, j, 0)),
                  pl.BlockSpec((1, ts, d), lambda b, j: (b, j, 0)),
                  pl.BlockSpec((1, 3, d), lambda b, j: (b, 0, 0)),
                  pl.BlockSpec((d, d), lambda b, j: (0, 0)),
                  pl.BlockSpec((1, d), lambda b, j: (0, 0)),
                  pl.BlockSpec((1, d), lambda b, j: (0, 0))],
        out_specs=pl.BlockSpec((1, ts, d), lambda b, j: (b, j, 0)),
        out_shape=jax.ShapeDtypeStruct(x.shape, F32),
        compiler_params=_params("arbitrary", "arbitrary"),
        name="attn_out_proj",
    )(o, x, mod, w_o, ln_g, ln_b)


def kernel(x, c, conv_w_in, conv_w, conv_w_out, kv_ada_w, kv_ada_b, w_kvf, b_f, attn_w_q, attn_w_o,
           ada_w, ada_b, ln_g, ln_b, router_w, router_b, exp_w_gu, exp_b_gu, exp_w_down, exp_b_down):
    bsz, seq, d = x.shape
    depth = ada_w.shape[0]
    n_conv = conv_w_in.shape[0]
    n_exp = router_w.shape[-1]
    alpha = (2.0 * depth) ** 0.25
    head_dim = d // N_HEADS
    assert head_dim * HEAD_PAIR == LANES and seq % SEQ_TILE == 0 and seq % ATTN_TILE == 0

    c_pad = jnp.pad(c, ((0, 8 - bsz), (0, 0)))
    mods = _ada_params(c_pad, ada_w.reshape(depth * 2, d, 3 * d), ada_b.reshape(depth * 2, 1, 3 * d))
    mods = mods[:, :bsz, :].reshape(depth, 2, bsz, 3, d)
    kv_mod = _ada_params(c_pad, kv_ada_w[None], kv_ada_b[None, None])[0, :bsz].reshape(bsz, 2, d)

    w_r = jnp.pad(router_w, ((0, 0), (0, 0), (0, LANES - n_exp)))
    w_r_hi = w_r.astype(BF16)
    w_r = jnp.concatenate([w_r_hi, (w_r - w_r_hi.astype(F32)).astype(BF16)], axis=-1)
    b_r = jnp.pad(router_b, ((0, 0), (0, LANES - n_exp)), constant_values=-1e30)[:, None, :]
    b_gu = exp_b_gu[:, :, None, :]
    b_dn = exp_b_down[:, :, None, :]
    k = kaug = v_t = cq = None
    for l in range(depth):
        g0, b0 = ln_g[l, 0][None], ln_b[l, 0][None]
        if l < n_conv:
            x = _conv_layer(x, mods[l, 0], conv_w_in[l].astype(BF16), conv_w[l],
                            conv_w_out[l].astype(BF16), g0, b0, alpha)
        else:
            j = l - n_conv
            q = _q_proj(x, mods[l, 0], attn_w_q[j].astype(BF16), head_dim ** -0.5 * LOG2E)
            o = _attention(q, k, kaug, v_t, cq)
            x = _out_proj(o, x, mods[l, 0], attn_w_o[j].astype(BF16), g0, b0, alpha)
        x = _moe_layer(l, x.reshape(bsz * seq, d), mods[l, 1], w_r[l], b_r[l], exp_w_gu, b_gu,
                       exp_w_down, b_dn,
                       ln_g[l, 1][None], ln_b[l, 1][None], alpha, seq).reshape(bsz, seq, d)
        if l == n_conv - 1:
            w_f = jnp.pad(w_kvf[:, 2 * d:], ((0, 0), (0, LANES - N_HEADS))).astype(BF16)
            bias_f = jnp.pad(b_f, (0, LANES - N_HEADS))[None]
            k, kaug, v_t, cum = _shared_kv(x, kv_mod, w_kvf[:, :d].astype(BF16),
                                           w_kvf[:, d:2 * d].astype(BF16), w_f, bias_f)
            cq = cum.reshape(bsz, seq, N_HEADS // HEAD_PAIR, HEAD_PAIR).transpose(0, 2, 3, 1)
    return x
```

```python
import functools

import numpy as np
import jax
import jax.numpy as jnp
from jax import lax
from jax.experimental import pallas as pl
from jax.experimental.pallas import tpu as pltpu

N_HEADS = 16
TOP_K = 4
SWIGLU_LIMIT = 7.0
SWIGLU_ALPHA = 1.702
LN_EPS = 1e-5
LANES = 128
BF16_ROWS = 16
HEAD_PAIR = 2
PAIR_GROUP = 2
LOG2E = 1.4426950408889634
SEQ_TILE = 512
MOE_TILE = 512
ATTN_TILE = 512
ATTN_KEY_TILE = 512
RUN_ALIGN = 8
RUN_COPY = 128
RUN_BITS = 7
VMEM_LIMIT = 56 * 1024 * 1024

F32 = jnp.float32
BF16 = jnp.bfloat16
NEG_INF = float("-inf")


def _params(*sem):
    return pltpu.CompilerParams(dimension_semantics=sem, vmem_limit_bytes=VMEM_LIMIT)


def _layer_norm(r, g, b):
    mu = jnp.mean(r, axis=-1, keepdims=True)
    d = r - mu
    var = jnp.mean(d * d, axis=-1, keepdims=True)
    return d * lax.rsqrt(var + LN_EPS) * g + b


def _modulate(x, mod_ref):
    return x * (1.0 + mod_ref[0, 1:2, :]) + mod_ref[0, 0:1, :]


def _ada_kernel(c_ref, w_ref, b_ref, o_ref):
    c = c_ref[...]
    cond = c * jax.nn.sigmoid(c)
    o_ref[0] = jnp.dot(cond, w_ref[0], precision=lax.Precision.HIGHEST,
                       preferred_element_type=F32) + b_ref[0]


def _ada_params(c_pad, w, b):
    g, d, n = w.shape
    tn = 1024 if n % 1024 == 0 else n
    return pl.pallas_call(
        _ada_kernel,
        grid=(g, n // tn),
        in_specs=[pl.BlockSpec((8, d), lambda i, j: (0, 0)),
                  pl.BlockSpec((1, d, tn), lambda i, j: (i, 0, j)),
                  pl.BlockSpec((1, 1, tn), lambda i, j: (i, 0, j))],
        out_specs=pl.BlockSpec((1, 8, tn), lambda i, j: (i, 0, j)),
        out_shape=jax.ShapeDtypeStruct((g, 8, n), F32),
        compiler_params=_params("arbitrary", "arbitrary"),
        name="ada_params",
    )(c_pad, w, b)


def _conv_layer_kernel(x_ref, mod_ref, win_ref, wc_ref, wout_ref, g_ref, b_ref, o_ref,
                       carry_ref, a_ref, *, alpha, col_chunk):
    ts, d = x_ref.shape[1], x_ref.shape[2]

    @pl.when(pl.program_id(1) == 0)
    def _():
        carry_ref[...] = jnp.zeros_like(carry_ref)

    x = x_ref[0]
    h = _modulate(x, mod_ref).astype(BF16)
    row = lax.broadcasted_iota(jnp.int32, (ts, col_chunk), 0)
    for c in range(0, d, col_chunk):
        gate_c = jnp.dot(h, win_ref[:, c:c + col_chunk], preferred_element_type=F32)
        u = jnp.dot(h, win_ref[:, 2 * d + c:2 * d + c + col_chunk], preferred_element_type=F32)
        z = gate_c * u
        z_m1 = carry_ref[1:2, c:c + col_chunk]
        z_m2 = carry_ref[0:1, c:c + col_chunk]
        z1 = jnp.where(row == 0, z_m1, pltpu.roll(z, 1, 0))
        z2 = jnp.where(row == 0, z_m2, jnp.where(row == 1, z_m1, pltpu.roll(z, 2, 0)))
        carry_ref[0:2, c:c + col_chunk] = z[ts - 2:ts, :]
        conv = (wc_ref[0:1, c:c + col_chunk] * z2 + wc_ref[1:2, c:c + col_chunk] * z1
                + wc_ref[2:3, c:c + col_chunk] * z)
        gate_b = jnp.dot(h, win_ref[:, d + c:d + c + col_chunk], preferred_element_type=F32)
        a_ref[:, c:c + col_chunk] = (gate_b * conv).astype(BF16)
    y = jnp.dot(a_ref[...], wout_ref[...], preferred_element_type=F32)
    r = alpha * x + (1.0 + mod_ref[0, 2:3, :]) * y
    o_ref[0] = _layer_norm(r, g_ref[...], b_ref[...])


def _conv_layer(x, mod, w_in, w_conv, w_out, ln_g, ln_b, alpha):
    bsz, seq, d = x.shape
    ts = SEQ_TILE
    kern = functools.partial(_conv_layer_kernel, alpha=alpha, col_chunk=256)
    return pl.pallas_call(
        kern,
        grid=(bsz, seq // ts),
        in_specs=[pl.BlockSpec((1, ts, d), lambda b, j: (b, j, 0)),
                  pl.BlockSpec((1, 3, d), lambda b, j: (b, 0, 0)),
                  pl.BlockSpec((d, 3 * d), lambda b, j: (0, 0)),
                  pl.BlockSpec((3, d), lambda b, j: (0, 0)),
                  pl.BlockSpec((d, d), lambda b, j: (0, 0)),
                  pl.BlockSpec((1, d), lambda b, j: (0, 0)),
                  pl.BlockSpec((1, d), lambda b, j: (0, 0))],
        out_specs=pl.BlockSpec((1, ts, d), lambda b, j: (b, j, 0)),
        out_shape=jax.ShapeDtypeStruct(x.shape, F32),
        scratch_shapes=[pltpu.VMEM((8, d), F32), pltpu.VMEM((ts, d), BF16)],
        compiler_params=_params("arbitrary", "arbitrary"),
        name="conv_layer",
    )(x, mod, w_in, w_conv, w_out, ln_g, ln_b)


def _pack_bf16_pairs(h):
    half = h.shape[1] // 2
    lo = lax.bitcast_convert_type(h[:, :half].astype(BF16).astype(F32), jnp.int32)
    hi = lax.bitcast_convert_type(h[:, half:].astype(BF16).astype(F32), jnp.int32)
    return lax.shift_right_logical(lo, 16) | hi


def _unpack_bf16_pairs(w):
    lo = lax.bitcast_convert_type(w << 16, F32).astype(BF16)
    hi = lax.bitcast_convert_type(w & jnp.int32(-65536), F32).astype(BF16)
    return lo, hi


def _router_kernel(x_ref, mod_ref, wr_ref, br_ref, h_ref, route_ref, cnt_ref):
    tr = x_ref.shape[0]
    h = _modulate(x_ref[...], mod_ref)
    h_hi = h.astype(BF16)
    h_ref[...] = h_hi
    h_lo = (h - h_hi.astype(F32)).astype(BF16)
    both = jnp.dot(h_hi, wr_ref[...], preferred_element_type=F32)
    logits = (both[:, :LANES] + both[:, LANES:]
              + jnp.dot(h_lo, wr_ref[:, :LANES], preferred_element_type=F32)) + br_ref[...]
    lane = lax.broadcasted_iota(jnp.int32, (tr, LANES), 1)
    lane_f = lane.astype(F32)
    work = logits
    vals, idxs, sels = [], [], []
    for _ in range(TOP_K):
        m = jnp.max(work, axis=-1, keepdims=True)
        idx = jnp.min(jnp.where(work == m, lane_f, float(LANES)), axis=-1, keepdims=True)
        sel = lane_f == idx
        vals.append(m)
        idxs.append(idx.astype(jnp.int32))
        sels.append(sel)
        work = jnp.where(sel, NEG_INF, work)
    exps = [jnp.exp(v - vals[0]) for v in vals]
    denom = exps[0] + exps[1] + exps[2] + exps[3]
    chosen = sels[0] | sels[1] | sels[2] | sels[3]
    onehot = jnp.where(chosen, 1.0, 0.0).astype(BF16)
    r_i = lax.broadcasted_iota(jnp.int32, (tr, tr), 0)
    c_i = lax.broadcasted_iota(jnp.int32, (tr, tr), 1)
    tri = jnp.where(c_i < r_i, 1.0, 0.0).astype(BF16)
    before = jnp.dot(tri, onehot, preferred_element_type=F32)
    out = jnp.zeros((tr, LANES), jnp.int32)
    for k in range(TOP_K):
        rank = jnp.sum(jnp.where(sels[k], before, 0.0), axis=-1, keepdims=True).astype(jnp.int32)
        wgt = lax.bitcast_convert_type(exps[k] / denom, jnp.int32)
        out = jnp.where(lane == k, idxs[k], out)
        out = jnp.where(lane == TOP_K + k, rank, out)
        out = jnp.where(lane == 2 * TOP_K + k, wgt, out)
    route_ref[...] = out
    counts = jnp.sum(onehot.astype(F32), axis=0, keepdims=True).astype(jnp.int32)
    cnt_ref[0] = jnp.broadcast_to(counts, (8, LANES))


def _router(x2, mod, w_r, b_r, seq):
    t, d = x2.shape
    tr = SEQ_TILE
    per_b = seq // tr
    return pl.pallas_call(
        _router_kernel,
        grid=(t // tr,),
        in_specs=[pl.BlockSpec((tr, d), lambda i: (i, 0)),
                  pl.BlockSpec((1, 3, d), lambda i: (i // per_b, 0, 0)),
                  pl.BlockSpec((d, 2 * LANES), lambda i: (0, 0)),
                  pl.BlockSpec((1, LANES), lambda i: (0, 0))],
        out_specs=[pl.BlockSpec((tr, d), lambda i: (i, 0)),
                   pl.BlockSpec((tr, LANES), lambda i: (i, 0)),
                   pl.BlockSpec((1, 8, LANES), lambda i: (i, 0, 0))],
        out_shape=[jax.ShapeDtypeStruct((t, d), BF16),
                   jax.ShapeDtypeStruct((t, LANES), jnp.int32),
                   jax.ShapeDtypeStruct((t // tr, 8, LANES), jnp.int32)],
        compiler_params=_params("arbitrary"),
        name="moe_router",
    )(x2, mod, w_r, b_r)


def _dispatch_kernel(meta_ref, prev_ref, fill_ref, offdst_ref, route_ref, h_ref, xs_hbm, pos_ref, xs_buf, sem,
                     *, n_exp):
    tr, d = h_ref.shape
    n_rows = xs_buf.shape[1] - RUN_COPY
    i = pl.program_id(0)
    buf = i % 2
    sizes = [RUN_ALIGN << b for b in range(RUN_BITS)]

    @pl.when(i == 0)
    def _():
        xs_buf[0, n_rows:, :] = jnp.zeros((RUN_COPY, d // 2), jnp.int32)
        xs_buf[1] = jnp.zeros(xs_buf.shape[1:], jnp.int32)
        for e in range(n_exp):
            start = pl.multiple_of(fill_ref[0, 0, e], RUN_ALIGN)
            length = fill_ref[0, 0, LANES // 4 + e]
            for size in reversed(sizes):
                done = pl.multiple_of((length // (2 * size)) * (2 * size), RUN_ALIGN)

                @pl.when((length & size) != 0)
                def _():
                    pltpu.make_async_copy(xs_buf.at[1, pl.ds(0, size)],
                                          xs_hbm.at[pl.ds(start + done, size)], sem).start()
        spare = pl.multiple_of(fill_ref[0, 0, LANES // 2], tr)

        def fill_spare(n, carry):
            pltpu.make_async_copy(xs_buf.at[1, pl.ds(0, tr)],
                                  xs_hbm.at[pl.ds(pl.multiple_of(spare + n * tr, tr), tr)], sem).start()
            return carry
        lax.fori_loop(0, fill_ref[0, 0, LANES // 2 + 1], fill_spare, 0)
        for b, size in enumerate(sizes):
            def wait_piece(n, carry, size=size):
                pltpu.make_async_copy(xs_buf.at[1, pl.ds(0, size)], xs_hbm.at[pl.ds(0, size)], sem).wait()
                return carry
            lax.fori_loop(0, fill_ref[0, 0, LANES // 2 + 2 + b], wait_piece, 0)

        def wait_spare(n, carry):
            pltpu.make_async_copy(xs_buf.at[1, pl.ds(0, tr)], xs_hbm.at[pl.ds(0, tr)], sem).wait()
            return carry
        lax.fori_loop(0, fill_ref[0, 0, LANES // 2 + 1], wait_spare, 0)

    route = route_ref[...]
    lane = lax.broadcasted_iota(jnp.int32, (tr, LANES), 1)
    off_row = offdst_ref[0, 0:1, :]
    pos = jnp.full((tr, LANES), -1.0, F32)
    pos_out = jnp.zeros((tr, LANES), jnp.int32)
    for k in range(TOP_K):
        sel = lane == route[:, k:k + 1]
        rank = route[:, TOP_K + k:TOP_K + k + 1].astype(F32)
        pos_k = jnp.sum(jnp.where(sel, off_row, 0.0), axis=-1, keepdims=True) + rank
        pos = jnp.where(lane == k, pos_k, pos)
        pos_out = jnp.where(lane == k, pos_k.astype(jnp.int32), pos_out)
    pos_ref[...] = pos_out
    pos_t = pos.T
    h = h_ref[...]
    chunk = n_rows // 4
    for c in range(4):
        slot = (lax.broadcasted_iota(jnp.int32, (chunk, tr), 0) + chunk * c).astype(F32)
        hit = slot == pos_t[0:1, :]
        for k in range(1, TOP_K):
            hit = hit | (slot == pos_t[k:k + 1, :])
        onehot = jnp.where(hit, 1.0, 0.0).astype(BF16)
        rows = jnp.dot(onehot, h, preferred_element_type=F32)
        xs_buf[buf, chunk * c:chunk * (c + 1), :] = _pack_bf16_pairs(rows)

    def wait_copies(count):
        def wait_one(n, carry):
            pltpu.make_async_copy(xs_buf.at[0, pl.ds(0, RUN_COPY)], xs_hbm.at[pl.ds(0, RUN_COPY)], sem).wait()
            return carry
        lax.fori_loop(0, count, wait_one, 0)

    @pl.when(i > 0)
    def _():
        wait_copies(prev_ref[0, 0, 3 * LANES // 4])

    for e in range(n_exp):
        src = pl.multiple_of(meta_ref[0, 0, e], RUN_ALIGN)
        dst = pl.multiple_of(meta_ref[0, 0, LANES // 4 + e], RUN_ALIGN)
        length = meta_ref[0, 0, LANES // 2 + e]
        for q in range(tr // RUN_COPY):
            @pl.when(length > RUN_COPY * q)
            def _():
                pltpu.make_async_copy(xs_buf.at[buf, pl.ds(src + RUN_COPY * q, RUN_COPY)],
                                      xs_hbm.at[pl.ds(dst + RUN_COPY * q, RUN_COPY)], sem).start()

    @pl.when(i == pl.num_programs(0) - 1)
    def _():
        wait_copies(meta_ref[0, 0, 3 * LANES // 4])


def _dispatch(meta, fill, offdst, route, h, n_slots, n_exp):
    t, d = h.shape
    tr = SEQ_TILE
    n_rows = TOP_K * tr + n_exp * RUN_ALIGN
    return pl.pallas_call(
        functools.partial(_dispatch_kernel, n_exp=n_exp),
        grid=(t // tr,),
        in_specs=[pl.BlockSpec((1, 1, LANES), lambda i: (i, 0, 0), memory_space=pltpu.SMEM),
                  pl.BlockSpec((1, 1, LANES), lambda i: (jnp.maximum(i - 1, 0), 0, 0), memory_space=pltpu.SMEM),
                  pl.BlockSpec((1, 1, LANES), lambda i: (0, 0, 0), memory_space=pltpu.SMEM),
                  pl.BlockSpec((1, 8, LANES), lambda i: (i, 0, 0)),
                  pl.BlockSpec((tr, LANES), lambda i: (i, 0)),
                  pl.BlockSpec((tr, d), lambda i: (i, 0))],
        out_specs=[pl.BlockSpec(memory_space=pl.ANY),
                   pl.BlockSpec((tr, LANES), lambda i: (i, 0))],
        out_shape=[jax.ShapeDtypeStruct((n_slots, d // 2), jnp.int32),
                   jax.ShapeDtypeStruct((t, LANES), jnp.int32)],
        scratch_shapes=[pltpu.VMEM((2, n_rows + RUN_COPY, d // 2), jnp.int32), pltpu.SemaphoreType.DMA(())],
        compiler_params=_params("arbitrary"),
        name="moe_dispatch",
    )(meta, meta, fill, offdst, route, h)


def _moe_gemm_kernel(te_ref, valid_ref, xs_ref, wgu_ref, bgu_ref, wd_ref, bd_ref, y_ref, wgu_bf, wd_bf):
    half = xs_ref.shape[1]
    f = wd_bf.shape[0]
    i = pl.program_id(0)

    @pl.when((i == 0) | (te_ref[i] != te_ref[jnp.maximum(i - 1, 0)]))
    def _():
        wgu_bf[...] = wgu_ref[0, 0].astype(BF16)
        wd_bf[...] = wd_ref[0, 0].astype(BF16)

    @pl.when(valid_ref[i] == 1)
    def _():
        lo, hi = _unpack_bf16_pairs(xs_ref[...])
        gu = (jnp.dot(lo, wgu_bf[:half, :], preferred_element_type=F32)
              + jnp.dot(hi, wgu_bf[half:, :], preferred_element_type=F32) + bgu_ref[0, 0])
        g = jnp.minimum(gu[:, :f], SWIGLU_LIMIT)
        u = jnp.clip(gu[:, f:], -SWIGLU_LIMIT, SWIGLU_LIMIT)
        a = g * jax.nn.sigmoid(SWIGLU_ALPHA * g) * (u + 1.0)
        y = jnp.dot(a.astype(BF16), wd_bf[...], preferred_element_type=F32) + bd_ref[0, 0]
        y_ref[...] = _pack_bf16_pairs(y)

    @pl.when(valid_ref[i] == 0)
    def _():
        y_ref[...] = jnp.zeros_like(y_ref)


def _moe_gemm(layer, tile_expert, tile_valid, xs, w_gu, b_gu, w_down, b_down):
    n_tiles = tile_expert.shape[0]
    tm = MOE_TILE
    _, _, d, f2 = w_gu.shape
    f = f2 // 2
    grid_spec = pltpu.PrefetchScalarGridSpec(
        num_scalar_prefetch=2,
        grid=(n_tiles,),
        in_specs=[pl.BlockSpec((tm, d // 2), lambda i, te, tv: (i, 0)),
                  pl.BlockSpec((1, 1, d, f2), lambda i, te, tv: (layer, te[i], 0, 0)),
                  pl.BlockSpec((1, 1, 1, f2), lambda i, te, tv: (layer, te[i], 0, 0)),
                  pl.BlockSpec((1, 1, f, d), lambda i, te, tv: (layer, te[i], 0, 0)),
                  pl.BlockSpec((1, 1, 1, d), lambda i, te, tv: (layer, te[i], 0, 0))],
        out_specs=pl.BlockSpec((tm, d // 2), lambda i, te, tv: (i, 0)),
        scratch_shapes=[pltpu.VMEM((d, f2), BF16), pltpu.VMEM((f, d), BF16)],
    )
    return pl.pallas_call(
        _moe_gemm_kernel,
        grid_spec=grid_spec,
        out_shape=jax.ShapeDtypeStruct((n_tiles * tm, d // 2), jnp.int32),
        compiler_params=_params("arbitrary"),
        name="moe_experts",
    )(tile_expert, tile_valid, xs, w_gu, b_gu, w_down, b_down)


def _combine_kernel(meta_ref, y_hbm, pos_ref, x_ref, mod_ref, route_ref, g_ref, b_ref, o_ref, ybuf, sem,
                    *, alpha, n_exp):
    tr = x_ref.shape[0]
    n_rows = ybuf.shape[0]
    sizes = [RUN_ALIGN << b for b in range(RUN_BITS)]

    @pl.when(pl.program_id(0) == 0)
    def _():
        ybuf[...] = jnp.zeros_like(ybuf)

    for e in range(n_exp):
        off = pl.multiple_of(meta_ref[0, 0, e], RUN_ALIGN)
        src = pl.multiple_of(meta_ref[0, 0, LANES // 4 + e], RUN_ALIGN)
        length = meta_ref[0, 0, LANES // 2 + e]
        for size in reversed(sizes):
            done = pl.multiple_of((length // (2 * size)) * (2 * size), RUN_ALIGN)

            @pl.when((length & size) != 0)
            def _():
                pltpu.make_async_copy(y_hbm.at[pl.ds(src + done, size)],
                                      ybuf.at[pl.ds(off + done, size)], sem).start()

    pos = pos_ref[...]
    wts = lax.bitcast_convert_type(route_ref[...], F32)
    n_chunks = 3
    width = n_rows // n_chunks
    spread = []
    for c in range(n_chunks):
        slot = lax.broadcasted_iota(jnp.int32, (tr, width), 1) + width * c
        w_c = jnp.zeros((tr, width), F32)
        for k in range(TOP_K):
            w_c = jnp.where(slot == pos[:, k:k + 1], wts[:, 2 * TOP_K + k:2 * TOP_K + k + 1], w_c)
        spread.append(w_c.astype(BF16))

    for b, size in enumerate(sizes):
        def wait_one(n, carry, size=size):
            pltpu.make_async_copy(y_hbm.at[pl.ds(0, size)], ybuf.at[pl.ds(0, size)], sem).wait()
            return carry
        lax.fori_loop(0, meta_ref[0, 0, 3 * LANES // 4 + 1 + b], wait_one, 0)

    sub_lo = sub_hi = None
    for c in range(n_chunks):
        lo, hi = _unpack_bf16_pairs(ybuf[width * c:width * (c + 1), :])
        part_lo = jnp.dot(spread[c], lo, preferred_element_type=F32)
        part_hi = jnp.dot(spread[c], hi, preferred_element_type=F32)
        sub_lo = part_lo if sub_lo is None else sub_lo + part_lo
        sub_hi = part_hi if sub_hi is None else sub_hi + part_hi
    sub = jnp.concatenate([sub_lo, sub_hi], axis=1)
    x = x_ref[...]
    r = alpha * x + (1.0 + mod_ref[0, 2:3, :]) * sub
    o_ref[...] = _layer_norm(r, g_ref[...], b_ref[...])


def _combine(meta, y, pos, x2, mod, route, ln_g, ln_b, alpha, seq, n_exp):
    t, d = x2.shape
    tr = SEQ_TILE
    per_b = seq // tr
    n_rows = TOP_K * tr + n_exp * RUN_ALIGN
    kern = functools.partial(_combine_kernel, alpha=alpha, n_exp=n_exp)
    return pl.pallas_call(
        kern,
        grid=(t // tr,),
        in_specs=[pl.BlockSpec((1, 1, LANES), lambda i: (i, 0, 0), memory_space=pltpu.SMEM),
                  pl.BlockSpec(memory_space=pl.ANY),
                  pl.BlockSpec((tr, LANES), lambda i: (i, 0)),
                  pl.BlockSpec((tr, d), lambda i: (i, 0)),
                  pl.BlockSpec((1, 3, d), lambda i: (i // per_b, 0, 0)),
                  pl.BlockSpec((tr, LANES), lambda i: (i, 0)),
                  pl.BlockSpec((1, d), lambda i: (0, 0)),
                  pl.BlockSpec((1, d), lambda i: (0, 0))],
        out_specs=pl.BlockSpec((tr, d), lambda i: (i, 0)),
        out_shape=jax.ShapeDtypeStruct((t, d), F32),
        scratch_shapes=[pltpu.VMEM((n_rows, d // 2), jnp.int32), pltpu.SemaphoreType.DMA(())],
        compiler_params=_params("arbitrary"),
        name="moe_combine",
    )(meta, y, pos, x2, mod, route, ln_g, ln_b)


def _moe_layer(layer, x2, mod, w_r, b_r, w_gu, b_gu, w_down, b_down, ln_g, ln_b, alpha, seq):
    t, d = x2.shape
    n_exp = w_gu.shape[1]
    tm = MOE_TILE
    n_tok_tiles = t // SEQ_TILE
    assert n_exp <= LANES // 4 and SEQ_TILE % RUN_COPY == 0 and SEQ_TILE == tm
    assert RUN_ALIGN << (RUN_BITS - 1) == SEQ_TILE and RUN_COPY + tm - 1 < 2 * SEQ_TILE
    h, route, cnt = _router(x2, mod, w_r, b_r, seq)
    runs = (cnt[:, 0, :n_exp] + RUN_ALIGN - 1) // RUN_ALIGN * RUN_ALIGN
    run_off = jnp.cumsum(runs, axis=1) - runs
    run_base = jnp.cumsum(runs, axis=0) - runs
    rows = jnp.sum(runs, axis=0)
    region = (rows + RUN_COPY + tm - 1) // tm * tm
    region_end = jnp.cumsum(region)
    region_start = region_end - region
    run_dst = region_start[None, :] + run_base
    n_tiles = -(-(t * TOP_K + n_tok_tiles * n_exp * (RUN_ALIGN - 1) + n_exp * (RUN_COPY + tm - 1)) // tm)
    tile_start = jnp.arange(n_tiles, dtype=jnp.int32) * tm
    tile_expert = jnp.minimum(jnp.sum(region_end[None, :] <= tile_start[:, None], axis=1),
                              n_exp - 1).astype(jnp.int32)
    tile_valid = ((tile_start - region_start[tile_expert] < rows[tile_expert])
                  & (tile_start < region_end[-1])).astype(jnp.int32)
    pad = jnp.zeros((n_tok_tiles, LANES // 4 - n_exp), jnp.int32)
    n_copies = jnp.sum((runs + RUN_COPY - 1) // RUN_COPY, axis=1, keepdims=True)
    pieces = jnp.stack([jnp.sum((runs // (RUN_ALIGN << b)) % 2, axis=1) for b in range(RUN_BITS)], axis=1)
    meta = jnp.concatenate([run_off, pad, run_dst, pad, runs, pad, n_copies, pieces,
                            jnp.zeros((n_tok_tiles, LANES // 4 - 1 - RUN_BITS), jnp.int32)],
                           axis=1).astype(jnp.int32)[:, None, :]
    wide = lambda v: jnp.pad(v.astype(F32), ((0, 0), (0, LANES - n_exp)))
    offdst = jnp.concatenate([wide(run_off)[:, None], wide(run_dst)[:, None],
                              jnp.zeros((n_tok_tiles, 6, LANES), F32)], axis=1)
    tail_len = region - rows
    tail_pieces = jnp.stack([jnp.sum((tail_len // (RUN_ALIGN << b)) % 2) for b in range(RUN_BITS)])
    spare = jnp.stack([region_end[-1], (n_tiles * tm - region_end[-1]) // SEQ_TILE])
    fill = jnp.concatenate([region_start + rows, pad[0], tail_len, pad[0], spare, tail_pieces,
                            jnp.zeros((LANES // 2 - 2 - RUN_BITS,), jnp.int32)]).astype(jnp.int32)
    xs, pos = _dispatch(meta, fill[None, None, :], offdst, route, h, n_tiles * tm, n_exp)
    y = _moe_gemm(layer, tile_expert, tile_valid, xs, w_gu, b_gu, w_down, b_down)
    return _combine(meta, y, pos, x2, mod, route, ln_g, ln_b, alpha, seq, n_exp)


def _split3(v):
    p0 = v.astype(BF16)
    r1 = v - p0.astype(F32)
    p1 = r1.astype(BF16)
    p2 = (r1 - p1.astype(F32)).astype(BF16)
    return p0, p1, p2


AUG_STRIDE = 8
AUG_PARTS = 3


def _aug_constants(d):
    n_pairs = d // LANES
    sel = np.zeros((AUG_PARTS, LANES, d), np.float32)
    ones = np.zeros((1, d), np.float32)
    for p in range(n_pairs):
        for hd in range(HEAD_PAIR):
            base = LANES * p + AUG_STRIDE * hd
            for part in range(AUG_PARTS):
                sel[part, HEAD_PAIR * p + hd, base + part] = 1.0
                ones[0, base + AUG_PARTS + part] = 1.0
    return jnp.asarray(sel, BF16), jnp.asarray(ones, F32)


def _kv_kernel(x_ref, mod_ref, wk_ref, wv_ref, wf_ref, bf_ref, sel_ref, ones_ref,
               k_ref, kaug_ref, vt_ref, cum_ref, carry_ref):
    ts = x_ref.shape[1]
    n_heads = cum_ref.shape[2]
    n_pairs = kaug_ref.shape[1]

    @pl.when(pl.program_id(1) == 0)
    def _():
        carry_ref[...] = jnp.zeros_like(carry_ref)

    h = _modulate(x_ref[0], mod_ref).astype(BF16)
    k_ref[0] = jnp.dot(h, wk_ref[...], preferred_element_type=F32).astype(BF16)
    vt = jnp.dot(h, wv_ref[...], preferred_element_type=F32).T.astype(BF16)
    tk = vt_ref.shape[4]
    for p in range(n_pairs):
        for c in range(ts // tk):
            vt_ref[0, p, c] = vt[LANES * p:LANES * (p + 1), tk * c:tk * (c + 1)]
    fz = jnp.dot(h, wf_ref[...], preferred_element_type=F32) + bf_ref[...]
    log_f = jnp.minimum(fz, 0.0) - jnp.log1p(jnp.exp(-jnp.abs(fz)))
    r_i = lax.broadcasted_iota(jnp.int32, (ts, ts), 0)
    c_i = lax.broadcasted_iota(jnp.int32, (ts, ts), 1)
    tri = jnp.where(c_i <= r_i, 1.0, 0.0).astype(BF16)
    cum = carry_ref[0:1, :]
    for part in _split3(log_f):
        cum = cum + jnp.dot(tri, part, preferred_element_type=F32)
    carry_ref[0:1, :] = cum[ts - 1:ts, :]
    cum = cum * LOG2E
    cum_ref[0] = cum[:, :n_heads]
    aug = ones_ref[...]
    for i, part in enumerate(_split3(cum)):
        aug = aug + jnp.dot(part, sel_ref[i], preferred_element_type=F32)
    aug = aug.astype(BF16)
    for p in range(n_pairs):
        kaug_ref[0, p] = aug[:, LANES * p:LANES * (p + 1)]


def _shared_kv(x, mod, w_k, w_v, w_f, b_f):
    bsz, seq, d = x.shape
    ts = ATTN_TILE
    tk = ATTN_KEY_TILE
    n_pairs = d // LANES
    sel, ones = _aug_constants(d)
    return pl.pallas_call(
        _kv_kernel,
        grid=(bsz, seq // ts),
        in_specs=[pl.BlockSpec((1, ts, d), lambda b, j: (b, j, 0)),
                  pl.BlockSpec((1, 2, d), lambda b, j: (b, 0, 0)),
                  pl.BlockSpec((d, d), lambda b, j: (0, 0)),
                  pl.BlockSpec((d, d), lambda b, j: (0, 0)),
                  pl.BlockSpec((d, LANES), lambda b, j: (0, 0)),
                  pl.BlockSpec((1, LANES), lambda b, j: (0, 0)),
                  pl.BlockSpec((AUG_PARTS, LANES, d), lambda b, j: (0, 0, 0)),
                  pl.BlockSpec((1, d), lambda b, j: (0, 0))],
        out_specs=[pl.BlockSpec((1, ts, d), lambda b, j: (b, j, 0)),
                   pl.BlockSpec((1, n_pairs, ts, LANES), lambda b, j: (b, 0, j, 0)),
                   pl.BlockSpec((1, n_pairs, ts // tk, LANES, tk), lambda b, j: (b, 0, j, 0, 0)),
                   pl.BlockSpec((1, ts, N_HEADS), lambda b, j: (b, j, 0))],
        out_shape=[jax.ShapeDtypeStruct((bsz, seq, d), BF16),
                   jax.ShapeDtypeStruct((bsz, n_pairs, seq, LANES), BF16),
                   jax.ShapeDtypeStruct((bsz, n_pairs, seq // tk, LANES, tk), BF16),
                   jax.ShapeDtypeStruct((bsz, seq, N_HEADS), F32)],
        scratch_shapes=[pltpu.VMEM((8, LANES), F32)],
        compiler_params=_params("arbitrary", "arbitrary"),
        name="shared_kv",
    )(x, mod, w_k, w_v, w_f, b_f, sel, ones)


def _q_proj_kernel(x_ref, mod_ref, wq_ref, q_ref, *, scale):
    h = _modulate(x_ref[0], mod_ref).astype(BF16)
    q_ref[0] = (jnp.dot(h, wq_ref[...], preferred_element_type=F32) * scale).astype(BF16)


def _q_proj(x, mod, w_q, scale):
    bsz, seq, d = x.shape
    ts = SEQ_TILE
    return pl.pallas_call(
        functools.partial(_q_proj_kernel, scale=scale),
        grid=(bsz, seq // ts),
        in_specs=[pl.BlockSpec((1, ts, d), lambda b, j: (b, j, 0)),
                  pl.BlockSpec((1, 3, d), lambda b, j: (b, 0, 0)),
                  pl.BlockSpec((d, d), lambda b, j: (0, 0))],
        out_specs=pl.BlockSpec((1, ts, d), lambda b, j: (b, j, 0)),
        out_shape=jax.ShapeDtypeStruct((bsz, seq, d), BF16),
        compiler_params=_params("arbitrary", "arbitrary"),
        name="q_proj",
    )(x, mod, w_q)


def _attn_kernel(q_ref, k_ref, kaug_ref, vt_ref, cq_ref, o_ref, acc_ref, m_ref, l_ref, s_ref):
    tq = q_ref.shape[1]
    tk = vt_ref.shape[4]
    per_q = tq // tk
    head_dim = LANES // HEAD_PAIR
    i = pl.program_id(2)
    q_t = q_ref[0].astype(F32).T
    row = lax.broadcasted_iota(jnp.int32, (LANES, tq), 0)
    rhs = []
    for g in range(PAIR_GROUP):
        q_pair = q_t[LANES * g:LANES * (g + 1), :]
        for hd in range(HEAD_PAIR):
            own = (row >= head_dim * hd) & (row < head_dim * (hd + 1))
            parts = _split3(cq_ref[0, g, hd:hd + 1, :])
            base = AUG_STRIDE * hd
            aug = jnp.where((row >= base) & (row < base + AUG_PARTS), -1.0, 0.0)
            for n, part in enumerate(parts):
                aug = jnp.where(row == base + AUG_PARTS + n, part.astype(F32), aug)
            rhs.append(jnp.concatenate([jnp.where(own, q_pair, 0.0).astype(BF16), aug.astype(BF16)],
                                       axis=0))
    ones_rows = jnp.ones((BF16_ROWS, tk), BF16)
    acc_ref[...] = jnp.zeros_like(acc_ref)
    m_ref[...] = jnp.full_like(m_ref, NEG_INF)
    l_ref[...] = jnp.zeros_like(l_ref)

    def keys_of(j, g):
        start = pl.multiple_of(j * tk, tk)
        return jnp.concatenate([k_ref[0, pl.ds(start, tk), LANES * g:LANES * (g + 1)],
                                kaug_ref[0, g, pl.ds(start, tk), :]], axis=1)

    s_ref[...] = jnp.dot(keys_of(0, 0), rhs[0], preferred_element_type=F32)

    def chunk(j, diagonal, offset=0, has_next=True):
        keys = [keys_of(j, g) for g in range(PAIR_GROUP)]

        def masked(s_t):
            if diagonal:
                k_i = lax.broadcasted_iota(jnp.int32, (tk, tq), 0) + offset
                q_i = lax.broadcasted_iota(jnp.int32, (tk, tq), 1)
                s_t = jnp.where(k_i <= q_i, s_t, NEG_INF)
            return s_t

        n_heads = PAIR_GROUP * HEAD_PAIR
        ahead = [s_ref[...], jnp.dot(keys[1 // HEAD_PAIR], rhs[1], preferred_element_type=F32)]
        for n in range(n_heads):
            g, hd = divmod(n, HEAD_PAIR)
            s_t = masked(ahead.pop(0))
            if n + 2 < n_heads:
                ahead.append(jnp.dot(keys[(n + 2) // HEAD_PAIR], rhs[n + 2], preferred_element_type=F32))
            elif n + 2 == n_heads and has_next:
                s_ref[...] = jnp.dot(keys_of(j + 1, 0), rhs[0], preferred_element_type=F32)
            m_old = m_ref[n:n + 1, :]
            m_new = jnp.maximum(m_old, jnp.max(s_t, axis=0, keepdims=True))
            a = jnp.exp2(m_old - m_new)
            p_t = jnp.exp2(s_t - m_new).astype(BF16)
            m_ref[n:n + 1, :] = m_new
            rows = slice(head_dim * n, head_dim * (n + 1))
            lhs = jnp.concatenate([vt_ref[0, g, j, head_dim * hd:head_dim * (hd + 1), :], ones_rows], axis=0)
            pv = jnp.dot(lhs, p_t, preferred_element_type=F32)
            l_ref[n:n + 1, :] = a * l_ref[n:n + 1, :] + pv[head_dim:head_dim + 1, :]
            acc_ref[rows, :] = acc_ref[rows, :] * a + pv[:head_dim, :]

    def body(j, carry):
        chunk(j, False)
        return carry
    lax.fori_loop(0, i * per_q, body, 0)
    for sub in range(per_q):
        chunk(i * per_q + sub, True, sub * tk, has_next=sub + 1 < per_q)
    inv = jnp.concatenate([jnp.broadcast_to(1.0 / l_ref[n:n + 1, :], (head_dim, tq))
                           for n in range(PAIR_GROUP * HEAD_PAIR)], axis=0)
    o_ref[0] = (acc_ref[...] * inv).T.astype(o_ref.dtype)


def _attention(q, k, kaug, v_t, cq):
    bsz, seq, d = q.shape
    width = PAIR_GROUP * LANES
    tq = ATTN_TILE
    return pl.pallas_call(
        _attn_kernel,
        grid=(bsz, d // width, seq // tq),
        in_specs=[pl.BlockSpec((1, tq, width), lambda b, p, i: (b, i, p)),
                  pl.BlockSpec((1, seq, width), lambda b, p, i: (b, 0, p)),
                  pl.BlockSpec((1, PAIR_GROUP, seq, LANES), lambda b, p, i: (b, p, 0, 0)),
                  pl.BlockSpec((1, PAIR_GROUP) + v_t.shape[2:], lambda b, p, i: (b, p, 0, 0, 0)),
                  pl.BlockSpec((1, PAIR_GROUP, HEAD_PAIR, tq), lambda b, p, i: (b, p, 0, i))],
        out_specs=pl.BlockSpec((1, tq, width), lambda b, p, i: (b, i, p)),
        out_shape=jax.ShapeDtypeStruct((bsz, seq, d), BF16),
        scratch_shapes=[pltpu.VMEM((width, tq), F32),
                        pltpu.VMEM((8, tq), F32),
                        pltpu.VMEM((8, tq), F32),
                        pltpu.VMEM((v_t.shape[4], tq), F32)],
        compiler_params=_params("arbitrary", "arbitrary", "arbitrary"),
        name="fox_attention",
    )(q, k, kaug, v_t, cq)


def _out_proj_kernel(o_ref, x_ref, mod_ref, wo_ref, g_ref, b_ref, out_ref, *, alpha):
    y = jnp.dot(o_ref[0], wo_ref[...], preferred_element_type=F32)
    r = alpha * x_ref[0] + (1.0 + mod_ref[0, 2:3, :]) * y
    out_ref[0] = _layer_norm(r, g_ref[...], b_ref[...])


def _out_proj(o, x, mod, w_o, ln_g, ln_b, alpha):
    bsz, seq, d = x.shape
    ts = SEQ_TILE
    return pl.pallas_call(
        functools.partial(_out_proj_kernel, alpha=alpha),
        grid=(bsz, seq // ts),
        in_specs=[pl.BlockSpec((1, ts, d), lambda b, j: (b, j, 0)),
                  pl.BlockSpec((1, ts, d), lambda b, j: (b, j, 0)),
                  pl.BlockSpec((1, 3, d), lambda b, j: (b, 0, 0)),
                  pl.BlockSpec((d, d), lambda b, j: (0, 0)),
                  pl.BlockSpec((1, d), lambda b, j: (0, 0)),
                  pl.BlockSpec((1, d), lambda b, j: (0, 0))],
        out_specs=pl.BlockSpec((1, ts, d), lambda b, j: (b, j, 0)),
        out_shape=jax.ShapeDtypeStruct(x.shape, F32),
        compiler_params=_params("arbitrary", "arbitrary"),
        name="attn_out_proj",
    )(o, x, mod, w_o, ln_g, ln_b)


def kernel(x, c, conv_w_in, conv_w, conv_w_out, kv_ada_w, kv_ada_b, w_kvf, b_f, attn_w_q, attn_w_o,
           ada_w, ada_b, ln_g, ln_b, router_w, router_b, exp_w_gu, exp_b_gu, exp_w_down, exp_b_down):
    bsz, seq, d = x.shape
    depth = ada_w.shape[0]
    n_conv = conv_w_in.shape[0]
    n_exp = router_w.shape[-1]
    alpha = (2.0 * depth) ** 0.25
    head_dim = d // N_HEADS
    assert head_dim * HEAD_PAIR == LANES and seq % SEQ_TILE == 0 and seq % ATTN_TILE == 0

    c_pad = jnp.pad(c, ((0, 8 - bsz), (0, 0)))
    mods = _ada_params(c_pad, ada_w.reshape(depth * 2, d, 3 * d), ada_b.reshape(depth * 2, 1, 3 * d))
    mods = mods[:, :bsz, :].reshape(depth, 2, bsz, 3, d)
    kv_mod = _ada_params(c_pad, kv_ada_w[None], kv_ada_b[None, None])[0, :bsz].reshape(bsz, 2, d)

    w_r = jnp.pad(router_w, ((0, 0), (0, 0), (0, LANES - n_exp)))
    w_r_hi = w_r.astype(BF16)
    w_r = jnp.concatenate([w_r_hi, (w_r - w_r_hi.astype(F32)).astype(BF16)], axis=-1)
    b_r = jnp.pad(router_b, ((0, 0), (0, LANES - n_exp)), constant_values=-1e30)[:, None, :]
    b_gu = exp_b_gu[:, :, None, :]
    b_dn = exp_b_down[:, :, None, :]
    k = kaug = v_t = cq = None
    for l in range(depth):
        g0, b0 = ln_g[l, 0][None], ln_b[l, 0][None]
        if l < n_conv:
            x = _conv_layer(x, mods[l, 0], conv_w_in[l].astype(BF16), conv_w[l],
                            conv_w_out[l].astype(BF16), g0, b0, alpha)
        else:
            j = l - n_conv
            q = _q_proj(x, mods[l, 0], attn_w_q[j].astype(BF16), head_dim ** -0.5 * LOG2E)
            o = _attention(q, k, kaug, v_t, cq)
            x = _out_proj(o, x, mods[l, 0], attn_w_o[j].astype(BF16), g0, b0, alpha)
        x = _moe_layer(l, x.reshape(bsz * seq, d), mods[l, 1], w_r[l], b_r[l], exp_w_gu, b_gu,
                       exp_w_down, b_dn,
                       ln_g[l, 1][None], ln_b[l, 1][None], alpha, seq).reshape(bsz, seq, d)
        if l == n_conv - 1:
            w_f = jnp.pad(w_kvf[:, 2 * d:], ((0, 0), (0, LANES - N_HEADS))).astype(BF16)
            bias_f = jnp.pad(b_f, (0, LANES - N_HEADS))[None]
            k, kaug, v_t, cum = _shared_kv(x, kv_mod, w_kvf[:, :d].astype(BF16),
                                           w_kvf[:, d:2 * d].astype(BF16), w_f, bias_f)
            cq = cum.reshape(bsz, seq, N_HEADS // HEAD_PAIR, HEAD_PAIR).transpose(0, 2, 3, 1)
    return x
```

```python
import functools

import numpy as np
import jax
import jax.numpy as jnp
from jax import lax
from jax.experimental import pallas as pl
from jax.experimental.pallas import tpu as pltpu

N_HEADS = 16
TOP_K = 4
SWIGLU_LIMIT = 7.0
SWIGLU_ALPHA = 1.702
LN_EPS = 1e-5
LANES = 128
BF16_ROWS = 16
HEAD_PAIR = 2
PAIR_GROUP = 2
LOG2E = 1.4426950408889634
SEQ_TILE = 512
MOE_TILE = 512
ATTN_TILE = 512
ATTN_KEY_TILE = 512
RUN_ALIGN = 8
RUN_COPY = 128
RUN_BITS = 7
VMEM_LIMIT = 56 * 1024 * 1024

F32 = jnp.float32
BF16 = jnp.bfloat16
NEG_INF = float("-inf")


def _params(*sem):
    return pltpu.CompilerParams(dimension_semantics=sem, vmem_limit_bytes=VMEM_LIMIT)


def _layer_norm(r, g, b):
    mu = jnp.mean(r, axis=-1, keepdims=True)
    d = r - mu
    var = jnp.mean(d * d, axis=-1, keepdims=True)
    return d * lax.rsqrt(var + LN_EPS) * g + b


def _modulate(x, mod_ref):
    return x * (1.0 + mod_ref[0, 1:2, :]) + mod_ref[0, 0:1, :]


def _ada_kernel(c_ref, w_ref, b_ref, o_ref):
    c = c_ref[...]
    cond = c * jax.nn.sigmoid(c)
    o_ref[0] = jnp.dot(cond, w_ref[0], precision=lax.Precision.HIGHEST,
                       preferred_element_type=F32) + b_ref[0]


def _ada_params(c_pad, w, b):
    g, d, n = w.shape
    tn = 1024 if n % 1024 == 0 else n
    return pl.pallas_call(
        _ada_kernel,
        grid=(g, n // tn),
        in_specs=[pl.BlockSpec((8, d), lambda i, j: (0, 0)),
                  pl.BlockSpec((1, d, tn), lambda i, j: (i, 0, j)),
                  pl.BlockSpec((1, 1, tn), lambda i, j: (i, 0, j))],
        out_specs=pl.BlockSpec((1, 8, tn), lambda i, j: (i, 0, j)),
        out_shape=jax.ShapeDtypeStruct((g, 8, n), F32),
        compiler_params=_params("arbitrary", "arbitrary"),
        name="ada_params",
    )(c_pad, w, b)


def _conv_layer_kernel(x_ref, mod_ref, win_ref, wc_ref, wout_ref, g_ref, b_ref, o_ref,
                       carry_ref, a_ref, *, alpha, col_chunk):
    ts, d = x_ref.shape[1], x_ref.shape[2]

    @pl.when(pl.program_id(1) == 0)
    def _():
        carry_ref[...] = jnp.zeros_like(carry_ref)

    x = x_ref[0]
    h = _modulate(x, mod_ref).astype(BF16)
    row = lax.broadcasted_iota(jnp.int32, (ts, col_chunk), 0)
    for c in range(0, d, col_chunk):
        gate_c = jnp.dot(h, win_ref[:, c:c + col_chunk], preferred_element_type=F32)
        u = jnp.dot(h, win_ref[:, 2 * d + c:2 * d + c + col_chunk], preferred_element_type=F32)
        z = gate_c * u
        z_m1 = carry_ref[1:2, c:c + col_chunk]
        z_m2 = carry_ref[0:1, c:c + col_chunk]
        z1 = jnp.where(row == 0, z_m1, pltpu.roll(z, 1, 0))
        z2 = jnp.where(row == 0, z_m2, jnp.where(row == 1, z_m1, pltpu.roll(z, 2, 0)))
        carry_ref[0:2, c:c + col_chunk] = z[ts - 2:ts, :]
        conv = (wc_ref[0:1, c:c + col_chunk] * z2 + wc_ref[1:2, c:c + col_chunk] * z1
                + wc_ref[2:3, c:c + col_chunk] * z)
        gate_b = jnp.dot(h, win_ref[:, d + c:d + c + col_chunk], preferred_element_type=F32)
        a_ref[:, c:c + col_chunk] = (gate_b * conv).astype(BF16)
    y = jnp.dot(a_ref[...], wout_ref[...], preferred_element_type=F32)
    r = alpha * x + (1.0 + mod_ref[0, 2:3, :]) * y
    o_ref[0] = _layer_norm(r, g_ref[...], b_ref[...])


def _conv_layer(x, mod, w_in, w_conv, w_out, ln_g, ln_b, alpha):
    bsz, seq, d = x.shape
    ts = SEQ_TILE
    kern = functools.partial(_conv_layer_kernel, alpha=alpha, col_chunk=512)
    return pl.pallas_call(
        kern,
        grid=(bsz, seq // ts),
        in_specs=[pl.BlockSpec((1, ts, d), lambda b, j: (b, j, 0)),
                  pl.BlockSpec((1, 3, d), lambda b, j: (b, 0, 0)),
                  pl.BlockSpec((d, 3 * d), lambda b, j: (0, 0)),
                  pl.BlockSpec((3, d), lambda b, j: (0, 0)),
                  pl.BlockSpec((d, d), lambda b, j: (0, 0)),
                  pl.BlockSpec((1, d), lambda b, j: (0, 0)),
                  pl.BlockSpec((1, d), lambda b, j: (0, 0))],
        out_specs=pl.BlockSpec((1, ts, d), lambda b, j: (b, j, 0)),
        out_shape=jax.ShapeDtypeStruct(x.shape, F32),
        scratch_shapes=[pltpu.VMEM((8, d), F32), pltpu.VMEM((ts, d), BF16)],
        compiler_params=_params("arbitrary", "arbitrary"),
        name="conv_layer",
    )(x, mod, w_in, w_conv, w_out, ln_g, ln_b)


def _pack_bf16_pairs(h):
    half = h.shape[1] // 2
    lo = lax.bitcast_convert_type(h[:, :half].astype(BF16).astype(F32), jnp.int32)
    hi = lax.bitcast_convert_type(h[:, half:].astype(BF16).astype(F32), jnp.int32)
    return lax.shift_right_logical(lo, 16) | hi


def _unpack_bf16_pairs(w):
    lo = lax.bitcast_convert_type(w << 16, F32).astype(BF16)
    hi = lax.bitcast_convert_type(w & jnp.int32(-65536), F32).astype(BF16)
    return lo, hi


def _router_kernel(x_ref, mod_ref, wr_ref, br_ref, h_ref, route_ref, cnt_ref):
    tr = x_ref.shape[0]
    h = _modulate(x_ref[...], mod_ref)
    h_hi = h.astype(BF16)
    h_ref[...] = h_hi
    h_lo = (h - h_hi.astype(F32)).astype(BF16)
    both = jnp.dot(h_hi, wr_ref[...], preferred_element_type=F32)
    logits = (both[:, :LANES] + both[:, LANES:]
              + jnp.dot(h_lo, wr_ref[:, :LANES], preferred_element_type=F32)) + br_ref[...]
    lane = lax.broadcasted_iota(jnp.int32, (tr, LANES), 1)
    lane_f = lane.astype(F32)
    work = logits
    vals, idxs, sels = [], [], []
    for _ in range(TOP_K):
        m = jnp.max(work, axis=-1, keepdims=True)
        idx = jnp.min(jnp.where(work == m, lane_f, float(LANES)), axis=-1, keepdims=True)
        sel = lane_f == idx
        vals.append(m)
        idxs.append(idx.astype(jnp.int32))
        sels.append(sel)
        work = jnp.where(sel, NEG_INF, work)
    exps = [jnp.exp(v - vals[0]) for v in vals]
    denom = exps[0] + exps[1] + exps[2] + exps[3]
    chosen = sels[0] | sels[1] | sels[2] | sels[3]
    onehot = jnp.where(chosen, 1.0, 0.0).astype(BF16)
    r_i = lax.broadcasted_iota(jnp.int32, (tr, tr), 0)
    c_i = lax.broadcasted_iota(jnp.int32, (tr, tr), 1)
    tri = jnp.where(c_i < r_i, 1.0, 0.0).astype(BF16)
    before = jnp.dot(tri, onehot, preferred_element_type=F32)
    out = jnp.zeros((tr, LANES), jnp.int32)
    for k in range(TOP_K):
        rank = jnp.sum(jnp.where(sels[k], before, 0.0), axis=-1, keepdims=True).astype(jnp.int32)
        wgt = lax.bitcast_convert_type(exps[k] / denom, jnp.int32)
        out = jnp.where(lane == k, idxs[k], out)
        out = jnp.where(lane == TOP_K + k, rank, out)
        out = jnp.where(lane == 2 * TOP_K + k, wgt, out)
    route_ref[...] = out
    counts = jnp.sum(onehot.astype(F32), axis=0, keepdims=True).astype(jnp.int32)
    cnt_ref[0] = jnp.broadcast_to(counts, (8, LANES))


def _router(x2, mod, w_r, b_r, seq):
    t, d = x2.shape
    tr = SEQ_TILE
    per_b = seq // tr
    return pl.pallas_call(
        _router_kernel,
        grid=(t // tr,),
        in_specs=[pl.BlockSpec((tr, d), lambda i: (i, 0)),
                  pl.BlockSpec((1, 3, d), lambda i: (i // per_b, 0, 0)),
                  pl.BlockSpec((d, 2 * LANES), lambda i: (0, 0)),
                  pl.BlockSpec((1, LANES), lambda i: (0, 0))],
        out_specs=[pl.BlockSpec((tr, d), lambda i: (i, 0)),
                   pl.BlockSpec((tr, LANES), lambda i: (i, 0)),
                   pl.BlockSpec((1, 8, LANES), lambda i: (i, 0, 0))],
        out_shape=[jax.ShapeDtypeStruct((t, d), BF16),
                   jax.ShapeDtypeStruct((t, LANES), jnp.int32),
                   jax.ShapeDtypeStruct((t // tr, 8, LANES), jnp.int32)],
        compiler_params=_params("arbitrary"),
        name="moe_router",
    )(x2, mod, w_r, b_r)


def _dispatch_kernel(meta_ref, prev_ref, fill_ref, offdst_ref, route_ref, h_ref, xs_hbm, pos_ref, xs_buf, sem,
                     *, n_exp):
    tr, d = h_ref.shape
    n_rows = xs_buf.shape[1] - RUN_COPY
    i = pl.program_id(0)
    buf = i % 2
    sizes = [RUN_ALIGN << b for b in range(RUN_BITS)]

    @pl.when(i == 0)
    def _():
        xs_buf[0, n_rows:, :] = jnp.zeros((RUN_COPY, d // 2), jnp.int32)
        xs_buf[1] = jnp.zeros(xs_buf.shape[1:], jnp.int32)
        for e in range(n_exp):
            start = pl.multiple_of(fill_ref[0, 0, e], RUN_ALIGN)
            length = fill_ref[0, 0, LANES // 4 + e]
            for size in reversed(sizes):
                done = pl.multiple_of((length // (2 * size)) * (2 * size), RUN_ALIGN)

                @pl.when((length & size) != 0)
                def _():
                    pltpu.make_async_copy(xs_buf.at[1, pl.ds(0, size)],
                                          xs_hbm.at[pl.ds(start + done, size)], sem).start()
        spare = pl.multiple_of(fill_ref[0, 0, LANES // 2], tr)

        def fill_spare(n, carry):
            pltpu.make_async_copy(xs_buf.at[1, pl.ds(0, tr)],
                                  xs_hbm.at[pl.ds(pl.multiple_of(spare + n * tr, tr), tr)], sem).start()
            return carry
        lax.fori_loop(0, fill_ref[0, 0, LANES // 2 + 1], fill_spare, 0)
        for b, size in enumerate(sizes):
            def wait_piece(n, carry, size=size):
                pltpu.make_async_copy(xs_buf.at[1, pl.ds(0, size)], xs_hbm.at[pl.ds(0, size)], sem).wait()
                return carry
            lax.fori_loop(0, fill_ref[0, 0, LANES // 2 + 2 + b], wait_piece, 0)

        def wait_spare(n, carry):
            pltpu.make_async_copy(xs_buf.at[1, pl.ds(0, tr)], xs_hbm.at[pl.ds(0, tr)], sem).wait()
            return carry
        lax.fori_loop(0, fill_ref[0, 0, LANES // 2 + 1], wait_spare, 0)

    route = route_ref[...]
    lane = lax.broadcasted_iota(jnp.int32, (tr, LANES), 1)
    off_row = offdst_ref[0, 0:1, :]
    pos = jnp.full((tr, LANES), -1.0, F32)
    pos_out = jnp.zeros((tr, LANES), jnp.int32)
    for k in range(TOP_K):
        sel = lane == route[:, k:k + 1]
        rank = route[:, TOP_K + k:TOP_K + k + 1].astype(F32)
        pos_k = jnp.sum(jnp.where(sel, off_row, 0.0), axis=-1, keepdims=True) + rank
        pos = jnp.where(lane == k, pos_k, pos)
        pos_out = jnp.where(lane == k, pos_k.astype(jnp.int32), pos_out)
    pos_ref[...] = pos_out
    pos_t = pos.T
    h = h_ref[...]
    chunk = n_rows // 4
    for c in range(4):
        slot = (lax.broadcasted_iota(jnp.int32, (chunk, tr), 0) + chunk * c).astype(F32)
        hit = slot == pos_t[0:1, :]
        for k in range(1, TOP_K):
            hit = hit | (slot == pos_t[k:k + 1, :])
        onehot = jnp.where(hit, 1.0, 0.0).astype(BF16)
        rows = jnp.dot(onehot, h, preferred_element_type=F32)
        xs_buf[buf, chunk * c:chunk * (c + 1), :] = _pack_bf16_pairs(rows)

    def wait_copies(count):
        def wait_one(n, carry):
            pltpu.make_async_copy(xs_buf.at[0, pl.ds(0, RUN_COPY)], xs_hbm.at[pl.ds(0, RUN_COPY)], sem).wait()
            return carry
        lax.fori_loop(0, count, wait_one, 0)

    @pl.when(i > 0)
    def _():
        wait_copies(prev_ref[0, 0, 3 * LANES // 4])

    for e in range(n_exp):
        src = pl.multiple_of(meta_ref[0, 0, e], RUN_ALIGN)
        dst = pl.multiple_of(meta_ref[0, 0, LANES // 4 + e], RUN_ALIGN)
        length = meta_ref[0, 0, LANES // 2 + e]
        for q in range(tr // RUN_COPY):
            @pl.when(length > RUN_COPY * q)
            def _():
                pltpu.make_async_copy(xs_buf.at[buf, pl.ds(src + RUN_COPY * q, RUN_COPY)],
                                      xs_hbm.at[pl.ds(dst + RUN_COPY * q, RUN_COPY)], sem).start()

    @pl.when(i == pl.num_programs(0) - 1)
    def _():
        wait_copies(meta_ref[0, 0, 3 * LANES // 4])


def _dispatch(meta, fill, offdst, route, h, n_slots, n_exp):
    t, d = h.shape
    tr = SEQ_TILE
    n_rows = TOP_K * tr + n_exp * RUN_ALIGN
    return pl.pallas_call(
        functools.partial(_dispatch_kernel, n_exp=n_exp),
        grid=(t // tr,),
        in_specs=[pl.BlockSpec((1, 1, LANES), lambda i: (i, 0, 0), memory_space=pltpu.SMEM),
                  pl.BlockSpec((1, 1, LANES), lambda i: (jnp.maximum(i - 1, 0), 0, 0), memory_space=pltpu.SMEM),
                  pl.BlockSpec((1, 1, LANES), lambda i: (0, 0, 0), memory_space=pltpu.SMEM),
                  pl.BlockSpec((1, 8, LANES), lambda i: (i, 0, 0)),
                  pl.BlockSpec((tr, LANES), lambda i: (i, 0)),
                  pl.BlockSpec((tr, d), lambda i: (i, 0))],
        out_specs=[pl.BlockSpec(memory_space=pl.ANY),
                   pl.BlockSpec((tr, LANES), lambda i: (i, 0))],
        out_shape=[jax.ShapeDtypeStruct((n_slots, d // 2), jnp.int32),
                   jax.ShapeDtypeStruct((t, LANES), jnp.int32)],
        scratch_shapes=[pltpu.VMEM((2, n_rows + RUN_COPY, d // 2), jnp.int32), pltpu.SemaphoreType.DMA(())],
        compiler_params=_params("arbitrary"),
        name="moe_dispatch",
    )(meta, meta, fill, offdst, route, h)


def _moe_gemm_kernel(te_ref, valid_ref, xs_ref, wgu_ref, bgu_ref, wd_ref, bd_ref, y_ref, wgu_bf, wd_bf):
    half = xs_ref.shape[1]
    f = wd_bf.shape[0]
    i = pl.program_id(0)

    @pl.when((i == 0) | (te_ref[i] != te_ref[jnp.maximum(i - 1, 0)]))
    def _():
        wgu_bf[...] = wgu_ref[0, 0].astype(BF16)
        wd_bf[...] = wd_ref[0, 0].astype(BF16)

    @pl.when(valid_ref[i] == 1)
    def _():
        lo, hi = _unpack_bf16_pairs(xs_ref[...])
        gu = (jnp.dot(lo, wgu_bf[:half, :], preferred_element_type=F32)
              + jnp.dot(hi, wgu_bf[half:, :], preferred_element_type=F32) + bgu_ref[0, 0])
        g = jnp.minimum(gu[:, :f], SWIGLU_LIMIT)
        u = jnp.clip(gu[:, f:], -SWIGLU_LIMIT, SWIGLU_LIMIT)
        a = g * jax.nn.sigmoid(SWIGLU_ALPHA * g) * (u + 1.0)
        y = jnp.dot(a.astype(BF16), wd_bf[...], preferred_element_type=F32) + bd_ref[0, 0]
        y_ref[...] = _pack_bf16_pairs(y)

    @pl.when(valid_ref[i] == 0)
    def _():
        y_ref[...] = jnp.zeros_like(y_ref)


def _moe_gemm(layer, tile_expert, tile_valid, xs, w_gu, b_gu, w_down, b_down):
    n_tiles = tile_expert.shape[0]
    tm = MOE_TILE
    _, _, d, f2 = w_gu.shape
    f = f2 // 2
    grid_spec = pltpu.PrefetchScalarGridSpec(
        num_scalar_prefetch=2,
        grid=(n_tiles,),
        in_specs=[pl.BlockSpec((tm, d // 2), lambda i, te, tv: (i, 0)),
                  pl.BlockSpec((1, 1, d, f2), lambda i, te, tv: (layer, te[i], 0, 0)),
                  pl.BlockSpec((1, 1, 1, f2), lambda i, te, tv: (layer, te[i], 0, 0)),
                  pl.BlockSpec((1, 1, f, d), lambda i, te, tv: (layer, te[i], 0, 0)),
                  pl.BlockSpec((1, 1, 1, d), lambda i, te, tv: (layer, te[i], 0, 0))],
        out_specs=pl.BlockSpec((tm, d // 2), lambda i, te, tv: (i, 0)),
        scratch_shapes=[pltpu.VMEM((d, f2), BF16), pltpu.VMEM((f, d), BF16)],
    )
    return pl.pallas_call(
        _moe_gemm_kernel,
        grid_spec=grid_spec,
        out_shape=jax.ShapeDtypeStruct((n_tiles * tm, d // 2), jnp.int32),
        compiler_params=_params("arbitrary"),
        name="moe_experts",
    )(tile_expert, tile_valid, xs, w_gu, b_gu, w_down, b_down)


def _combine_kernel(meta_ref, y_hbm, pos_ref, x_ref, mod_ref, route_ref, g_ref, b_ref, o_ref, ybuf, sem,
                    *, alpha, n_exp):
    tr = x_ref.shape[0]
    n_rows = ybuf.shape[0]
    sizes = [RUN_ALIGN << b for b in range(RUN_BITS)]

    @pl.when(pl.program_id(0) == 0)
    def _():
        ybuf[...] = jnp.zeros_like(ybuf)

    for e in range(n_exp):
        off = pl.multiple_of(meta_ref[0, 0, e], RUN_ALIGN)
        src = pl.multiple_of(meta_ref[0, 0, LANES // 4 + e], RUN_ALIGN)
        length = meta_ref[0, 0, LANES // 2 + e]
        for size in reversed(sizes):
            done = pl.multiple_of((length // (2 * size)) * (2 * size), RUN_ALIGN)

            @pl.when((length & size) != 0)
            def _():
                pltpu.make_async_copy(y_hbm.at[pl.ds(src + done, size)],
                                      ybuf.at[pl.ds(off + done, size)], sem).start()

    pos = pos_ref[...]
    wts = lax.bitcast_convert_type(route_ref[...], F32)
    n_chunks = 3
    width = n_rows // n_chunks
    spread = []
    for c in range(n_chunks):
        slot = lax.broadcasted_iota(jnp.int32, (tr, width), 1) + width * c
        w_c = jnp.zeros((tr, width), F32)
        for k in range(TOP_K):
            w_c = jnp.where(slot == pos[:, k:k + 1], wts[:, 2 * TOP_K + k:2 * TOP_K + k + 1], w_c)
        spread.append(w_c.astype(BF16))

    for b, size in enumerate(sizes):
        def wait_one(n, carry, size=size):
            pltpu.make_async_copy(y_hbm.at[pl.ds(0, size)], ybuf.at[pl.ds(0, size)], sem).wait()
            return carry
        lax.fori_loop(0, meta_ref[0, 0, 3 * LANES // 4 + 1 + b], wait_one, 0)

    sub_lo = sub_hi = None
    for c in range(n_chunks):
        lo, hi = _unpack_bf16_pairs(ybuf[width * c:width * (c + 1), :])
        part_lo = jnp.dot(spread[c], lo, preferred_element_type=F32)
        part_hi = jnp.dot(spread[c], hi, preferred_element_type=F32)
        sub_lo = part_lo if sub_lo is None else sub_lo + part_lo
        sub_hi = part_hi if sub_hi is None else sub_hi + part_hi
    sub = jnp.concatenate([sub_lo, sub_hi], axis=1)
    x = x_ref[...]
    r = alpha * x + (1.0 + mod_ref[0, 2:3, :]) * sub
    o_ref[...] = _layer_norm(r, g_ref[...], b_ref[...])


def _combine(meta, y, pos, x2, mod, route, ln_g, ln_b, alpha, seq, n_exp):
    t, d = x2.shape
    tr = SEQ_TILE
    per_b = seq // tr
    n_rows = TOP_K * tr + n_exp * RUN_ALIGN
    kern = functools.partial(_combine_kernel, alpha=alpha, n_exp=n_exp)
    return pl.pallas_call(
        kern,
        grid=(t // tr,),
        in_specs=[pl.BlockSpec((1, 1, LANES), lambda i: (i, 0, 0), memory_space=pltpu.SMEM),
                  pl.BlockSpec(memory_space=pl.ANY),
                  pl.BlockSpec((tr, LANES), lambda i: (i, 0)),
                  pl.BlockSpec((tr, d), lambda i: (i, 0)),
                  pl.BlockSpec((1, 3, d), lambda i: (i // per_b, 0, 0)),
                  pl.BlockSpec((tr, LANES), lambda i: (i, 0)),
                  pl.BlockSpec((1, d), lambda i: (0, 0)),
                  pl.BlockSpec((1, d), lambda i: (0, 0))],
        out_specs=pl.BlockSpec((tr, d), lambda i: (i, 0)),
        out_shape=jax.ShapeDtypeStruct((t, d), F32),
        scratch_shapes=[pltpu.VMEM((n_rows, d // 2), jnp.int32), pltpu.SemaphoreType.DMA(())],
        compiler_params=_params("arbitrary"),
        name="moe_combine",
    )(meta, y, pos, x2, mod, route, ln_g, ln_b)


def _moe_layer(layer, x2, mod, w_r, b_r, w_gu, b_gu, w_down, b_down, ln_g, ln_b, alpha, seq):
    t, d = x2.shape
    n_exp = w_gu.shape[1]
    tm = MOE_TILE
    n_tok_tiles = t // SEQ_TILE
    assert n_exp <= LANES // 4 and SEQ_TILE % RUN_COPY == 0 and SEQ_TILE == tm
    assert RUN_ALIGN << (RUN_BITS - 1) == SEQ_TILE and RUN_COPY + tm - 1 < 2 * SEQ_TILE
    h, route, cnt = _router(x2, mod, w_r, b_r, seq)
    runs = (cnt[:, 0, :n_exp] + RUN_ALIGN - 1) // RUN_ALIGN * RUN_ALIGN
    run_off = jnp.cumsum(runs, axis=1) - runs
    run_base = jnp.cumsum(runs, axis=0) - runs
    rows = jnp.sum(runs, axis=0)
    region = (rows + RUN_COPY + tm - 1) // tm * tm
    region_end = jnp.cumsum(region)
    region_start = region_end - region
    run_dst = region_start[None, :] + run_base
    n_tiles = -(-(t * TOP_K + n_tok_tiles * n_exp * (RUN_ALIGN - 1) + n_exp * (RUN_COPY + tm - 1)) // tm)
    tile_start = jnp.arange(n_tiles, dtype=jnp.int32) * tm
    tile_expert = jnp.minimum(jnp.sum(region_end[None, :] <= tile_start[:, None], axis=1),
                              n_exp - 1).astype(jnp.int32)
    tile_valid = ((tile_start - region_start[tile_expert] < rows[tile_expert])
                  & (tile_start < region_end[-1])).astype(jnp.int32)
    pad = jnp.zeros((n_tok_tiles, LANES // 4 - n_exp), jnp.int32)
    n_copies = jnp.sum((runs + RUN_COPY - 1) // RUN_COPY, axis=1, keepdims=True)
    pieces = jnp.stack([jnp.sum((runs // (RUN_ALIGN << b)) % 2, axis=1) for b in range(RUN_BITS)], axis=1)
    meta = jnp.concatenate([run_off, pad, run_dst, pad, runs, pad, n_copies, pieces,
                            jnp.zeros((n_tok_tiles, LANES // 4 - 1 - RUN_BITS), jnp.int32)],
                           axis=1).astype(jnp.int32)[:, None, :]
    wide = lambda v: jnp.pad(v.astype(F32), ((0, 0), (0, LANES - n_exp)))
    offdst = jnp.concatenate([wide(run_off)[:, None], wide(run_dst)[:, None],
                              jnp.zeros((n_tok_tiles, 6, LANES), F32)], axis=1)
    tail_len = region - rows
    tail_pieces = jnp.stack([jnp.sum((tail_len // (RUN_ALIGN << b)) % 2) for b in range(RUN_BITS)])
    spare = jnp.stack([region_end[-1], (n_tiles * tm - region_end[-1]) // SEQ_TILE])
    fill = jnp.concatenate([region_start + rows, pad[0], tail_len, pad[0], spare, tail_pieces,
                            jnp.zeros((LANES // 2 - 2 - RUN_BITS,), jnp.int32)]).astype(jnp.int32)
    xs, pos = _dispatch(meta, fill[None, None, :], offdst, route, h, n_tiles * tm, n_exp)
    y = _moe_gemm(layer, tile_expert, tile_valid, xs, w_gu, b_gu, w_down, b_down)
    return _combine(meta, y, pos, x2, mod, route, ln_g, ln_b, alpha, seq, n_exp)


def _split3(v):
    p0 = v.astype(BF16)
    r1 = v - p0.astype(F32)
    p1 = r1.astype(BF16)
    p2 = (r1 - p1.astype(F32)).astype(BF16)
    return p0, p1, p2


AUG_STRIDE = 8
AUG_PARTS = 3


def _aug_constants(d):
    n_pairs = d // LANES
    sel = np.zeros((AUG_PARTS, LANES, d), np.float32)
    ones = np.zeros((1, d), np.float32)
    for p in range(n_pairs):
        for hd in range(HEAD_PAIR):
            base = LANES * p + AUG_STRIDE * hd
            for part in range(AUG_PARTS):
                sel[part, HEAD_PAIR * p + hd, base + part] = 1.0
                ones[0, base + AUG_PARTS + part] = 1.0
    return jnp.asarray(sel, BF16), jnp.asarray(ones, F32)


def _kv_kernel(x_ref, mod_ref, wk_ref, wv_ref, wf_ref, bf_ref, sel_ref, ones_ref,
               k_ref, kaug_ref, vt_ref, cum_ref, carry_ref):
    ts = x_ref.shape[1]
    n_heads = cum_ref.shape[2]
    n_pairs = kaug_ref.shape[1]

    @pl.when(pl.program_id(1) == 0)
    def _():
        carry_ref[...] = jnp.zeros_like(carry_ref)

    h = _modulate(x_ref[0], mod_ref).astype(BF16)
    k_ref[0] = jnp.dot(h, wk_ref[...], preferred_element_type=F32).astype(BF16)
    vt = jnp.dot(h, wv_ref[...], preferred_element_type=F32).T.astype(BF16)
    tk = vt_ref.shape[4]
    for p in range(n_pairs):
        for c in range(ts // tk):
            vt_ref[0, p, c] = vt[LANES * p:LANES * (p + 1), tk * c:tk * (c + 1)]
    fz = jnp.dot(h, wf_ref[...], preferred_element_type=F32) + bf_ref[...]
    log_f = jnp.minimum(fz, 0.0) - jnp.log1p(jnp.exp(-jnp.abs(fz)))
    r_i = lax.broadcasted_iota(jnp.int32, (ts, ts), 0)
    c_i = lax.broadcasted_iota(jnp.int32, (ts, ts), 1)
    tri = jnp.where(c_i <= r_i, 1.0, 0.0).astype(BF16)
    cum = carry_ref[0:1, :]
    for part in _split3(log_f):
        cum = cum + jnp.dot(tri, part, preferred_element_type=F32)
    carry_ref[0:1, :] = cum[ts - 1:ts, :]
    cum = cum * LOG2E
    cum_ref[0] = cum[:, :n_heads]
    aug = ones_ref[...]
    for i, part in enumerate(_split3(cum)):
        aug = aug + jnp.dot(part, sel_ref[i], preferred_element_type=F32)
    aug = aug.astype(BF16)
    for p in range(n_pairs):
        kaug_ref[0, p] = aug[:, LANES * p:LANES * (p + 1)]


def _shared_kv(x, mod, w_k, w_v, w_f, b_f):
    bsz, seq, d = x.shape
    ts = ATTN_TILE
    tk = ATTN_KEY_TILE
    n_pairs = d // LANES
    sel, ones = _aug_constants(d)
    return pl.pallas_call(
        _kv_kernel,
        grid=(bsz, seq // ts),
        in_specs=[pl.BlockSpec((1, ts, d), lambda b, j: (b, j, 0)),
                  pl.BlockSpec((1, 2, d), lambda b, j: (b, 0, 0)),
                  pl.BlockSpec((d, d), lambda b, j: (0, 0)),
                  pl.BlockSpec((d, d), lambda b, j: (0, 0)),
                  pl.BlockSpec((d, LANES), lambda b, j: (0, 0)),
                  pl.BlockSpec((1, LANES), lambda b, j: (0, 0)),
                  pl.BlockSpec((AUG_PARTS, LANES, d), lambda b, j: (0, 0, 0)),
                  pl.BlockSpec((1, d), lambda b, j: (0, 0))],
        out_specs=[pl.BlockSpec((1, ts, d), lambda b, j: (b, j, 0)),
                   pl.BlockSpec((1, n_pairs, ts, LANES), lambda b, j: (b, 0, j, 0)),
                   pl.BlockSpec((1, n_pairs, ts // tk, LANES, tk), lambda b, j: (b, 0, j, 0, 0)),
                   pl.BlockSpec((1, ts, N_HEADS), lambda b, j: (b, j, 0))],
        out_shape=[jax.ShapeDtypeStruct((bsz, seq, d), BF16),
                   jax.ShapeDtypeStruct((bsz, n_pairs, seq, LANES), BF16),
                   jax.ShapeDtypeStruct((bsz, n_pairs, seq // tk, LANES, tk), BF16),
                   jax.ShapeDtypeStruct((bsz, seq, N_HEADS), F32)],
        scratch_shapes=[pltpu.VMEM((8, LANES), F32)],
        compiler_params=_params("arbitrary", "arbitrary"),
        name="shared_kv",
    )(x, mod, w_k, w_v, w_f, b_f, sel, ones)


def _q_proj_kernel(x_ref, mod_ref, wq_ref, q_ref, *, scale):
    h = _modulate(x_ref[0], mod_ref).astype(BF16)
    q_ref[0] = (jnp.dot(h, wq_ref[...], preferred_element_type=F32) * scale).astype(BF16)


def _q_proj(x, mod, w_q, scale):
    bsz, seq, d = x.shape
    ts = SEQ_TILE
    return pl.pallas_call(
        functools.partial(_q_proj_kernel, scale=scale),
        grid=(bsz, seq // ts),
        in_specs=[pl.BlockSpec((1, ts, d), lambda b, j: (b, j, 0)),
                  pl.BlockSpec((1, 3, d), lambda b, j: (b, 0, 0)),
                  pl.BlockSpec((d, d), lambda b, j: (0, 0))],
        out_specs=pl.BlockSpec((1, ts, d), lambda b, j: (b, j, 0)),
        out_shape=jax.ShapeDtypeStruct((bsz, seq, d), BF16),
        compiler_params=_params("arbitrary", "arbitrary"),
        name="q_proj",
    )(x, mod, w_q)


def _attn_kernel(q_ref, k_ref, kaug_ref, vt_ref, cq_ref, o_ref, acc_ref, m_ref, l_ref, s_ref):
    tq = q_ref.shape[1]
    tk = vt_ref.shape[4]
    per_q = tq // tk
    head_dim = LANES // HEAD_PAIR
    i = pl.program_id(2)
    q_t = q_ref[0].astype(F32).T
    row = lax.broadcasted_iota(jnp.int32, (LANES, tq), 0)
    rhs = []
    for g in range(PAIR_GROUP):
        q_pair = q_t[LANES * g:LANES * (g + 1), :]
        for hd in range(HEAD_PAIR):
            own = (row >= head_dim * hd) & (row < head_dim * (hd + 1))
            parts = _split3(cq_ref[0, g, hd:hd + 1, :])
            base = AUG_STRIDE * hd
            aug = jnp.where((row >= base) & (row < base + AUG_PARTS), -1.0, 0.0)
            for n, part in enumerate(parts):
                aug = jnp.where(row == base + AUG_PARTS + n, part.astype(F32), aug)
            rhs.append(jnp.concatenate([jnp.where(own, q_pair, 0.0).astype(BF16), aug.astype(BF16)],
                                       axis=0))
    ones_rows = jnp.ones((BF16_ROWS, tk), BF16)
    acc_ref[...] = jnp.zeros_like(acc_ref)
    m_ref[...] = jnp.full_like(m_ref, NEG_INF)
    l_ref[...] = jnp.zeros_like(l_ref)

    def keys_of(j, g):
        start = pl.multiple_of(j * tk, tk)
        return jnp.concatenate([k_ref[0, pl.ds(start, tk), LANES * g:LANES * (g + 1)],
                                kaug_ref[0, g, pl.ds(start, tk), :]], axis=1)

    s_ref[...] = jnp.dot(keys_of(0, 0), rhs[0], preferred_element_type=F32)

    def chunk(j, diagonal, offset=0, has_next=True):
        keys = [keys_of(j, g) for g in range(PAIR_GROUP)]

        def masked(s_t):
            if diagonal:
                k_i = lax.broadcasted_iota(jnp.int32, (tk, tq), 0) + offset
                q_i = lax.broadcasted_iota(jnp.int32, (tk, tq), 1)
                s_t = jnp.where(k_i <= q_i, s_t, NEG_INF)
            return s_t

        n_heads = PAIR_GROUP * HEAD_PAIR
        s_next = s_ref[...]
        for n in range(n_heads):
            g, hd = divmod(n, HEAD_PAIR)
            s_t = masked(s_next)
            if n + 1 < n_heads:
                s_next = jnp.dot(keys[(n + 1) // HEAD_PAIR], rhs[n + 1], preferred_element_type=F32)
            elif has_next:
                s_ref[...] = jnp.dot(keys_of(j + 1, 0), rhs[0], preferred_element_type=F32)
            m_old = m_ref[n:n + 1, :]
            m_new = jnp.maximum(m_old, jnp.max(s_t, axis=0, keepdims=True))
            a = jnp.exp2(m_old - m_new)
            p_t = jnp.exp2(s_t - m_new).astype(BF16)
            m_ref[n:n + 1, :] = m_new
            rows = slice(head_dim * n, head_dim * (n + 1))
            lhs = jnp.concatenate([vt_ref[0, g, j, head_dim * hd:head_dim * (hd + 1), :], ones_rows], axis=0)
            pv = jnp.dot(lhs, p_t, preferred_element_type=F32)
            l_ref[n:n + 1, :] = a * l_ref[n:n + 1, :] + pv[head_dim:head_dim + 1, :]
            acc_ref[rows, :] = acc_ref[rows, :] * a + pv[:head_dim, :]

    def body(j, carry):
        chunk(j, False)
        return carry
    lax.fori_loop(0, i * per_q, body, 0)
    for sub in range(per_q):
        chunk(i * per_q + sub, True, sub * tk, has_next=sub + 1 < per_q)
    inv = jnp.concatenate([jnp.broadcast_to(1.0 / l_ref[n:n + 1, :], (head_dim, tq))
                           for n in range(PAIR_GROUP * HEAD_PAIR)], axis=0)
    o_ref[0] = (acc_ref[...] * inv).T.astype(o_ref.dtype)


def _attention(q, k, kaug, v_t, cq):
    bsz, seq, d = q.shape
    width = PAIR_GROUP * LANES
    tq = ATTN_TILE
    return pl.pallas_call(
        _attn_kernel,
        grid=(bsz, d // width, seq // tq),
        in_specs=[pl.BlockSpec((1, tq, width), lambda b, p, i: (b, i, p)),
                  pl.BlockSpec((1, seq, width), lambda b, p, i: (b, 0, p)),
                  pl.BlockSpec((1, PAIR_GROUP, seq, LANES), lambda b, p, i: (b, p, 0, 0)),
                  pl.BlockSpec((1, PAIR_GROUP) + v_t.shape[2:], lambda b, p, i: (b, p, 0, 0, 0)),
                  pl.BlockSpec((1, PAIR_GROUP, HEAD_PAIR, tq), lambda b, p, i: (b, p, 0, i))],
        out_specs=pl.BlockSpec((1, tq, width), lambda b, p, i: (b, i, p)),
        out_shape=jax.ShapeDtypeStruct((bsz, seq, d), BF16),
        scratch_shapes=[pltpu.VMEM((width, tq), F32),
                        pltpu.VMEM((8, tq), F32),
                        pltpu.VMEM((8, tq), F32),
                        pltpu.VMEM((v_t.shape[4], tq), F32)],
        compiler_params=_params("arbitrary", "arbitrary", "arbitrary"),
        name="fox_attention",
    )(q, k, kaug, v_t, cq)


def _out_proj_kernel(o_ref, x_ref, mod_ref, wo_ref, g_ref, b_ref, out_ref, *, alpha):
    y = jnp.dot(o_ref[0], wo_ref[...], preferred_element_type=F32)
    r = alpha * x_ref[0] + (1.0 + mod_ref[0, 2:3, :]) * y
    out_ref[0] = _layer_norm(r, g_ref[...], b_ref[...])


def _out_proj(o, x, mod, w_o, ln_g, ln_b, alpha):
    bsz, seq, d = x.shape
    ts = SEQ_TILE
    return pl.pallas_call(
        functools.partial(_out_proj_kernel, alpha=alpha),
        grid=(bsz, seq // ts),
        in_specs=[pl.BlockSpec((1, ts, d), lambda b, j: (b, j, 0)),
                  pl.BlockSpec((1, ts, d), lambda b, j: (b, j, 0)),
                  pl.BlockSpec((1, 3, d), lambda b, j: (b, 0, 0)),
                  pl.BlockSpec((d, d), lambda b, j: (0, 0)),
                  pl.BlockSpec((1, d), lambda b, j: (0, 0)),
                  pl.BlockSpec((1, d), lambda b, j: (0, 0))],
        out_specs=pl.BlockSpec((1, ts, d), lambda b, j: (b, j, 0)),
        out_shape=jax.ShapeDtypeStruct(x.shape, F32),
        compiler_params=_params("arbitrary", "arbitrary"),
        name="attn_out_proj",
    )(o, x, mod, w_o, ln_g, ln_b)


def kernel(x, c, conv_w_in, conv_w, conv_w_out, kv_ada_w, kv_ada_b, w_kvf, b_f, attn_w_q, attn_w_o,
           ada_w, ada_b, ln_g, ln_b, router_w, router_b, exp_w_gu, exp_b_gu, exp_w_down, exp_b_down):
    bsz, seq, d = x.shape
    depth = ada_w.shape[0]
    n_conv = conv_w_in.shape[0]
    n_exp = router_w.shape[-1]
    alpha = (2.0 * depth) ** 0.25
    head_dim = d // N_HEADS
    assert head_dim * HEAD_PAIR == LANES and seq % SEQ_TILE == 0 and seq % ATTN_TILE == 0

    c_pad = jnp.pad(c, ((0, 8 - bsz), (0, 0)))
    mods = _ada_params(c_pad, ada_w.reshape(depth * 2, d, 3 * d), ada_b.reshape(depth * 2, 1, 3 * d))
    mods = mods[:, :bsz, :].reshape(depth, 2, bsz, 3, d)
    kv_mod = _ada_params(c_pad, kv_ada_w[None], kv_ada_b[None, None])[0, :bsz].reshape(bsz, 2, d)

    w_r = jnp.pad(router_w, ((0, 0), (0, 0), (0, LANES - n_exp)))
    w_r_hi = w_r.astype(BF16)
    w_r = jnp.concatenate([w_r_hi, (w_r - w_r_hi.astype(F32)).astype(BF16)], axis=-1)
    b_r = jnp.pad(router_b, ((0, 0), (0, LANES - n_exp)), constant_values=-1e30)[:, None, :]
    b_gu = exp_b_gu[:, :, None, :]
    b_dn = exp_b_down[:, :, None, :]
    k = kaug = v_t = cq = None
    for l in range(depth):
        g0, b0 = ln_g[l, 0][None], ln_b[l, 0][None]
        if l < n_conv:
            x = _conv_layer(x, mods[l, 0], conv_w_in[l].astype(BF16), conv_w[l],
                            conv_w_out[l].astype(BF16), g0, b0, alpha)
        else:
            j = l - n_conv
            q = _q_proj(x, mods[l, 0], attn_w_q[j].astype(BF16), head_dim ** -0.5 * LOG2E)
            o = _attention(q, k, kaug, v_t, cq)
            x = _out_proj(o, x, mods[l, 0], attn_w_o[j].astype(BF16), g0, b0, alpha)
        x = _moe_layer(l, x.reshape(bsz * seq, d), mods[l, 1], w_r[l], b_r[l], exp_w_gu, b_gu,
                       exp_w_down, b_dn,
                       ln_g[l, 1][None], ln_b[l, 1][None], alpha, seq).reshape(bsz, seq, d)
        if l == n_conv - 1:
            w_f = jnp.pad(w_kvf[:, 2 * d:], ((0, 0), (0, LANES - N_HEADS))).astype(BF16)
            bias_f = jnp.pad(b_f, (0, LANES - N_HEADS))[None]
            k, kaug, v_t, cum = _shared_kv(x, kv_mod, w_kvf[:, :d].astype(BF16),
                                           w_kvf[:, d:2 * d].astype(BF16), w_f, bias_f)
            cq = cum.reshape(bsz, seq, N_HEADS // HEAD_PAIR, HEAD_PAIR).transpose(0, 2, 3, 1)
    return x
```
